```python
import jax, jax.numpy as jnp
from jax import lax
import numpy as np

D_MODEL = 2048
BATCH = 4
SEQ = 4096
DEPTH = 2

HEAD_DIM = 64
SB_HEADS = D_MODEL // (4 * HEAD_DIM)
MOBA_HEADS = D_MODEL // (4 * HEAD_DIM)
SWA_Q_HEADS = D_MODEL // (2 * HEAD_DIM)
SWA_KV_HEADS = SWA_Q_HEADS // 8
SB_WIDTH = SB_HEADS * HEAD_DIM
MOBA_WIDTH = MOBA_HEADS * HEAD_DIM
SWA_Q_WIDTH = SWA_Q_HEADS * HEAD_DIM
SWA_KV_WIDTH = SWA_KV_HEADS * HEAD_DIM
MIX_WIDTH = SB_WIDTH + MOBA_WIDTH + SWA_Q_WIDTH
SPLIT_SIZES = (SB_WIDTH, SB_WIDTH, SB_WIDTH,
               MOBA_WIDTH, MOBA_WIDTH, MOBA_WIDTH,
               SWA_Q_WIDTH, SWA_KV_WIDTH, SWA_KV_WIDTH)
IN_WIDTH = sum(SPLIT_SIZES)
D_FF = 4 * D_MODEL
Q_BLOCK = 128
MOBA_BLOCK = 256
MOBA_TOPK = 3
WINDOW = 128
ROPE_THETA = 10000.0
EPS = 1e-6
NEG = -1e30

kernel_name = "hymba_style_stickbreak_moba_swa_sink_hybrid"


def rmsnorm(x, g):
    xf = x.astype(jnp.float32)
    y = xf * lax.rsqrt(jnp.mean(xf * xf, axis=-1, keepdims=True) + EPS)
    return (y * g.astype(jnp.float32)).astype(x.dtype)


def split_heads(t, n_heads):
    B, S, _ = t.shape
    return t.reshape(B, S, n_heads, HEAD_DIM).transpose(0, 2, 1, 3)


def merge_heads(t):
    B, H, S, d = t.shape
    return t.transpose(0, 2, 1, 3).reshape(B, S, H * d)


def rope_tables(S):
    half = HEAD_DIM // 2
    inv_freq = ROPE_THETA ** (-jnp.arange(half, dtype=jnp.float32) * 2.0 / HEAD_DIM)
    ang = jnp.arange(S, dtype=jnp.float32)[:, None] * inv_freq[None, :]
    return jnp.cos(ang), jnp.sin(ang)


def apply_rope(t, cos, sin):
    half = HEAD_DIM // 2
    tf = t.astype(jnp.float32)
    t1, t2 = tf[..., :half], tf[..., half:]
    out = jnp.concatenate([t1 * cos - t2 * sin, t2 * cos + t1 * sin], axis=-1)
    return out.astype(t.dtype)


def stick_breaking_attention(q, k, v):
    B, H, S, d = q.shape
    nq = S // Q_BLOCK
    scale = d ** -0.5
    qb = q.reshape(B, H, nq, Q_BLOCK, d).transpose(2, 0, 1, 3, 4)
    kpos = jnp.arange(S)

    def block(args):
        qi, i = args
        qpos = i * Q_BLOCK + jnp.arange(Q_BLOCK)
        z = jnp.einsum('bhqd,bhkd->bhqk', qi, k, preferred_element_type=jnp.float32) * scale
        past = kpos[None, :] < qpos[:, None]
        log_beta = jax.nn.log_sigmoid(z)
        log_keep = jnp.where(past, jax.nn.log_sigmoid(-z), 0.0)
        later = lax.cumsum(log_keep, axis=3, reverse=True) - log_keep
        w = jnp.where(past, jnp.exp(log_beta + later), 0.0)
        return jnp.einsum('bhqk,bhkd->bhqd', w.astype(v.dtype), v)

    out = lax.map(block, (qb, jnp.arange(nq)))
    return out.transpose(1, 2, 0, 3, 4).reshape(B, H, S, d)


def moba_attention(q, k, v):
    B, H, S, d = q.shape
    L = MOBA_BLOCK
    nkb = -(-S // L)
    pad = nkb * L - S
    kp = jnp.pad(k, ((0, 0), (0, 0), (0, pad), (0, 0)))
    vp = jnp.pad(v, ((0, 0), (0, 0), (0, pad), (0, 0)))
    k_blocks = kp.reshape(B, H, nkb, L, d)
    v_blocks = vp.reshape(B, H, nkb, L, d)
    k_mean = jnp.mean(k_blocks.astype(jnp.float32), axis=3)
    k_sel = min(MOBA_TOPK, nkb)
    scale = d ** -0.5
    nq = S // Q_BLOCK
    qb = q.reshape(B, H, nq, Q_BLOCK, d).transpose(2, 0, 1, 3, 4)
    gather = jax.vmap(jax.vmap(lambda blk, idx: blk[idx]))

    def block(args):
        qi, i = args
        qpos = i * Q_BLOCK + jnp.arange(Q_BLOCK)
        own = (i * Q_BLOCK) // L
        gate = jnp.einsum('bhqd,bhnd->bhqn', qi.astype(jnp.float32), k_mean)
        gate = jnp.where(jnp.arange(nkb) < own, gate, NEG)
        _, top_idx = lax.top_k(gate, k_sel)
        sel_valid = jnp.arange(k_sel) < own
        k_g = gather(k_blocks, top_idx)
        v_g = gather(v_blocks, top_idx)
        s_sel = jnp.einsum('bhqd,bhqkld->bhqkl', qi, k_g, preferred_element_type=jnp.float32) * scale
        s_sel = jnp.where(sel_valid[:, None], s_sel, NEG).reshape(B, H, Q_BLOCK, k_sel * L)
        k_own = lax.dynamic_index_in_dim(k_blocks, own, axis=2, keepdims=False)
        v_own = lax.dynamic_index_in_dim(v_blocks, own, axis=2, keepdims=False)
        own_pos = own * L + jnp.arange(L)
        s_own = jnp.einsum('bhqd,bhld->bhql', qi, k_own, preferred_element_type=jnp.float32) * scale
        s_own = jnp.where(own_pos[None, :] <= qpos[:, None], s_own, NEG)
        p = jax.nn.softmax(jnp.concatenate([s_sel, s_own], axis=-1), axis=-1).astype(v.dtype)
        p_sel, p_own = p[..., :k_sel * L], p[..., k_sel * L:]
        out = jnp.einsum('bhqm,bhqmd->bhqd', p_sel, v_g.reshape(B, H, Q_BLOCK, k_sel * L, d))
        return out + jnp.einsum('bhql,bhld->bhqd', p_own, v_own)

    out = lax.map(block, (qb, jnp.arange(nq)))
    return out.transpose(1, 2, 0, 3, 4).reshape(B, H, S, d)


def sliding_window_attention(q, k, v, sinks):
    B, Hq, S, d = q.shape
    Hkv = k.shape[1]
    G = Hq // Hkv
    W = WINDOW
    nb = S // W
    scale = d ** -0.5
    qb = q.reshape(B, Hkv, G, nb, W, d)
    kb = k.reshape(B, Hkv, nb, W, d)
    vb = v.reshape(B, Hkv, nb, W, d)
    shift = ((0, 0), (0, 0), (1, 0), (0, 0), (0, 0))
    k_band = jnp.concatenate([jnp.pad(kb, shift)[:, :, :-1], kb], axis=3)
    v_band = jnp.concatenate([jnp.pad(vb, shift)[:, :, :-1], vb], axis=3)
    s = jnp.einsum('bhgnqd,bhnkd->bhgnqk', qb, k_band, preferred_element_type=jnp.float32) * scale
    q_rel = jnp.arange(W)[:, None] + W
    k_rel = jnp.arange(2 * W)[None, :]
    delta = q_rel - k_rel
    in_window = (delta >= 0) & (delta < W)
    blk = jnp.arange(nb)[:, None, None]
    valid = in_window[None] & ((blk - 1) * W + k_rel[None] >= 0)
    s = jnp.where(valid, s, NEG)
    sink_col = jnp.broadcast_to(sinks.astype(jnp.float32).reshape(1, Hkv, G, 1, 1, 1),
                                s.shape[:-1] + (1,))
    p = jax.nn.softmax(jnp.concatenate([s, sink_col], axis=-1), axis=-1)[..., :-1]
    out = jnp.einsum('bhgnqk,bhnkd->bhgnqd', p.astype(v.dtype), v_band)
    return out.reshape(B, Hq, S, d)


def split_columns(t):
    parts, start = [], 0
    for size in SPLIT_SIZES:
        parts.append(t[..., start:start + size])
        start += size
    return parts


def setup_inputs(seed: int = 0) -> dict:
    key = jax.random.key(seed)
    ks = jax.random.split(key, 12)
    f32 = jnp.float32

    def dense(k, shape, fan_in):
        return jax.random.normal(k, shape, f32) * (fan_in ** -0.5)

    def gain(k, shape):
        return 1.0 + 0.02 * jax.random.normal(k, shape, f32)

    return {
        "x": jax.random.normal(ks[0], (BATCH, SEQ, D_MODEL), f32),
        "attn_norm": gain(ks[1], (DEPTH, D_MODEL)),
        "w_in": dense(ks[2], (DEPTH, D_MODEL, IN_WIDTH), D_MODEL),
        "sinks": 0.5 * jax.random.normal(ks[3], (DEPTH, SWA_Q_HEADS), f32),
        "gn_sb": gain(ks[4], (DEPTH, SB_WIDTH)),
        "gn_moba": gain(ks[5], (DEPTH, MOBA_WIDTH)),
        "gn_swa": gain(ks[6], (DEPTH, SWA_Q_WIDTH)),
        "w_out": dense(ks[7], (DEPTH, MIX_WIDTH, D_MODEL), MIX_WIDTH),
        "mlp_norm": gain(ks[8], (DEPTH, D_MODEL)),
        "w_up": dense(ks[9], (DEPTH, D_MODEL, D_FF), D_MODEL),
        "w_down": dense(ks[10], (DEPTH, D_FF, D_MODEL), D_FF),
        "final_norm": gain(ks[11], (D_MODEL,)),
    }


def reference(x, attn_norm, w_in, sinks, gn_sb, gn_moba, gn_swa, w_out,
              mlp_norm, w_up, w_down, final_norm):
    S = x.shape[1]
    cos, sin = rope_tables(S)
    for l in range(DEPTH):
        h = rmsnorm(x, attn_norm[l])
        proj = jnp.einsum('bsd,de->bse', h, w_in[l])
        qa, ka, va, qb, kb, vb, qc, kc, vc = split_columns(proj)
        ya = stick_breaking_attention(split_heads(qa, SB_HEADS), split_heads(ka, SB_HEADS),
                                      split_heads(va, SB_HEADS))
        ya_b = moba_attention(apply_rope(split_heads(qb, MOBA_HEADS), cos, sin),
                              apply_rope(split_heads(kb, MOBA_HEADS), cos, sin),
                              split_heads(vb, MOBA_HEADS))
        yc = sliding_window_attention(apply_rope(split_heads(qc, SWA_Q_HEADS), cos, sin),
                                      apply_rope(split_heads(kc, SWA_KV_HEADS), cos, sin),
                                      split_heads(vc, SWA_KV_HEADS), sinks[l])
        mixed = jnp.concatenate([rmsnorm(merge_heads(ya), gn_sb[l]),
                                 rmsnorm(merge_heads(ya_b), gn_moba[l]),
                                 rmsnorm(merge_heads(yc), gn_swa[l])], axis=-1)
        x = x + jnp.einsum('bse,ed->bsd', mixed, w_out[l])
        h = rmsnorm(x, mlp_norm[l])
        u = jax.nn.relu(jnp.einsum('bsd,df->bsf', h, w_up[l]))
        x = x + jnp.einsum('bsf,fd->bsd', u * u, w_down[l])
    return rmsnorm(x, final_norm)
```

```python
import functools

import numpy as np
import jax
import jax.numpy as jnp
from jax import lax
from jax.experimental import pallas as pl
from jax.experimental.pallas import tpu as pltpu

F32 = jnp.float32
BF16 = jnp.bfloat16

HEAD_DIM = 64
PAIR = 2 * HEAD_DIM
ROPE_HALF = HEAD_DIM // 2
MOBA_BLOCK = 256
MOBA_TOPK = 3
WINDOW = 128
SWA_GROUP = 8
ROPE_THETA = 10000.0
EPS = 1e-6
NEG = -1e30
Q_SCALE = HEAD_DIM ** -0.5

VMEM_LIMIT_BYTES = 48 * 1024 * 1024

NT_DIMS = (((1,), (1,)), ((), ()))


def _params(*semantics):
    return pltpu.CompilerParams(dimension_semantics=semantics,
                                vmem_limit_bytes=VMEM_LIMIT_BYTES)


def _lane_index(shape, dtype=jnp.int32):
    idx = lax.broadcasted_iota(jnp.int32, shape, len(shape) - 1)
    return idx if dtype == jnp.int32 else idx.astype(F32).astype(dtype)


def _layout(d_model):
    sb = d_model // 4
    moba = d_model // 4
    swa_q = d_model // 2
    swa_kv = (swa_q // HEAD_DIM // SWA_GROUP) * HEAD_DIM
    sizes = (sb, sb, sb, moba, moba, moba, swa_q, swa_kv, swa_kv)
    offs = np.concatenate([[0], np.cumsum(sizes)]).astype(int)
    names = ("qa", "ka", "va", "qb", "kb", "vb", "qc", "kc", "vc")
    lay = {n: (int(offs[i]), int(sizes[i])) for i, n in enumerate(names)}
    lay["in_width"] = int(offs[-1])
    assert sb % PAIR == 0 and swa_q % PAIR == 0 and swa_kv == PAIR
    return lay


def _tiles(n_tokens, seq):
    tm = min(512, seq)
    assert seq % tm == 0 and n_tokens % tm == 0
    return dict(tm=tm, tn_in=256, tn_out=512, tf=512)


def _rope_tables(seq):
    inv_freq = ROPE_THETA ** (-jnp.arange(ROPE_HALF, dtype=F32) * 2.0 / HEAD_DIM)
    ang = jnp.arange(seq, dtype=F32)[:, None] * inv_freq[None, :]
    cos, sin = jnp.cos(ang), jnp.sin(ang)
    cos_t = jnp.tile(cos, (1, PAIR // ROPE_HALF))
    sin_t = jnp.tile(jnp.concatenate([-sin, sin], axis=1), (1, PAIR // HEAD_DIM))
    return cos_t, sin_t


def _in_proj_kernel(x_ref, g_ref, w_ref, cos_ref, sin_ref, flag_ref, scale_ref, o_ref, h_ref,
                    *, rope_ranges):
    j = pl.program_id(1)

    @pl.when(j == 0)
    def _():
        x = x_ref[...]
        ms = jnp.mean(x * x, axis=-1, keepdims=True)
        h_ref[...] = ((x * lax.rsqrt(ms + EPS)) * g_ref[...]).astype(BF16)

    acc = jnp.dot(h_ref[...], w_ref[...], preferred_element_type=F32)
    tm, tn = acc.shape
    scale = scale_ref[...]

    has_rope = functools.reduce(jnp.logical_or, [(j >= a) & (j < b) for a, b in rope_ranges])

    @pl.when(has_rope)
    def _():
        reps = tn // PAIR
        c = jnp.concatenate([cos_ref[...]] * reps, axis=1)
        s = jnp.concatenate([sin_ref[...]] * reps, axis=1)
        lane = lax.broadcasted_iota(jnp.int32, (tm, tn), 1)
        first = (lane % HEAD_DIM) < ROPE_HALF
        partner = jnp.where(first, pltpu.roll(acc, tn - ROPE_HALF, axis=1),
                            pltpu.roll(acc, ROPE_HALF, axis=1))
        roped = acc * c + partner * s
        out = jnp.where(flag_ref[...] > 0.0, roped, acc)
        o_ref[...] = (out * scale).astype(o_ref.dtype)

    @pl.when(jnp.logical_not(has_rope))
    def _():
        o_ref[...] = (acc * scale).astype(o_ref.dtype)


def _in_proj(xt, gain, w, cos_t, sin_t, lay, seq, tiles):
    n_tok, d = xt.shape
    in_w = lay["in_width"]
    tm, tn = tiles["tm"], tiles["tn_in"]
    assert in_w % tn == 0
    flag = np.zeros((1, in_w), np.float32)
    scale = np.ones((1, in_w), np.float32)
    for name in ("qb", "kb", "qc", "kc"):
        o, s = lay[name]
        flag[0, o:o + s] = 1.0
    for name in ("qa", "qb", "qc"):
        o, s = lay[name]
        scale[0, o:o + s] = Q_SCALE
    tile_has = flag.reshape(in_w // tn, tn).max(axis=1) > 0
    ranges, start = [], None
    for t, f in enumerate(list(tile_has) + [False]):
        if f and start is None:
            start = t
        if not f and start is not None:
            ranges.append((start, t))
            start = None
    pos_blocks = seq // tm
    return pl.pallas_call(
        functools.partial(_in_proj_kernel, rope_ranges=tuple(ranges)),
        grid=(n_tok // tm, in_w // tn),
        in_specs=[
            pl.BlockSpec((tm, d), lambda i, j: (i, 0)),
            pl.BlockSpec((1, d), lambda i, j: (0, 0)),
            pl.BlockSpec((d, tn), lambda i, j: (0, j)),
            pl.BlockSpec((tm, PAIR), lambda i, j: (i % pos_blocks, 0)),
            pl.BlockSpec((tm, PAIR), lambda i, j: (i % pos_blocks, 0)),
            pl.BlockSpec((1, tn), lambda i, j: (0, j)),
            pl.BlockSpec((1, tn), lambda i, j: (0, j)),
        ],
        out_specs=pl.BlockSpec((tm, tn), lambda i, j: (i, j)),
        out_shape=jax.ShapeDtypeStruct((n_tok, in_w), BF16),
        scratch_shapes=[pltpu.VMEM((tm, d), BF16)],
        compiler_params=_params("parallel", "arbitrary"),
        name="in_proj",
    )(xt, gain.reshape(1, d), w, cos_t, sin_t, jnp.asarray(flag), jnp.asarray(scale))


def _sb_kernel(q_ref, k_ref, v_ref, u_ref, o_ref, *, tq):
    i = pl.program_id(2)
    q = q_ref[0]
    lo = _lane_index((tq, PAIR), BF16) < HEAD_DIM
    zq = jnp.zeros_like(q)
    qs = jnp.concatenate([jnp.where(lo, q, zq), jnp.where(lo, zq, q)], axis=0)
    u = u_ref[...]
    row = lax.broadcasted_iota(jnp.int32, (2 * tq, tq), 0) % tq
    col = lax.broadcasted_iota(jnp.int32, (2 * tq, tq), 1)
    past = col < row

    def tile(kblk, vblk, carry, acc, mask):
        z = lax.dot_general(qs, kblk, NT_DIMS, preferred_element_type=F32)
        sp = jnp.maximum(z, 0.0) + jnp.log(1.0 + jnp.exp(-jnp.abs(z)))
        spm = sp if mask is None else jnp.where(mask, sp, 0.0)
        hi = spm.astype(BF16)
        lo_part = (spm - hi.astype(F32)).astype(BF16)
        r = jnp.dot(jnp.concatenate([hi, lo_part], axis=1), u, preferred_element_type=F32)
        later, total = r[:, :tq], r[:, tq:]
        w = jnp.exp(z - sp - later - carry)
        if mask is not None:
            w = jnp.where(mask, w, 0.0)
        wb = w.astype(BF16)
        wcat = jnp.concatenate([wb[:tq], wb[tq:]], axis=1)
        zv = jnp.zeros_like(vblk)
        vcat = jnp.concatenate([jnp.where(lo, vblk, zv), jnp.where(lo, zv, vblk)], axis=0)
        acc = acc + jnp.dot(wcat, vcat, preferred_element_type=F32)
        return carry + total, acc

    off = pl.multiple_of(i * tq, tq)
    carry, acc = tile(k_ref[0, pl.ds(off, tq), :], v_ref[0, pl.ds(off, tq), :],
                      jnp.zeros((2 * tq, PAIR), F32), jnp.zeros((tq, PAIR), F32), past)

    def body(t, state):
        o = pl.multiple_of((i - 1 - t) * tq, tq)
        return tile(k_ref[0, pl.ds(o, tq), :], v_ref[0, pl.ds(o, tq), :], state[0], state[1], None)

    carry, acc = lax.fori_loop(0, i, body, (carry, acc))
    o_ref[0] = acc.astype(o_ref.dtype)


def _sb_attention(proj, lay):
    b, seq, _ = proj.shape
    tq = PAIR
    q_off, width = lay["qa"]
    k_off, v_off = lay["ka"][0], lay["va"][0]
    pairs = width // PAIR
    tri = np.tril(np.ones((tq, tq), np.float32), -1)
    uu = np.concatenate([tri, np.ones((tq, PAIR), np.float32)], axis=1)
    uu = jnp.asarray(np.concatenate([uu, uu], axis=0), dtype=BF16)
    return pl.pallas_call(
        functools.partial(_sb_kernel, tq=tq),
        grid=(b, pairs, seq // tq),
        in_specs=[
            pl.BlockSpec((1, tq, PAIR), lambda bi, p, i: (bi, i, q_off // PAIR + p)),
            pl.BlockSpec((1, seq, PAIR), lambda bi, p, i: (bi, 0, k_off // PAIR + p)),
            pl.BlockSpec((1, seq, PAIR), lambda bi, p, i: (bi, 0, v_off // PAIR + p)),
            pl.BlockSpec((2 * tq, tq + PAIR), lambda bi, p, i: (0, 0)),
        ],
        out_specs=pl.BlockSpec((1, tq, PAIR), lambda bi, p, i: (bi, i, p)),
        out_shape=jax.ShapeDtypeStruct((b, seq, width), BF16),
        compiler_params=_params("parallel", "parallel", "arbitrary"),
        name="sb_attention",
    )(proj, proj, proj, uu)


def _moba_kernel(q_ref, k_ref, v_ref, o_ref, kmean_ref, *, tq, nb, nbp):
    i = pl.program_id(2)
    blk = MOBA_BLOCK

    @pl.when(i == 0)
    def _():
        kf = k_ref[0].astype(F32).reshape(nb, blk, PAIR)
        km = jnp.sum(kf, axis=1) * (1.0 / blk)
        if nbp > nb:
            km = jnp.concatenate([km, jnp.zeros((nbp - nb, PAIR), F32)], axis=0)
        kmean_ref[...] = km

    q = q_ref[0]
    own = (i * tq) // blk
    lo_q = _lane_index((tq, PAIR)) < HEAD_DIM
    lo_qb = _lane_index((tq, PAIR), BF16) < HEAD_DIM
    lane_k = _lane_index((blk, PAIR), BF16)
    lo_k = lane_k < HEAD_DIM
    lo_m = _lane_index((nbp, PAIR)) < HEAD_DIM
    jidx = lax.broadcasted_iota(jnp.int32, (nbp, tq), 0)
    valid = jidx < own
    km = kmean_ref[...]
    zq = jnp.zeros_like(q)

    q_plain, q_aug = [], []
    for h in (0, 1):
        head_m = lo_m if h == 0 else jnp.logical_not(lo_m)
        head_q = lo_qb if h == 0 else jnp.logical_not(lo_qb)
        kmh = jnp.where(head_m, km, 0.0)
        a = kmh.astype(BF16)
        r1 = kmh - a.astype(F32)
        b2 = r1.astype(BF16)
        c3 = (r1 - b2.astype(F32)).astype(BF16)
        g3 = lax.dot_general(jnp.concatenate([a, b2, c3], axis=0), q, NT_DIMS,
                             preferred_element_type=F32)
        gate = g3[:nbp] + g3[nbp:2 * nbp] + g3[2 * nbp:]
        gate = jnp.where(valid, gate, -jnp.inf)
        beaten_by = jnp.zeros((nbp, tq), jnp.int32)
        for jp in range(nb):
            other = gate[jp:jp + 1, :]
            beats = (other > gate) | ((other == gate) & (jidx > jp))
            beaten_by = beaten_by + beats.astype(jnp.int32)
        sel = valid & (beaten_by < MOBA_TOPK)
        bias_t = jnp.where(sel, 0.0, NEG)
        top = HEAD_DIM if h == 0 else 0
        pieces = [jnp.zeros((top, tq), F32)] if top else []
        pieces.append(bias_t)
        if PAIR - top - nbp:
            pieces.append(jnp.zeros((PAIR - top - nbp, tq), F32))
        placed = jnp.concatenate(pieces, axis=0).T
        q_plain.append(jnp.where(head_q, q, zq))
        q_aug.append(jnp.where(head_q, q, placed.astype(BF16)))

    def attend(s0, s1, vblk, m0, m1, acc):
        n0 = jnp.maximum(m0, jnp.max(s0, axis=1, keepdims=True))
        n1 = jnp.maximum(m1, jnp.max(s1, axis=1, keepdims=True))
        p0 = jnp.exp(s0 - n0).astype(BF16)
        p1 = jnp.exp(s1 - n1).astype(BF16)
        alpha = jnp.where(lo_q, jnp.exp(m0 - n0), jnp.exp(m1 - n1))
        zv = jnp.zeros_like(vblk)
        ov = jnp.ones_like(vblk)
        rhs = jnp.concatenate(
            [jnp.concatenate([jnp.where(lo_k, vblk, zv), jnp.where(lo_k, ov, zv)], axis=1),
             jnp.concatenate([jnp.where(lo_k, zv, vblk), jnp.where(lo_k, zv, ov)], axis=1)], axis=0)
        upd = jnp.dot(jnp.concatenate([p0, p1], axis=1), rhs, preferred_element_type=F32)
        acc = acc * jnp.concatenate([alpha, alpha], axis=1) + upd
        return n0, n1, acc

    own_off = pl.multiple_of(own * blk, blk)
    k_own = k_ref[0, pl.ds(own_off, blk), :]
    row = lax.broadcasted_iota(jnp.int32, (tq, blk), 0)
    col = lax.broadcasted_iota(jnp.int32, (tq, blk), 1)
    causal = col <= row + (i * tq - own * blk)
    s_own = [jnp.where(causal, lax.dot_general(qh, k_own, NT_DIMS, preferred_element_type=F32), NEG)
             for qh in q_plain]
    m_init = jnp.full((tq, 1), NEG, F32)
    state = attend(s_own[0], s_own[1], v_ref[0, pl.ds(own_off, blk), :],
                   m_init, m_init, jnp.zeros((tq, 2 * PAIR), F32))

    def body(j, st):
        o = pl.multiple_of(j * blk, blk)
        kblk = k_ref[0, pl.ds(o, blk), :]
        one = jnp.ones_like(kblk)
        zk = jnp.zeros_like(kblk)
        jb = jnp.full((1, PAIR), j, jnp.int32).astype(F32).astype(BF16)
        k0 = jnp.where(lo_k, kblk, jnp.where(lane_k == jb + HEAD_DIM, one, zk))
        k1 = jnp.where(lo_k, jnp.where(lane_k == jb, one, zk), kblk)
        s0 = lax.dot_general(q_aug[0], k0, NT_DIMS, preferred_element_type=F32)
        s1 = lax.dot_general(q_aug[1], k1, NT_DIMS, preferred_element_type=F32)
        return attend(s0, s1, v_ref[0, pl.ds(o, blk), :], *st)

    _, _, acc = lax.fori_loop(0, own, body, state)
    o_ref[0] = (acc[:, :PAIR] / acc[:, PAIR:]).astype(o_ref.dtype)


def _moba_attention(proj, lay):
    b, seq, _ = proj.shape
    tq = PAIR
    assert seq % MOBA_BLOCK == 0
    nb = seq // MOBA_BLOCK
    nbp = -(-nb // 8) * 8
    assert nbp <= HEAD_DIM
    q_off, width = lay["qb"]
    k_off, v_off = lay["kb"][0], lay["vb"][0]
    return pl.pallas_call(
        functools.partial(_moba_kernel, tq=tq, nb=nb, nbp=nbp),
        grid=(b, width // PAIR, seq // tq),
        in_specs=[
            pl.BlockSpec((1, tq, PAIR), lambda bi, p, i: (bi, i, q_off // PAIR + p)),
            pl.BlockSpec((1, seq, PAIR), lambda bi, p, i: (bi, 0, k_off // PAIR + p)),
            pl.BlockSpec((1, seq, PAIR), lambda bi, p, i: (bi, 0, v_off // PAIR + p)),
        ],
        out_specs=pl.BlockSpec((1, tq, PAIR), lambda bi, p, i: (bi, i, p)),
        out_shape=jax.ShapeDtypeStruct((b, seq, width), BF16),
        scratch_shapes=[pltpu.VMEM((nbp, PAIR), F32)],
        compiler_params=_params("parallel", "parallel", "arbitrary"),
        name="moba_attention",
    )(proj, proj, proj)


def _swa_kernel(sink_ref, q_ref, kp_ref, kc_ref, vp_ref, vc_ref, o_ref, *, n_pairs):
    n = pl.program_id(1)
    w = WINDOW
    k = jnp.concatenate([kp_ref[0], kc_ref[0]], axis=0).astype(F32)
    v = jnp.concatenate([vp_ref[0], vc_ref[0]], axis=0).astype(F32)
    lo_k = lax.broadcasted_iota(jnp.int32, (2 * w, PAIR), 1) < HEAD_DIM
    k_sw = pltpu.roll(k, HEAD_DIM, axis=1)
    v_sw = pltpu.roll(v, HEAD_DIM, axis=1)
    kk = [jnp.where(lo_k, k, k_sw).astype(BF16), jnp.where(lo_k, k_sw, k).astype(BF16)]
    vv = [jnp.where(lo_k, v, v_sw).astype(BF16), jnp.where(lo_k, v_sw, v).astype(BF16)]
    ones = jnp.ones((2 * w, PAIR), BF16)
    rhs = [jnp.concatenate([vg, ones], axis=1) for vg in vv]

    row = lax.broadcasted_iota(jnp.int32, (w, 2 * w), 0)
    col = lax.broadcasted_iota(jnp.int32, (w, 2 * w), 1)
    delta = row + w - col
    valid = (delta >= 0) & (delta < w) & ((n - 1) * w + col >= 0)
    lo_q = _lane_index((w, PAIR)) < HEAD_DIM
    lo_qb = _lane_index((w, PAIR), BF16) < HEAD_DIM
    pairs_per_kv = SWA_GROUP // 2

    for p in range(n_pairs):
        g = p // pairs_per_kv
        qp = q_ref[0, :, p * PAIR:(p + 1) * PAIR]
        zq = jnp.zeros_like(qp)
        outs = []
        for h in (0, 1):
            qh = jnp.where(lo_qb, qp, zq) if h == 0 else jnp.where(lo_qb, zq, qp)
            s = lax.dot_general(qh, kk[g], NT_DIMS, preferred_element_type=F32)
            s = jnp.where(valid, s, NEG)
            sink = sink_ref[2 * p + h]
            m = jnp.maximum(jnp.max(s, axis=1, keepdims=True), sink)
            pr = jnp.exp(s - m).astype(BF16)
            o2 = jnp.dot(pr, rhs[g], preferred_element_type=F32)
            outs.append(o2[:, :PAIR] / (o2[:, PAIR:] + jnp.exp(sink - m)))
        o_ref[0, :, p * PAIR:(p + 1) * PAIR] = jnp.where(lo_q, outs[0], outs[1]).astype(o_ref.dtype)


def _swa_attention(proj, sinks, lay):
    b, seq, _ = proj.shape
    w = WINDOW
    q_off, width = lay["qc"]
    k_off, v_off = lay["kc"][0], lay["vc"][0]
    assert q_off % width == 0 and seq % w == 0
    return pl.pallas_call(
        functools.partial(_swa_kernel, n_pairs=width // PAIR),
        grid=(b, seq // w),
        in_specs=[
            pl.BlockSpec(memory_space=pltpu.SMEM),
            pl.BlockSpec((1, w, width), lambda bi, n: (bi, n, q_off // width)),
            pl.BlockSpec((1, w, PAIR), lambda bi, n: (bi, jnp.maximum(n - 1, 0), k_off // PAIR)),
            pl.BlockSpec((1, w, PAIR), lambda bi, n: (bi, n, k_off // PAIR)),
            pl.BlockSpec((1, w, PAIR), lambda bi, n: (bi, jnp.maximum(n - 1, 0), v_off // PAIR)),
            pl.BlockSpec((1, w, PAIR), lambda bi, n: (bi, n, v_off // PAIR)),
        ],
        out_specs=pl.BlockSpec((1, w, width), lambda bi, n: (bi, n, 0)),
        out_shape=jax.ShapeDtypeStruct((b, seq, width), BF16),
        compiler_params=_params("parallel", "arbitrary"),
        name="swa_attention",
    )(sinks.astype(F32), proj, proj, proj, proj, proj)


def _out_proj_kernel(ya_ref, yb_ref, yc_ref, ga_ref, gb_ref, gc_ref, w_ref, x_ref, o_ref, mix_ref):
    j = pl.program_id(1)

    @pl.when(j == 0)
    def _():
        start = 0
        for y_ref, g_ref in ((ya_ref, ga_ref), (yb_ref, gb_ref), (yc_ref, gc_ref)):
            y = y_ref[...].astype(F32)
            ms = jnp.mean(y * y, axis=-1, keepdims=True)
            width = y.shape[1]
            mix_ref[:, start:start + width] = ((y * lax.rsqrt(ms + EPS)) * g_ref[...]).astype(BF16)
            start += width

    o_ref[...] = x_ref[...] + jnp.dot(mix_ref[...], w_ref[...], preferred_element_type=F32)


def _out_proj(ya, yb, yc, ga, gb, gc, w, xt, tiles):
    n_tok, d = xt.shape
    tm, tn = tiles["tm"], tiles["tn_out"]
    wa, wb, wc = ya.shape[1], yb.shape[1], yc.shape[1]
    mix_w = wa + wb + wc
    assert d % tn == 0 and w.shape == (mix_w, d)
    return pl.pallas_call(
        _out_proj_kernel,
        grid=(n_tok // tm, d // tn),
        in_specs=[
            pl.BlockSpec((tm, wa), lambda i, j: (i, 0)),
            pl.BlockSpec((tm, wb), lambda i, j: (i, 0)),
            pl.BlockSpec((tm, wc), lambda i, j: (i, 0)),
            pl.BlockSpec((1, wa), lambda i, j: (0, 0)),
            pl.BlockSpec((1, wb), lambda i, j: (0, 0)),
            pl.BlockSpec((1, wc), lambda i, j: (0, 0)),
            pl.BlockSpec((mix_w, tn), lambda i, j: (0, j)),
            pl.BlockSpec((tm, tn), lambda i, j: (i, j)),
        ],
        out_specs=pl.BlockSpec((tm, tn), lambda i, j: (i, j)),
        out_shape=jax.ShapeDtypeStruct((n_tok, d), F32),
        scratch_shapes=[pltpu.VMEM((tm, mix_w), BF16)],
        compiler_params=_params("parallel", "arbitrary"),
        name="out_proj",
    )(ya, yb, yc, ga.reshape(1, wa), gb.reshape(1, wb), gc.reshape(1, wc), w, xt)


def _mlp_kernel(x_ref, g_ref, wu_ref, wd_ref, gf_ref, o_ref, h_ref, acc_ref, *, final):
    f = pl.program_id(1)

    @pl.when(f == 0)
    def _():
        x = x_ref[...]
        ms = jnp.mean(x * x, axis=-1, keepdims=True)
        h_ref[...] = ((x * lax.rsqrt(ms + EPS)) * g_ref[...]).astype(BF16)
        acc_ref[...] = jnp.zeros_like(acc_ref)

    u = jnp.maximum(jnp.dot(h_ref[...], wu_ref[...], preferred_element_type=F32), 0.0)
    acc_ref[...] += jnp.dot((u * u).astype(BF16), wd_ref[...], preferred_element_type=F32)

    @pl.when(f == pl.num_programs(1) - 1)
    def _():
        y = x_ref[...] + acc_ref[...]
        if final:
            ms = jnp.mean(y * y, axis=-1, keepdims=True)
            y = (y * lax.rsqrt(ms + EPS)) * gf_ref[...]
        o_ref[...] = y


def _mlp(xt, gain, w_up, w_down, final_gain, tiles, final):
    n_tok, d = xt.shape
    d_ff = w_up.shape[1]
    tm, tf = tiles["tm"], tiles["tf"]
    assert d_ff % tf == 0
    return pl.pallas_call(
        functools.partial(_mlp_kernel, final=final),
        grid=(n_tok // tm, d_ff // tf),
        in_specs=[
            pl.BlockSpec((tm, d), lambda i, f: (i, 0)),
            pl.BlockSpec((1, d), lambda i, f: (0, 0)),
            pl.BlockSpec((d, tf), lambda i, f: (0, f)),
            pl.BlockSpec((tf, d), lambda i, f: (f, 0)),
            pl.BlockSpec((1, d), lambda i, f: (0, 0)),
        ],
        out_specs=pl.BlockSpec((tm, d), lambda i, f: (i, 0)),
        out_shape=jax.ShapeDtypeStruct((n_tok, d), F32),
        scratch_shapes=[pltpu.VMEM((tm, d), BF16), pltpu.VMEM((tm, d), F32)],
        compiler_params=_params("parallel", "arbitrary"),
        name="mlp",
    )(xt, gain.reshape(1, d), w_up, w_down, final_gain.reshape(1, d))


def kernel(x, attn_norm, w_in, sinks, gn_sb, gn_moba, gn_swa, w_out, mlp_norm, w_up, w_down, final_norm):
    b, seq, d = x.shape
    depth = w_in.shape[0]
    n_tok = b * seq
    lay = _layout(d)
    tiles = _tiles(n_tok, seq)
    cos_t, sin_t = _rope_tables(seq)
    xt = x.reshape(n_tok, d)
    for l in range(depth):
        proj = _in_proj(xt, attn_norm[l], w_in[l].astype(BF16), cos_t, sin_t, lay, seq, tiles)
        proj = proj.reshape(b, seq, lay["in_width"])
        ya = _sb_attention(proj, lay).reshape(n_tok, -1)
        yb = _moba_attention(proj, lay).reshape(n_tok, -1)
        yc = _swa_attention(proj, sinks[l], lay).reshape(n_tok, -1)
        xt = _out_proj(ya, yb, yc, gn_sb[l], gn_moba[l], gn_swa[l], w_out[l].astype(BF16), xt, tiles)
        xt = _mlp(xt, mlp_norm[l], w_up[l].astype(BF16), w_down[l].astype(BF16), final_norm,
                  tiles, final=(l == depth - 1))
    return xt.reshape(b, seq, d)
```

```python
import functools

import numpy as np
import jax
import jax.numpy as jnp
from jax import lax
from jax.experimental import pallas as pl
from jax.experimental.pallas import tpu as pltpu

F32 = jnp.float32
BF16 = jnp.bfloat16

HEAD_DIM = 64
PAIR = 2 * HEAD_DIM
ROPE_HALF = HEAD_DIM // 2
MOBA_BLOCK = 256
MOBA_TOPK = 3
WINDOW = 128
SWA_GROUP = 8
ROPE_THETA = 10000.0
EPS = 1e-6
NEG = -1e30
Q_SCALE = HEAD_DIM ** -0.5
MOBA_GROUP = 4
SB_TILE = 256
SB_CHUNK = PAIR
SB_UNDERFLOW = 104.0

VMEM_LIMIT_BYTES = 48 * 1024 * 1024

NT_DIMS = (((1,), (1,)), ((), ()))


def _params(*semantics):
    return pltpu.CompilerParams(dimension_semantics=semantics,
                                vmem_limit_bytes=VMEM_LIMIT_BYTES)


def _lane_index(shape, dtype=jnp.int32):
    idx = lax.broadcasted_iota(jnp.int32, shape, len(shape) - 1)
    return idx if dtype == jnp.int32 else idx.astype(F32).astype(dtype)


def _layout(d_model):
    sb = d_model // 4
    moba = d_model // 4
    swa_q = d_model // 2
    swa_kv = (swa_q // HEAD_DIM // SWA_GROUP) * HEAD_DIM
    sizes = (sb, sb, sb, moba, moba, moba, swa_q, swa_kv, swa_kv)
    offs = np.concatenate([[0], np.cumsum(sizes)]).astype(int)
    names = ("qa", "ka", "va", "qb", "kb", "vb", "qc", "kc", "vc")
    lay = {n: (int(offs[i]), int(sizes[i])) for i, n in enumerate(names)}
    lay["in_width"] = int(offs[-1])
    assert sb % PAIR == 0 and swa_q % PAIR == 0 and swa_kv == PAIR
    return lay


def _tiles(n_tokens, seq):
    tm = min(512, seq)
    assert seq % tm == 0 and n_tokens % tm == 0
    return dict(tm=tm, tn_in=256, tn_out=512, tf=512)


def _rope_tables(seq):
    inv_freq = ROPE_THETA ** (-jnp.arange(ROPE_HALF, dtype=F32) * 2.0 / HEAD_DIM)
    ang = jnp.arange(seq, dtype=F32)[:, None] * inv_freq[None, :]
    cos, sin = jnp.cos(ang), jnp.sin(ang)
    cos_t = jnp.tile(cos, (1, PAIR // ROPE_HALF))
    sin_t = jnp.tile(jnp.concatenate([-sin, sin], axis=1), (1, PAIR // HEAD_DIM))
    return cos_t, sin_t


def _in_proj_kernel(x_ref, g_ref, w_ref, cos_ref, sin_ref, flag_ref, scale_ref, o_ref, h_ref,
                    *, rope_ranges):
    j = pl.program_id(1)

    @pl.when(j == 0)
    def _():
        x = x_ref[...]
        ms = jnp.mean(x * x, axis=-1, keepdims=True)
        h_ref[...] = ((x * lax.rsqrt(ms + EPS)) * g_ref[...]).astype(BF16)

    acc = jnp.dot(h_ref[...], w_ref[...], preferred_element_type=F32)
    tm, tn = acc.shape
    scale = scale_ref[...]

    has_rope = functools.reduce(jnp.logical_or, [(j >= a) & (j < b) for a, b in rope_ranges])

    @pl.when(has_rope)
    def _():
        reps = tn // PAIR
        c = jnp.concatenate([cos_ref[...]] * reps, axis=1)
        s = jnp.concatenate([sin_ref[...]] * reps, axis=1)
        lane = lax.broadcasted_iota(jnp.int32, (tm, tn), 1)
        first = (lane % HEAD_DIM) < ROPE_HALF
        partner = jnp.where(first, pltpu.roll(acc, tn - ROPE_HALF, axis=1),
                            pltpu.roll(acc, ROPE_HALF, axis=1))
        roped = acc * c + partner * s
        out = jnp.where(flag_ref[...] > 0.0, roped, acc)
        o_ref[...] = (out * scale).astype(o_ref.dtype)

    @pl.when(jnp.logical_not(has_rope))
    def _():
        o_ref[...] = (acc * scale).astype(o_ref.dtype)


def _in_proj(xt, gain, w, cos_t, sin_t, lay, seq, tiles):
    n_tok, d = xt.shape
    in_w = lay["in_width"]
    tm, tn = tiles["tm"], tiles["tn_in"]
    assert in_w % tn == 0
    flag = np.zeros((1, in_w), np.float32)
    scale = np.ones((1, in_w), np.float32)
    for name in ("qb", "kb", "qc", "kc"):
        o, s = lay[name]
        flag[0, o:o + s] = 1.0
    for name in ("qa", "qb", "qc"):
        o, s = lay[name]
        scale[0, o:o + s] = Q_SCALE
    tile_has = flag.reshape(in_w // tn, tn).max(axis=1) > 0
    ranges, start = [], None
    for t, f in enumerate(list(tile_has) + [False]):
        if f and start is None:
            start = t
        if not f and start is not None:
            ranges.append((start, t))
            start = None
    pos_blocks = seq // tm
    return pl.pallas_call(
        functools.partial(_in_proj_kernel, rope_ranges=tuple(ranges)),
        grid=(n_tok // tm, in_w // tn),
        in_specs=[
            pl.BlockSpec((tm, d), lambda i, j: (i, 0)),
            pl.BlockSpec((1, d), lambda i, j: (0, 0)),
            pl.BlockSpec((d, tn), lambda i, j: (0, j)),
            pl.BlockSpec((tm, PAIR), lambda i, j: (i % pos_blocks, 0)),
            pl.BlockSpec((tm, PAIR), lambda i, j: (i % pos_blocks, 0)),
            pl.BlockSpec((1, tn), lambda i, j: (0, j)),
            pl.BlockSpec((1, tn), lambda i, j: (0, j)),
        ],
        out_specs=pl.BlockSpec((tm, tn), lambda i, j: (i, j)),
        out_shape=jax.ShapeDtypeStruct((n_tok, in_w), BF16),
        scratch_shapes=[pltpu.VMEM((tm, d), BF16)],
        compiler_params=_params("parallel", "arbitrary"),
        name="in_proj",
    )(xt, gain.reshape(1, d), w, cos_t, sin_t, jnp.asarray(flag), jnp.asarray(scale))


def _sb_kernel(q_ref, k_ref, v_ref, u_ref, o_ref, *, tq):
    i = pl.program_id(2)
    ch = SB_CHUNK
    q = q_ref[0]
    lo = _lane_index((tq, PAIR), BF16) < HEAD_DIM
    zq = jnp.zeros_like(q)
    qs = jnp.concatenate([jnp.where(lo, q, zq), jnp.where(lo, zq, q)], axis=0)
    u = u_ref[...]
    row = lax.broadcasted_iota(jnp.int32, (2 * tq, tq), 0) % tq
    col = lax.broadcasted_iota(jnp.int32, (2 * tq, tq), 1)
    past = col < row

    def tile(kblk, vblk, carry, acc, mask):
        z = lax.dot_general(qs, kblk, NT_DIMS, preferred_element_type=F32)
        sp = jnp.maximum(z, 0.0) + jnp.log(1.0 + jnp.exp(-jnp.abs(z)))
        spm = sp if mask is None else jnp.where(mask, sp, 0.0)
        logb = z - sp
        ws = [None] * (tq // ch)
        for c in reversed(range(tq // ch)):
            sl = slice(c * ch, (c + 1) * ch)
            s_c = spm[:, sl]
            hi = s_c.astype(BF16)
            lo_part = (s_c - hi.astype(F32)).astype(BF16)
            r = jnp.dot(jnp.concatenate([hi, lo_part], axis=1), u, preferred_element_type=F32)
            w = jnp.exp(logb[:, sl] - r[:, :ch] - carry)
            if mask is not None:
                w = jnp.where(mask[:, sl], w, 0.0)
            ws[c] = w.astype(BF16)
            carry = carry + r[:, ch:]
        wb = jnp.concatenate(ws, axis=1)
        wcat = jnp.concatenate([wb[:tq], wb[tq:]], axis=1)
        zv = jnp.zeros_like(vblk)
        vcat = jnp.concatenate([jnp.where(lo, vblk, zv), jnp.where(lo, zv, vblk)], axis=0)
        acc = acc + jnp.dot(wcat, vcat, preferred_element_type=F32)
        return carry, acc

    off = pl.multiple_of(i * tq, tq)
    carry, acc = tile(k_ref[0, pl.ds(off, tq), :], v_ref[0, pl.ds(off, tq), :],
                      jnp.zeros((2 * tq, PAIR), F32), jnp.zeros((tq, PAIR), F32), past)

    def cond(state):
        return (state[0] < i) & state[3]

    def body(state):
        t = state[0]
        o = pl.multiple_of((i - 1 - t) * tq, tq)
        carry, acc = tile(k_ref[0, pl.ds(o, tq), :], v_ref[0, pl.ds(o, tq), :], state[1], state[2], None)
        return t + 1, carry, acc, jnp.min(carry) < SB_UNDERFLOW

    _, _, acc, _ = lax.while_loop(cond, body, (jnp.int32(0), carry, acc, jnp.min(carry) < SB_UNDERFLOW))
    o_ref[0] = acc.astype(o_ref.dtype)


def _sb_attention(proj, lay):
    b, seq, _ = proj.shape
    tq = min(SB_TILE, seq)
    ch = SB_CHUNK
    assert seq % tq == 0 and tq % ch == 0
    q_off, width = lay["qa"]
    k_off, v_off = lay["ka"][0], lay["va"][0]
    pairs = width // PAIR
    tri = np.tril(np.ones((ch, ch), np.float32), -1)
    uu = np.concatenate([tri, np.ones((ch, PAIR), np.float32)], axis=1)
    uu = jnp.asarray(np.concatenate([uu, uu], axis=0), dtype=BF16)
    return pl.pallas_call(
        functools.partial(_sb_kernel, tq=tq),
        grid=(b, pairs, seq // tq),
        in_specs=[
            pl.BlockSpec((1, tq, PAIR), lambda bi, p, i: (bi, i, q_off // PAIR + p)),
            pl.BlockSpec((1, seq, PAIR), lambda bi, p, i: (bi, 0, k_off // PAIR + p)),
            pl.BlockSpec((1, seq, PAIR), lambda bi, p, i: (bi, 0, v_off // PAIR + p)),
            pl.BlockSpec((2 * ch, ch + PAIR), lambda bi, p, i: (0, 0)),
        ],
        out_specs=pl.BlockSpec((1, tq, PAIR), lambda bi, p, i: (bi, i, p)),
        out_shape=jax.ShapeDtypeStruct((b, seq, width), BF16),
        compiler_params=_params("parallel", "parallel", "arbitrary"),
        name="sb_attention",
    )(proj, proj, proj, uu)


def _moba_kernel(q_ref, k_ref, v_ref, o_ref, kmean_ref, *, tq, nb, nbp, group):
    i = pl.program_id(2)
    blk = MOBA_BLOCK

    @pl.when(i == 0)
    def _():
        kf = k_ref[0].astype(F32).reshape(nb, blk, PAIR)
        km = jnp.sum(kf, axis=1) * (1.0 / blk)
        if nbp > nb:
            km = jnp.concatenate([km, jnp.zeros((nbp - nb, PAIR), F32)], axis=0)
        kmean_ref[...] = km

    q = q_ref[0]
    own = (i * tq) // blk
    lo_q = _lane_index((tq, PAIR)) < HEAD_DIM
    lo_qb = _lane_index((tq, PAIR), BF16) < HEAD_DIM
    lane_k = _lane_index((blk, PAIR), BF16)
    lo_k = lane_k < HEAD_DIM
    lo_m = _lane_index((nbp, PAIR)) < HEAD_DIM
    jidx = lax.broadcasted_iota(jnp.int32, (nbp, tq), 0)
    valid = jidx < own
    km = kmean_ref[...]
    zq = jnp.zeros_like(q)

    q_plain, q_aug = [], []
    for h in (0, 1):
        head_m = lo_m if h == 0 else jnp.logical_not(lo_m)
        head_q = lo_qb if h == 0 else jnp.logical_not(lo_qb)
        kmh = jnp.where(head_m, km, 0.0)
        a = kmh.astype(BF16)
        r1 = kmh - a.astype(F32)
        b2 = r1.astype(BF16)
        c3 = (r1 - b2.astype(F32)).astype(BF16)
        g3 = lax.dot_general(jnp.concatenate([a, b2, c3], axis=0), q, NT_DIMS,
                             preferred_element_type=F32)
        gate = g3[:nbp] + g3[nbp:2 * nbp] + g3[2 * nbp:]
        gate = jnp.where(valid, gate, -jnp.inf)
        beaten_by = jnp.zeros((nbp, tq), jnp.int32)
        for jp in range(nb):
            other = gate[jp:jp + 1, :]
            beats = (other > gate) | ((other == gate) & (jidx > jp))
            beaten_by = beaten_by + beats.astype(jnp.int32)
        sel = valid & (beaten_by < MOBA_TOPK)
        bias_t = jnp.where(sel, 0.0, NEG)
        top = HEAD_DIM if h == 0 else 0
        pieces = [jnp.zeros((top, tq), F32)] if top else []
        pieces.append(bias_t)
        if PAIR - top - nbp:
            pieces.append(jnp.zeros((PAIR - top - nbp, tq), F32))
        placed = jnp.concatenate(pieces, axis=0).T
        q_plain.append(jnp.where(head_q, q, zq))
        q_aug.append(jnp.where(head_q, q, placed.astype(BF16)))

    def attend(s0, s1, vblk, lo_v, m0, m1, acc):
        n0 = jnp.maximum(m0, jnp.max(s0, axis=1, keepdims=True))
        n1 = jnp.maximum(m1, jnp.max(s1, axis=1, keepdims=True))
        p0 = jnp.exp(s0 - n0).astype(BF16)
        p1 = jnp.exp(s1 - n1).astype(BF16)
        alpha = jnp.where(lo_q, jnp.exp(m0 - n0), jnp.exp(m1 - n1))
        zv = jnp.zeros_like(vblk)
        ov = jnp.ones_like(vblk)
        rhs = jnp.concatenate(
            [jnp.concatenate([jnp.where(lo_v, vblk, zv), jnp.where(lo_v, ov, zv)], axis=1),
             jnp.concatenate([jnp.where(lo_v, zv, vblk), jnp.where(lo_v, zv, ov)], axis=1)], axis=0)
        upd = jnp.dot(jnp.concatenate([p0, p1], axis=1), rhs, preferred_element_type=F32)
        acc = acc * jnp.concatenate([alpha, alpha], axis=1) + upd
        return n0, n1, acc

    own_off = pl.multiple_of(own * blk, blk)
    k_own = k_ref[0, pl.ds(own_off, blk), :]
    row = lax.broadcasted_iota(jnp.int32, (tq, blk), 0)
    col = lax.broadcasted_iota(jnp.int32, (tq, blk), 1)
    causal = col <= row + (i * tq - own * blk)
    s_own = [jnp.where(causal, lax.dot_general(qh, k_own, NT_DIMS, preferred_element_type=F32), NEG)
             for qh in q_plain]
    m_init = jnp.full((tq, 1), NEG, F32)
    state = attend(s_own[0], s_own[1], v_ref[0, pl.ds(own_off, blk), :], lo_k,
                   m_init, m_init, jnp.zeros((tq, 2 * PAIR), F32))

    gk = group * blk
    lane_g = _lane_index((gk, PAIR), BF16)
    lo_g = lane_g < HEAD_DIM
    blk_in_group = (lax.broadcasted_iota(jnp.int32, (gk, PAIR), 0) // blk).astype(F32).astype(BF16)

    def body(g, st):
        o = pl.multiple_of(g * gk, gk)
        kg = k_ref[0, pl.ds(o, gk), :]
        one = jnp.ones_like(kg)
        zk = jnp.zeros_like(kg)
        first = jnp.full((1, PAIR), g * group, jnp.int32).astype(F32).astype(BF16)
        blk_id = blk_in_group + first
        k0 = jnp.where(lo_g, kg, jnp.where(lane_g == blk_id + HEAD_DIM, one, zk))
        k1 = jnp.where(lo_g, jnp.where(lane_g == blk_id, one, zk), kg)
        s0 = lax.dot_general(q_aug[0], k0, NT_DIMS, preferred_element_type=F32)
        s1 = lax.dot_general(q_aug[1], k1, NT_DIMS, preferred_element_type=F32)
        return attend(s0, s1, v_ref[0, pl.ds(o, gk), :], lo_g, *st)

    _, _, acc = lax.fori_loop(0, (own + group - 1) // group, body, state)
    o_ref[0] = (acc[:, :PAIR] / acc[:, PAIR:]).astype(o_ref.dtype)


def _moba_attention(proj, lay):
    b, seq, _ = proj.shape
    tq = MOBA_BLOCK
    assert seq % MOBA_BLOCK == 0
    nb = seq // MOBA_BLOCK
    nbp = -(-nb // 8) * 8
    group = min(MOBA_GROUP, nb)
    assert nbp <= HEAD_DIM
    assert nb % group == 0
    q_off, width = lay["qb"]
    k_off, v_off = lay["kb"][0], lay["vb"][0]
    return pl.pallas_call(
        functools.partial(_moba_kernel, tq=tq, nb=nb, nbp=nbp, group=group),
        grid=(b, width // PAIR, seq // tq),
        in_specs=[
            pl.BlockSpec((1, tq, PAIR), lambda bi, p, i: (bi, i, q_off // PAIR + p)),
            pl.BlockSpec((1, seq, PAIR), lambda bi, p, i: (bi, 0, k_off // PAIR + p)),
            pl.BlockSpec((1, seq, PAIR), lambda bi, p, i: (bi, 0, v_off // PAIR + p)),
        ],
        out_specs=pl.BlockSpec((1, tq, PAIR), lambda bi, p, i: (bi, i, p)),
        out_shape=jax.ShapeDtypeStruct((b, seq, width), BF16),
        scratch_shapes=[pltpu.VMEM((nbp, PAIR), F32)],
        compiler_params=_params("parallel", "parallel", "arbitrary"),
        name="moba_attention",
    )(proj, proj, proj)


def _swa_kernel(sink_ref, q_ref, kp_ref, kc_ref, vp_ref, vc_ref, o_ref, *, n_pairs):
    n = pl.program_id(1)
    w = WINDOW
    k = jnp.concatenate([kp_ref[0], kc_ref[0]], axis=0).astype(F32)
    v = jnp.concatenate([vp_ref[0], vc_ref[0]], axis=0).astype(F32)
    lo_k = lax.broadcasted_iota(jnp.int32, (2 * w, PAIR), 1) < HEAD_DIM
    k_sw = pltpu.roll(k, HEAD_DIM, axis=1)
    v_sw = pltpu.roll(v, HEAD_DIM, axis=1)
    kk = [jnp.where(lo_k, k, k_sw).astype(BF16), jnp.where(lo_k, k_sw, k).astype(BF16)]
    vv = [jnp.where(lo_k, v, v_sw).astype(BF16), jnp.where(lo_k, v_sw, v).astype(BF16)]
    ones = jnp.ones((2 * w, PAIR), BF16)
    rhs = [jnp.concatenate([vg, ones], axis=1) for vg in vv]

    row = lax.broadcasted_iota(jnp.int32, (w, 2 * w), 0)
    col = lax.broadcasted_iota(jnp.int32, (w, 2 * w), 1)
    delta = row + w - col
    valid = (delta >= 0) & (delta < w) & ((n - 1) * w + col >= 0)
    lo_q = _lane_index((w, PAIR)) < HEAD_DIM
    lo_qb = _lane_index((w, PAIR), BF16) < HEAD_DIM
    pairs_per_kv = SWA_GROUP // 2

    for p in range(n_pairs):
        g = p // pairs_per_kv
        qp = q_ref[0, :, p * PAIR:(p + 1) * PAIR]
        zq = jnp.zeros_like(qp)
        outs = []
        for h in (0, 1):
            qh = jnp.where(lo_qb, qp, zq) if h == 0 else jnp.where(lo_qb, zq, qp)
            s = lax.dot_general(qh, kk[g], NT_DIMS, preferred_element_type=F32)
            s = jnp.where(valid, s, NEG)
            sink = sink_ref[2 * p + h]
            m = jnp.maximum(jnp.max(s, axis=1, keepdims=True), sink)
            pr = jnp.exp(s - m).astype(BF16)
            o2 = jnp.dot(pr, rhs[g], preferred_element_type=F32)
            outs.append(o2[:, :PAIR] / (o2[:, PAIR:] + jnp.exp(sink - m)))
        o_ref[0, :, p * PAIR:(p + 1) * PAIR] = jnp.where(lo_q, outs[0], outs[1]).astype(o_ref.dtype)


def _swa_attention(proj, sinks, lay):
    b, seq, _ = proj.shape
    w = WINDOW
    q_off, width = lay["qc"]
    k_off, v_off = lay["kc"][0], lay["vc"][0]
    assert q_off % width == 0 and seq % w == 0
    return pl.pallas_call(
        functools.partial(_swa_kernel, n_pairs=width // PAIR),
        grid=(b, seq // w),
        in_specs=[
            pl.BlockSpec(memory_space=pltpu.SMEM),
            pl.BlockSpec((1, w, width), lambda bi, n: (bi, n, q_off // width)),
            pl.BlockSpec((1, w, PAIR), lambda bi, n: (bi, jnp.maximum(n - 1, 0), k_off // PAIR)),
            pl.BlockSpec((1, w, PAIR), lambda bi, n: (bi, n, k_off // PAIR)),
            pl.BlockSpec((1, w, PAIR), lambda bi, n: (bi, jnp.maximum(n - 1, 0), v_off // PAIR)),
            pl.BlockSpec((1, w, PAIR), lambda bi, n: (bi, n, v_off // PAIR)),
        ],
        out_specs=pl.BlockSpec((1, w, width), lambda bi, n: (bi, n, 0)),
        out_shape=jax.ShapeDtypeStruct((b, seq, width), BF16),
        compiler_params=_params("parallel", "arbitrary"),
        name="swa_attention",
    )(sinks.astype(F32), proj, proj, proj, proj, proj)


def _out_proj_kernel(ya_ref, yb_ref, yc_ref, ga_ref, gb_ref, gc_ref, w_ref, x_ref, o_ref, mix_ref):
    j = pl.program_id(1)

    @pl.when(j == 0)
    def _():
        start = 0
        for y_ref, g_ref in ((ya_ref, ga_ref), (yb_ref, gb_ref), (yc_ref, gc_ref)):
            y = y_ref[...].astype(F32)
            ms = jnp.mean(y * y, axis=-1, keepdims=True)
            width = y.shape[1]
            mix_ref[:, start:start + width] = ((y * lax.rsqrt(ms + EPS)) * g_ref[...]).astype(BF16)
            start += width

    o_ref[...] = x_ref[...] + jnp.dot(mix_ref[...], w_ref[...], preferred_element_type=F32)


def _out_proj(ya, yb, yc, ga, gb, gc, w, xt, tiles):
    n_tok, d = xt.shape
    tm, tn = tiles["tm"], tiles["tn_out"]
    wa, wb, wc = ya.shape[1], yb.shape[1], yc.shape[1]
    mix_w = wa + wb + wc
    assert d % tn == 0 and w.shape == (mix_w, d)
    return pl.pallas_call(
        _out_proj_kernel,
        grid=(n_tok // tm, d // tn),
        in_specs=[
            pl.BlockSpec((tm, wa), lambda i, j: (i, 0)),
            pl.BlockSpec((tm, wb), lambda i, j: (i, 0)),
            pl.BlockSpec((tm, wc), lambda i, j: (i, 0)),
            pl.BlockSpec((1, wa), lambda i, j: (0, 0)),
            pl.BlockSpec((1, wb), lambda i, j: (0, 0)),
            pl.BlockSpec((1, wc), lambda i, j: (0, 0)),
            pl.BlockSpec((mix_w, tn), lambda i, j: (0, j)),
            pl.BlockSpec((tm, tn), lambda i, j: (i, j)),
        ],
        out_specs=pl.BlockSpec((tm, tn), lambda i, j: (i, j)),
        out_shape=jax.ShapeDtypeStruct((n_tok, d), F32),
        scratch_shapes=[pltpu.VMEM((tm, mix_w), BF16)],
        compiler_params=_params("parallel", "arbitrary"),
        name="out_proj",
    )(ya, yb, yc, ga.reshape(1, wa), gb.reshape(1, wb), gc.reshape(1, wc), w, xt)


def _mlp_kernel(x_ref, g_ref, wu_ref, wd_ref, gf_ref, o_ref, h_ref, acc_ref, *, final):
    f = pl.program_id(1)

    @pl.when(f == 0)
    def _():
        x = x_ref[...]
        ms = jnp.mean(x * x, axis=-1, keepdims=True)
        h_ref[...] = ((x * lax.rsqrt(ms + EPS)) * g_ref[...]).astype(BF16)
        acc_ref[...] = jnp.zeros_like(acc_ref)

    u = jnp.maximum(jnp.dot(h_ref[...], wu_ref[...], preferred_element_type=F32), 0.0)
    acc_ref[...] += jnp.dot((u * u).astype(BF16), wd_ref[...], preferred_element_type=F32)

    @pl.when(f == pl.num_programs(1) - 1)
    def _():
        y = x_ref[...] + acc_ref[...]
        if final:
            ms = jnp.mean(y * y, axis=-1, keepdims=True)
            y = (y * lax.rsqrt(ms + EPS)) * gf_ref[...]
        o_ref[...] = y


def _mlp(xt, gain, w_up, w_down, final_gain, tiles, final):
    n_tok, d = xt.shape
    d_ff = w_up.shape[1]
    tm, tf = tiles["tm"], tiles["tf"]
    assert d_ff % tf == 0
    return pl.pallas_call(
        functools.partial(_mlp_kernel, final=final),
        grid=(n_tok // tm, d_ff // tf),
        in_specs=[
            pl.BlockSpec((tm, d), lambda i, f: (i, 0)),
            pl.BlockSpec((1, d), lambda i, f: (0, 0)),
            pl.BlockSpec((d, tf), lambda i, f: (0, f)),
            pl.BlockSpec((tf, d), lambda i, f: (f, 0)),
            pl.BlockSpec((1, d), lambda i, f: (0, 0)),
        ],
        out_specs=pl.BlockSpec((tm, d), lambda i, f: (i, 0)),
        out_shape=jax.ShapeDtypeStruct((n_tok, d), F32),
        scratch_shapes=[pltpu.VMEM((tm, d), BF16), pltpu.VMEM((tm, d), F32)],
        compiler_params=_params("parallel", "arbitrary"),
        name="mlp",
    )(xt, gain.reshape(1, d), w_up, w_down, final_gain.reshape(1, d))


def kernel(x, attn_norm, w_in, sinks, gn_sb, gn_moba, gn_swa, w_out, mlp_norm, w_up, w_down, final_norm):
    b, seq, d = x.shape
    depth = w_in.shape[0]
    n_tok = b * seq
    lay = _layout(d)
    tiles = _tiles(n_tok, seq)
    cos_t, sin_t = _rope_tables(seq)
    xt = x.reshape(n_tok, d)
    for l in range(depth):
        proj = _in_proj(xt, attn_norm[l], w_in[l].astype(BF16), cos_t, sin_t, lay, seq, tiles)
        proj = proj.reshape(b, seq, lay["in_width"])
        ya = _sb_attention(proj, lay).reshape(n_tok, -1)
        yb = _moba_attention(proj, lay).reshape(n_tok, -1)
        yc = _swa_attention(proj, sinks[l], lay).reshape(n_tok, -1)
        xt = _out_proj(ya, yb, yc, gn_sb[l], gn_moba[l], gn_swa[l], w_out[l].astype(BF16), xt, tiles)
        xt = _mlp(xt, mlp_norm[l], w_up[l].astype(BF16), w_down[l].astype(BF16), final_norm,
                  tiles, final=(l == depth - 1))
    return xt.reshape(b, seq, d)
```

```python
import functools

import numpy as np
import jax
import jax.numpy as jnp
from jax import lax
from jax.experimental import pallas as pl
from jax.experimental.pallas import tpu as pltpu

F32 = jnp.float32
BF16 = jnp.bfloat16

HEAD_DIM = 64
PAIR = 2 * HEAD_DIM
ROPE_HALF = HEAD_DIM // 2
MOBA_BLOCK = 256
MOBA_TOPK = 3
WINDOW = 128
SWA_GROUP = 8
ROPE_THETA = 10000.0
EPS = 1e-6
NEG = -1e30
Q_SCALE = HEAD_DIM ** -0.5
MOBA_GROUP = 4
SB_TILE = 256
SB_CHUNK = PAIR
SB_UNDERFLOW = 104.0

VMEM_LIMIT_BYTES = 56 * 1024 * 1024

NT_DIMS = (((1,), (1,)), ((), ()))


def _params(*semantics):
    return pltpu.CompilerParams(dimension_semantics=semantics,
                                vmem_limit_bytes=VMEM_LIMIT_BYTES)


def _lane_index(shape, dtype=jnp.int32):
    idx = lax.broadcasted_iota(jnp.int32, shape, len(shape) - 1)
    return idx if dtype == jnp.int32 else idx.astype(F32).astype(dtype)


def _layout(d_model):
    sb = d_model // 4
    moba = d_model // 4
    swa_q = d_model // 2
    swa_kv = (swa_q // HEAD_DIM // SWA_GROUP) * HEAD_DIM
    sizes = (sb, sb, sb, moba, moba, moba, swa_q, swa_kv, swa_kv)
    offs = np.concatenate([[0], np.cumsum(sizes)]).astype(int)
    names = ("qa", "ka", "va", "qb", "kb", "vb", "qc", "kc", "vc")
    lay = {n: (int(offs[i]), int(sizes[i])) for i, n in enumerate(names)}
    lay["in_width"] = int(offs[-1])
    assert sb % PAIR == 0 and swa_q % PAIR == 0 and swa_kv == PAIR
    return lay


def _tiles(n_tokens, seq):
    tm = min(512, seq)
    assert seq % tm == 0 and n_tokens % tm == 0
    return dict(tm=tm, tm_mlp=min(1024, n_tokens), tn_in=256, tn_out=512, tf=512)


def _rope_tables(seq):
    inv_freq = ROPE_THETA ** (-jnp.arange(ROPE_HALF, dtype=F32) * 2.0 / HEAD_DIM)
    ang = jnp.arange(seq, dtype=F32)[:, None] * inv_freq[None, :]
    cos, sin = jnp.cos(ang), jnp.sin(ang)
    cos_t = jnp.tile(cos, (1, PAIR // ROPE_HALF))
    sin_t = jnp.tile(jnp.concatenate([-sin, sin], axis=1), (1, PAIR // HEAD_DIM))
    return cos_t, sin_t


def _in_proj_kernel(x_ref, g_ref, w_ref, cos_ref, sin_ref, o_ref, h_ref, *, chunks):
    x = x_ref[...]
    ms = jnp.mean(x * x, axis=-1, keepdims=True)
    h_ref[...] = ((x * lax.rsqrt(ms + EPS)) * g_ref[...]).astype(BF16)
    tm = x.shape[0]
    lane = lax.broadcasted_iota(jnp.int32, (tm, PAIR), 1)
    first = (lane % HEAD_DIM) < ROPE_HALF
    for start, classes in chunks:
        width = len(classes) * PAIR
        acc = jnp.dot(h_ref[...], w_ref[:, start:start + width], preferred_element_type=F32)
        for t, (rope, scale) in enumerate(classes):
            a = acc[:, t * PAIR:(t + 1) * PAIR]
            if rope:
                partner = jnp.where(first, pltpu.roll(a, PAIR - ROPE_HALF, axis=1),
                                    pltpu.roll(a, ROPE_HALF, axis=1))
                a = a * cos_ref[...] + partner * sin_ref[...]
            if scale != 1.0:
                a = a * scale
            o_ref[:, start + t * PAIR:start + (t + 1) * PAIR] = a.astype(o_ref.dtype)


def _in_proj(xt, gain, w, cos_t, sin_t, lay, seq, tiles):
    n_tok, d = xt.shape
    in_w = lay["in_width"]
    tm, tn = tiles["tm"], tiles["tn_in"]
    assert in_w % tn == 0 and tn % PAIR == 0
    rope = np.zeros(in_w // PAIR, bool)
    scale = np.ones(in_w // PAIR, np.float32)
    for name in ("qb", "kb", "qc", "kc"):
        o, s = lay[name]
        rope[o // PAIR:(o + s) // PAIR] = True
    for name in ("qa", "qb", "qc"):
        o, s = lay[name]
        scale[o // PAIR:(o + s) // PAIR] = Q_SCALE
    per = tn // PAIR
    chunks = tuple((c * tn, tuple((bool(rope[c * per + t]), float(scale[c * per + t])) for t in range(per)))
                   for c in range(in_w // tn))
    pos_blocks = seq // tm
    return pl.pallas_call(
        functools.partial(_in_proj_kernel, chunks=chunks),
        grid=(n_tok // tm,),
        in_specs=[
            pl.BlockSpec((tm, d), lambda i: (i, 0)),
            pl.BlockSpec((1, d), lambda i: (0, 0)),
            pl.BlockSpec((d, in_w), lambda i: (0, 0), pipeline_mode=pl.Buffered(1)),
            pl.BlockSpec((tm, PAIR), lambda i: (i % pos_blocks, 0)),
            pl.BlockSpec((tm, PAIR), lambda i: (i % pos_blocks, 0)),
        ],
        out_specs=pl.BlockSpec((tm, in_w), lambda i: (i, 0)),
        out_shape=jax.ShapeDtypeStruct((n_tok, in_w), BF16),
        scratch_shapes=[pltpu.VMEM((tm, d), BF16)],
        compiler_params=_params("arbitrary"),
        name="in_proj",
    )(xt, gain.reshape(1, d), w, cos_t, sin_t)


def _sb_kernel(q_ref, k_ref, v_ref, u_ref, o_ref, *, tq):
    i = pl.program_id(2)
    ch = SB_CHUNK
    q = q_ref[0]
    lo = _lane_index((tq, PAIR), BF16) < HEAD_DIM
    zq = jnp.zeros_like(q)
    qs = jnp.concatenate([jnp.where(lo, q, zq), jnp.where(lo, zq, q)], axis=0)
    u = u_ref[...]
    row = lax.broadcasted_iota(jnp.int32, (2 * tq, tq), 0) % tq
    col = lax.broadcasted_iota(jnp.int32, (2 * tq, tq), 1)
    past = col < row

    def tile(kblk, vblk, carry, acc, mask):
        z = lax.dot_general(qs, kblk, NT_DIMS, preferred_element_type=F32)
        sp = jnp.maximum(z, 0.0) + jnp.log(1.0 + jnp.exp(-jnp.abs(z)))
        spm = sp if mask is None else jnp.where(mask, sp, 0.0)
        logb = z - sp
        ws = [None] * (tq // ch)
        for c in reversed(range(tq // ch)):
            sl = slice(c * ch, (c + 1) * ch)
            s_c = spm[:, sl]
            hi = s_c.astype(BF16)
            lo_part = (s_c - hi.astype(F32)).astype(BF16)
            r = jnp.dot(jnp.concatenate([hi, lo_part], axis=1), u, preferred_element_type=F32)
            w = jnp.exp(logb[:, sl] - r[:, :ch] - carry)
            if mask is not None:
                w = jnp.where(mask[:, sl], w, 0.0)
            ws[c] = w.astype(BF16)
            carry = carry + r[:, ch:]
        wb = jnp.concatenate(ws, axis=1)
        wcat = jnp.concatenate([wb[:tq], wb[tq:]], axis=1)
        zv = jnp.zeros_like(vblk)
        vcat = jnp.concatenate([jnp.where(lo, vblk, zv), jnp.where(lo, zv, vblk)], axis=0)
        acc = acc + jnp.dot(wcat, vcat, preferred_element_type=F32)
        return carry, acc

    off = pl.multiple_of(i * tq, tq)
    carry, acc = tile(k_ref[0, pl.ds(off, tq), :], v_ref[0, pl.ds(off, tq), :],
                      jnp.zeros((2 * tq, PAIR), F32), jnp.zeros((tq, PAIR), F32), past)

    def cond(state):
        return (state[0] < i) & state[3]

    def body(state):
        t = state[0]
        o = pl.multiple_of((i - 1 - t) * tq, tq)
        carry, acc = tile(k_ref[0, pl.ds(o, tq), :], v_ref[0, pl.ds(o, tq), :], state[1], state[2], None)
        return t + 1, carry, acc, jnp.min(carry) < SB_UNDERFLOW

    _, _, acc, _ = lax.while_loop(cond, body, (jnp.int32(0), carry, acc, jnp.min(carry) < SB_UNDERFLOW))
    o_ref[0] = acc.astype(o_ref.dtype)


def _sb_attention(proj, lay):
    b, seq, _ = proj.shape
    tq = min(SB_TILE, seq)
    ch = SB_CHUNK
    assert seq % tq == 0 and tq % ch == 0
    q_off, width = lay["qa"]
    k_off, v_off = lay["ka"][0], lay["va"][0]
    pairs = width // PAIR
    tri = np.tril(np.ones((ch, ch), np.float32), -1)
    uu = np.concatenate([tri, np.ones((ch, PAIR), np.float32)], axis=1)
    uu = jnp.asarray(np.concatenate([uu, uu], axis=0), dtype=BF16)
    return pl.pallas_call(
        functools.partial(_sb_kernel, tq=tq),
        grid=(b, pairs, seq // tq),
        in_specs=[
            pl.BlockSpec((1, tq, PAIR), lambda bi, p, i: (bi, i, q_off // PAIR + p)),
            pl.BlockSpec((1, seq, PAIR), lambda bi, p, i: (bi, 0, k_off // PAIR + p)),
            pl.BlockSpec((1, seq, PAIR), lambda bi, p, i: (bi, 0, v_off // PAIR + p)),
            pl.BlockSpec((2 * ch, ch + PAIR), lambda bi, p, i: (0, 0)),
        ],
        out_specs=pl.BlockSpec((1, tq, PAIR), lambda bi, p, i: (bi, i, p)),
        out_shape=jax.ShapeDtypeStruct((b, seq, width), BF16),
        compiler_params=_params("parallel", "parallel", "arbitrary"),
        name="sb_attention",
    )(proj, proj, proj, uu)


def _moba_kernel(q_ref, k_ref, v_ref, o_ref, kmean_ref, *, tq, nb, nbp, group):
    i = pl.program_id(2)
    blk = MOBA_BLOCK

    @pl.when(i == 0)
    def _():
        kf = k_ref[0].astype(F32).reshape(nb, blk, PAIR)
        km = jnp.sum(kf, axis=1) * (1.0 / blk)
        if nbp > nb:
            km = jnp.concatenate([km, jnp.zeros((nbp - nb, PAIR), F32)], axis=0)
        kmean_ref[...] = km

    q = q_ref[0]
    own = (i * tq) // blk
    lo_q = _lane_index((tq, PAIR)) < HEAD_DIM
    lo_q2 = jnp.concatenate([lo_q, lo_q], axis=1)
    lo_qb = _lane_index((tq, PAIR), BF16) < HEAD_DIM
    lo_m = _lane_index((nbp, PAIR)) < HEAD_DIM
    jidx = lax.broadcasted_iota(jnp.int32, (nbp, tq), 0)
    valid = jidx < own
    km = kmean_ref[...]
    zq = jnp.zeros_like(q)

    q_plain, q_aug = [], []
    for h in (0, 1):
        head_m = lo_m if h == 0 else jnp.logical_not(lo_m)
        head_q = lo_qb if h == 0 else jnp.logical_not(lo_qb)
        kmh = jnp.where(head_m, km, 0.0)
        a = kmh.astype(BF16)
        r1 = kmh - a.astype(F32)
        b2 = r1.astype(BF16)
        c3 = (r1 - b2.astype(F32)).astype(BF16)
        g3 = lax.dot_general(jnp.concatenate([a, b2, c3], axis=0), q, NT_DIMS,
                             preferred_element_type=F32)
        gate = g3[:nbp] + g3[nbp:2 * nbp] + g3[2 * nbp:]
        gate = jnp.where(valid, gate, -jnp.inf)
        beaten_by = jnp.zeros((nbp, tq), jnp.int32)
        for jp in range(nb):
            other = gate[jp:jp + 1, :]
            beats = (other > gate) | ((other == gate) & (jidx > jp))
            beaten_by = beaten_by + beats.astype(jnp.int32)
        sel = valid & (beaten_by < MOBA_TOPK)
        bias_t = jnp.where(sel, 0.0, NEG)
        top = HEAD_DIM if h == 0 else 0
        pieces = [jnp.zeros((top, tq), F32)] if top else []
        pieces.append(bias_t)
        if PAIR - top - nbp:
            pieces.append(jnp.zeros((PAIR - top - nbp, tq), F32))
        placed = jnp.concatenate(pieces, axis=0).T
        q_plain.append(jnp.where(head_q, q, zq))
        q_aug.append(jnp.where(head_q, q, placed.astype(BF16)))

    def attend(s0, s1, vblk, m0, m1, acc):
        n0 = jnp.maximum(m0, jnp.max(s0, axis=1, keepdims=True))
        n1 = jnp.maximum(m1, jnp.max(s1, axis=1, keepdims=True))
        p0 = jnp.exp(s0 - n0).astype(BF16)
        p1 = jnp.exp(s1 - n1).astype(BF16)
        alpha = jnp.where(lo_q, jnp.exp(m0 - n0), jnp.exp(m1 - n1))
        rhs = jnp.concatenate([vblk, jnp.ones_like(vblk)], axis=1)
        u0 = jnp.dot(p0, rhs, preferred_element_type=F32)
        u1 = jnp.dot(p1, rhs, preferred_element_type=F32)
        acc = acc * jnp.concatenate([alpha, alpha], axis=1) + jnp.where(lo_q2, u0, u1)
        return n0, n1, acc

    own_off = pl.multiple_of(own * blk, blk)
    k_own = k_ref[0, pl.ds(own_off, blk), :]
    row = lax.broadcasted_iota(jnp.int32, (tq, blk), 0)
    col = lax.broadcasted_iota(jnp.int32, (tq, blk), 1)
    causal = col <= row + (i * tq - own * blk)
    s_own = [jnp.where(causal, lax.dot_general(qh, k_own, NT_DIMS, preferred_element_type=F32), NEG)
             for qh in q_plain]
    m_init = jnp.full((tq, 1), NEG, F32)
    state = attend(s_own[0], s_own[1], v_ref[0, pl.ds(own_off, blk), :],
                   m_init, m_init, jnp.zeros((tq, 2 * PAIR), F32))

    gk = group * blk
    lane_g = _lane_index((gk, PAIR), BF16)
    lo_g = lane_g < HEAD_DIM
    blk_in_group = (lax.broadcasted_iota(jnp.int32, (gk, PAIR), 0) // blk).astype(F32).astype(BF16)

    def body(g, st):
        o = pl.multiple_of(g * gk, gk)
        kg = k_ref[0, pl.ds(o, gk), :]
        one = jnp.ones_like(kg)
        zk = jnp.zeros_like(kg)
        first = jnp.full((1, PAIR), g * group, jnp.int32).astype(F32).astype(BF16)
        blk_id = blk_in_group + first
        k0 = jnp.where(lo_g, kg, jnp.where(lane_g == blk_id + HEAD_DIM, one, zk))
        k1 = jnp.where(lo_g, jnp.where(lane_g == blk_id, one, zk), kg)
        s0 = lax.dot_general(q_aug[0], k0, NT_DIMS, preferred_element_type=F32)
        s1 = lax.dot_general(q_aug[1], k1, NT_DIMS, preferred_element_type=F32)
        return attend(s0, s1, v_ref[0, pl.ds(o, gk), :], *st)

    _, _, acc = lax.fori_loop(0, (own + group - 1) // group, body, state)
    o_ref[0] = (acc[:, :PAIR] / acc[:, PAIR:]).astype(o_ref.dtype)


def _moba_attention(proj, lay):
    b, seq, _ = proj.shape
    tq = MOBA_BLOCK
    assert seq % MOBA_BLOCK == 0
    nb = seq // MOBA_BLOCK
    nbp = -(-nb // 8) * 8
    group = min(MOBA_GROUP, nb)
    assert nbp <= HEAD_DIM
    assert nb % group == 0
    q_off, width = lay["qb"]
    k_off, v_off = lay["kb"][0], lay["vb"][0]
    return pl.pallas_call(
        functools.partial(_moba_kernel, tq=tq, nb=nb, nbp=nbp, group=group),
        grid=(b, width // PAIR, seq // tq),
        in_specs=[
            pl.BlockSpec((1, tq, PAIR), lambda bi, p, i: (bi, i, q_off // PAIR + p)),
            pl.BlockSpec((1, seq, PAIR), lambda bi, p, i: (bi, 0, k_off // PAIR + p)),
            pl.BlockSpec((1, seq, PAIR), lambda bi, p, i: (bi, 0, v_off // PAIR + p)),
        ],
        out_specs=pl.BlockSpec((1, tq, PAIR), lambda bi, p, i: (bi, i, p)),
        out_shape=jax.ShapeDtypeStruct((b, seq, width), BF16),
        scratch_shapes=[pltpu.VMEM((nbp, PAIR), F32)],
        compiler_params=_params("parallel", "parallel", "arbitrary"),
        name="moba_attention",
    )(proj, proj, proj)


def _swa_kernel(sink_ref, q_ref, kp_ref, kc_ref, vp_ref, vc_ref, o_ref, *, n_pairs):
    n = pl.program_id(1)
    w = WINDOW
    k = jnp.concatenate([kp_ref[0], kc_ref[0]], axis=0).astype(F32)
    v = jnp.concatenate([vp_ref[0], vc_ref[0]], axis=0).astype(F32)
    lo_k = lax.broadcasted_iota(jnp.int32, (2 * w, PAIR), 1) < HEAD_DIM
    k_sw = pltpu.roll(k, HEAD_DIM, axis=1)
    v_sw = pltpu.roll(v, HEAD_DIM, axis=1)
    kk = [jnp.where(lo_k, k, k_sw).astype(BF16), jnp.where(lo_k, k_sw, k).astype(BF16)]
    vv = [jnp.where(lo_k, v, v_sw).astype(BF16), jnp.where(lo_k, v_sw, v).astype(BF16)]
    ones = jnp.ones((2 * w, PAIR), BF16)
    rhs = [jnp.concatenate([vg, ones], axis=1) for vg in vv]

    row = lax.broadcasted_iota(jnp.int32, (w, 2 * w), 0)
    col = lax.broadcasted_iota(jnp.int32, (w, 2 * w), 1)
    delta = row + w - col
    valid = (delta >= 0) & (delta < w) & ((n - 1) * w + col >= 0)
    lo_q = _lane_index((w, PAIR)) < HEAD_DIM
    lo_qb = _lane_index((w, PAIR), BF16) < HEAD_DIM
    pairs_per_kv = SWA_GROUP // 2

    for p in range(n_pairs):
        g = p // pairs_per_kv
        qp = q_ref[0, :, p * PAIR:(p + 1) * PAIR]
        zq = jnp.zeros_like(qp)
        outs = []
        for h in (0, 1):
            qh = jnp.where(lo_qb, qp, zq) if h == 0 else jnp.where(lo_qb, zq, qp)
            s = lax.dot_general(qh, kk[g], NT_DIMS, preferred_element_type=F32)
            s = jnp.where(valid, s, NEG)
            sink = sink_ref[2 * p + h]
            m = jnp.maximum(jnp.max(s, axis=1, keepdims=True), sink)
            pr = jnp.exp(s - m).astype(BF16)
            o2 = jnp.dot(pr, rhs[g], preferred_element_type=F32)
            outs.append(o2[:, :PAIR] / (o2[:, PAIR:] + jnp.exp(sink - m)))
        o_ref[0, :, p * PAIR:(p + 1) * PAIR] = jnp.where(lo_q, outs[0], outs[1]).astype(o_ref.dtype)


def _swa_attention(proj, sinks, lay):
    b, seq, _ = proj.shape
    w = WINDOW
    q_off, width = lay["qc"]
    k_off, v_off = lay["kc"][0], lay["vc"][0]
    assert q_off % width == 0 and seq % w == 0
    return pl.pallas_call(
        functools.partial(_swa_kernel, n_pairs=width // PAIR),
        grid=(b, seq // w),
        in_specs=[
            pl.BlockSpec(memory_space=pltpu.SMEM),
            pl.BlockSpec((1, w, width), lambda bi, n: (bi, n, q_off // width)),
            pl.BlockSpec((1, w, PAIR), lambda bi, n: (bi, jnp.maximum(n - 1, 0), k_off // PAIR)),
            pl.BlockSpec((1, w, PAIR), lambda bi, n: (bi, n, k_off // PAIR)),
            pl.BlockSpec((1, w, PAIR), lambda bi, n: (bi, jnp.maximum(n - 1, 0), v_off // PAIR)),
            pl.BlockSpec((1, w, PAIR), lambda bi, n: (bi, n, v_off // PAIR)),
        ],
        out_specs=pl.BlockSpec((1, w, width), lambda bi, n: (bi, n, 0)),
        out_shape=jax.ShapeDtypeStruct((b, seq, width), BF16),
        compiler_params=_params("parallel", "arbitrary"),
        name="swa_attention",
    )(sinks.astype(F32), proj, proj, proj, proj, proj)


def _out_proj_kernel(ya_ref, yb_ref, yc_ref, ga_ref, gb_ref, gc_ref, w_ref, x_ref, o_ref, mix_ref,
                     *, tn):
    start = 0
    for y_ref, g_ref in ((ya_ref, ga_ref), (yb_ref, gb_ref), (yc_ref, gc_ref)):
        y = y_ref[...].astype(F32)
        ms = jnp.mean(y * y, axis=-1, keepdims=True)
        width = y.shape[1]
        mix_ref[:, start:start + width] = ((y * lax.rsqrt(ms + EPS)) * g_ref[...]).astype(BF16)
        start += width
    for c in range(o_ref.shape[1] // tn):
        cols = slice(c * tn, (c + 1) * tn)
        o_ref[:, cols] = x_ref[:, cols] + jnp.dot(mix_ref[...], w_ref[:, cols],
                                                  preferred_element_type=F32)


def _out_proj(ya, yb, yc, ga, gb, gc, w, xt, tiles):
    n_tok, d = xt.shape
    tm, tn = tiles["tm"], tiles["tn_out"]
    wa, wb, wc = ya.shape[1], yb.shape[1], yc.shape[1]
    mix_w = wa + wb + wc
    assert d % tn == 0 and w.shape == (mix_w, d)
    return pl.pallas_call(
        functools.partial(_out_proj_kernel, tn=tn),
        grid=(n_tok // tm,),
        in_specs=[
            pl.BlockSpec((tm, wa), lambda i: (i, 0)),
            pl.BlockSpec((tm, wb), lambda i: (i, 0)),
            pl.BlockSpec((tm, wc), lambda i: (i, 0)),
            pl.BlockSpec((1, wa), lambda i: (0, 0)),
            pl.BlockSpec((1, wb), lambda i: (0, 0)),
            pl.BlockSpec((1, wc), lambda i: (0, 0)),
            pl.BlockSpec((mix_w, d), lambda i: (0, 0), pipeline_mode=pl.Buffered(1)),
            pl.BlockSpec((tm, d), lambda i: (i, 0)),
        ],
        out_specs=pl.BlockSpec((tm, d), lambda i: (i, 0)),
        out_shape=jax.ShapeDtypeStruct((n_tok, d), F32),
        scratch_shapes=[pltpu.VMEM((tm, mix_w), BF16)],
        compiler_params=_params("arbitrary"),
        name="out_proj",
    )(ya, yb, yc, ga.reshape(1, wa), gb.reshape(1, wb), gc.reshape(1, wc), w, xt)


def _mlp_kernel(x_ref, g_ref, wu_ref, wd_ref, gf_ref, o_ref, h_ref, *, final):
    f = pl.program_id(1)

    @pl.when(f == 0)
    def _():
        x = x_ref[...]
        ms = jnp.mean(x * x, axis=-1, keepdims=True)
        h_ref[...] = ((x * lax.rsqrt(ms + EPS)) * g_ref[...]).astype(BF16)
        o_ref[...] = x

    u = jnp.maximum(jnp.dot(h_ref[...], wu_ref[...], preferred_element_type=F32), 0.0)
    o_ref[...] += jnp.dot((u * u).astype(BF16), wd_ref[...], preferred_element_type=F32)

    if final:
        @pl.when(f == pl.num_programs(1) - 1)
        def _():
            y = o_ref[...]
            ms = jnp.mean(y * y, axis=-1, keepdims=True)
            o_ref[...] = (y * lax.rsqrt(ms + EPS)) * gf_ref[...]


def _mlp(xt, gain, w_up, w_down, final_gain, tiles, final):
    n_tok, d = xt.shape
    d_ff = w_up.shape[1]
    tm, tf = tiles["tm_mlp"], tiles["tf"]
    assert d_ff % tf == 0 and n_tok % tm == 0
    return pl.pallas_call(
        functools.partial(_mlp_kernel, final=final),
        grid=(n_tok // tm, d_ff // tf),
        in_specs=[
            pl.BlockSpec((tm, d), lambda i, f: (i, 0)),
            pl.BlockSpec((1, d), lambda i, f: (0, 0)),
            pl.BlockSpec((d, tf), lambda i, f: (0, f)),
            pl.BlockSpec((tf, d), lambda i, f: (f, 0)),
            pl.BlockSpec((1, d), lambda i, f: (0, 0)),
        ],
        out_specs=pl.BlockSpec((tm, d), lambda i, f: (i, 0)),
        out_shape=jax.ShapeDtypeStruct((n_tok, d), F32),
        scratch_shapes=[pltpu.VMEM((tm, d), BF16)],
        compiler_params=_params("parallel", "arbitrary"),
        name="mlp",
    )(xt, gain.reshape(1, d), w_up, w_down, final_gain.reshape(1, d))


def kernel(x, attn_norm, w_in, sinks, gn_sb, gn_moba, gn_swa, w_out, mlp_norm, w_up, w_down, final_norm):
    b, seq, d = x.shape
    depth = w_in.shape[0]
    n_tok = b * seq
    lay = _layout(d)
    tiles = _tiles(n_tok, seq)
    cos_t, sin_t = _rope_tables(seq)
    xt = x.reshape(n_tok, d)
    for l in range(depth):
        proj = _in_proj(xt, attn_norm[l], w_in[l].astype(BF16), cos_t, sin_t, lay, seq, tiles)
        proj = proj.reshape(b, seq, lay["in_width"])
        ya = _sb_attention(proj, lay).reshape(n_tok, -1)
        yb = _moba_attention(proj, lay).reshape(n_tok, -1)
        yc = _swa_attention(proj, sinks[l], lay).reshape(n_tok, -1)
        xt = _out_proj(ya, yb, yc, gn_sb[l], gn_moba[l], gn_swa[l], w_out[l].astype(BF16), xt, tiles)
        xt = _mlp(xt, mlp_norm[l], w_up[l].astype(BF16), w_down[l].astype(BF16), final_norm,
                  tiles, final=(l == depth - 1))
    return xt.reshape(b, seq, d)
```

```python
import functools

import numpy as np
import jax
import jax.numpy as jnp
from jax import lax
from jax.experimental import pallas as pl
from jax.experimental.pallas import tpu as pltpu

F32 = jnp.float32
BF16 = jnp.bfloat16

HEAD_DIM = 64
PAIR = 2 * HEAD_DIM
ROPE_HALF = HEAD_DIM // 2
MOBA_BLOCK = 256
MOBA_TOPK = 3
WINDOW = 128
SWA_GROUP = 8
ROPE_THETA = 10000.0
EPS = 1e-6
NEG = -1e30
Q_SCALE = HEAD_DIM ** -0.5
MOBA_GROUP = 4
SB_TILE = 256
SB_CHUNK = PAIR
SB_UNDERFLOW = 104.0

VMEM_LIMIT_BYTES = 56 * 1024 * 1024

NT_DIMS = (((1,), (1,)), ((), ()))


def _params(*semantics):
    return pltpu.CompilerParams(dimension_semantics=semantics,
                                vmem_limit_bytes=VMEM_LIMIT_BYTES)


def _lane_index(shape, dtype=jnp.int32):
    idx = lax.broadcasted_iota(jnp.int32, shape, len(shape) - 1)
    return idx if dtype == jnp.int32 else idx.astype(F32).astype(dtype)


def _layout(d_model):
    sb = d_model // 4
    moba = d_model // 4
    swa_q = d_model // 2
    swa_kv = (swa_q // HEAD_DIM // SWA_GROUP) * HEAD_DIM
    sizes = (sb, sb, sb, moba, moba, moba, swa_q, swa_kv, swa_kv)
    offs = np.concatenate([[0], np.cumsum(sizes)]).astype(int)
    names = ("qa", "ka", "va", "qb", "kb", "vb", "qc", "kc", "vc")
    lay = {n: (int(offs[i]), int(sizes[i])) for i, n in enumerate(names)}
    lay["in_width"] = int(offs[-1])
    assert sb % PAIR == 0 and swa_q % PAIR == 0 and swa_kv == PAIR
    return lay


def _tiles(n_tokens, seq):
    tm = min(512, seq)
    assert seq % tm == 0 and n_tokens % tm == 0
    return dict(tm=tm, tm_mlp=min(1024, n_tokens), tn_in=256, tn_out=512, tf=512)


def _rope_tables(seq):
    inv_freq = ROPE_THETA ** (-jnp.arange(ROPE_HALF, dtype=F32) * 2.0 / HEAD_DIM)
    ang = jnp.arange(seq, dtype=F32)[:, None] * inv_freq[None, :]
    cos, sin = jnp.cos(ang), jnp.sin(ang)
    cos_t = jnp.tile(cos, (1, PAIR // ROPE_HALF))
    sin_t = jnp.tile(jnp.concatenate([-sin, sin], axis=1), (1, PAIR // HEAD_DIM))
    return cos_t, sin_t


def _in_proj_kernel(x_ref, g_ref, w_ref, cos_ref, sin_ref, o_ref, h_ref, *, chunks):
    x = x_ref[...]
    ms = jnp.mean(x * x, axis=-1, keepdims=True)
    h_ref[...] = ((x * lax.rsqrt(ms + EPS)) * g_ref[...]).astype(BF16)
    tm = x.shape[0]
    lane = lax.broadcasted_iota(jnp.int32, (tm, PAIR), 1)
    first = (lane % HEAD_DIM) < ROPE_HALF
    for start, classes in chunks:
        width = len(classes) * PAIR
        acc = jnp.dot(h_ref[...], w_ref[:, start:start + width], preferred_element_type=F32)
        for t, (rope, scale) in enumerate(classes):
            a = acc[:, t * PAIR:(t + 1) * PAIR]
            if rope:
                partner = jnp.where(first, pltpu.roll(a, PAIR - ROPE_HALF, axis=1),
                                    pltpu.roll(a, ROPE_HALF, axis=1))
                a = a * cos_ref[...] + partner * sin_ref[...]
            if scale != 1.0:
                a = a * scale
            o_ref[:, start + t * PAIR:start + (t + 1) * PAIR] = a.astype(o_ref.dtype)


def _in_proj(xt, gain, w, layer, cos_t, sin_t, lay, seq, tiles):
    n_tok, d = xt.shape
    in_w = lay["in_width"]
    tm, tn = tiles["tm"], tiles["tn_in"]
    assert in_w % tn == 0 and tn % PAIR == 0
    rope = np.zeros(in_w // PAIR, bool)
    scale = np.ones(in_w // PAIR, np.float32)
    for name in ("qb", "kb", "qc", "kc"):
        o, s = lay[name]
        rope[o // PAIR:(o + s) // PAIR] = True
    for name in ("qa", "qb", "qc"):
        o, s = lay[name]
        scale[o // PAIR:(o + s) // PAIR] = Q_SCALE
    per = tn // PAIR
    chunks = tuple((c * tn, tuple((bool(rope[c * per + t]), float(scale[c * per + t])) for t in range(per)))
                   for c in range(in_w // tn))
    pos_blocks = seq // tm
    return pl.pallas_call(
        functools.partial(_in_proj_kernel, chunks=chunks),
        grid=(n_tok // tm,),
        in_specs=[
            pl.BlockSpec((tm, d), lambda i: (i, 0)),
            pl.BlockSpec((1, d), lambda i: (0, 0)),
            pl.BlockSpec((None, d, in_w), lambda i: (layer, 0, 0), pipeline_mode=pl.Buffered(1)),
            pl.BlockSpec((tm, PAIR), lambda i: (i % pos_blocks, 0)),
            pl.BlockSpec((tm, PAIR), lambda i: (i % pos_blocks, 0)),
        ],
        out_specs=pl.BlockSpec((tm, in_w), lambda i: (i, 0)),
        out_shape=jax.ShapeDtypeStruct((n_tok, in_w), BF16),
        scratch_shapes=[pltpu.VMEM((tm, d), BF16)],
        compiler_params=_params("arbitrary"),
        name="in_proj",
    )(xt, gain.reshape(1, d), w, cos_t, sin_t)


def _sb_kernel(q_ref, k_ref, v_ref, u_ref, o_ref, carry_ref, acc_ref, *, tq):
    i = pl.program_id(2)
    ch = SB_CHUNK
    q = q_ref[0]
    lo = _lane_index((tq, PAIR), BF16) < HEAD_DIM
    zq = jnp.zeros_like(q)
    qs = jnp.concatenate([jnp.where(lo, q, zq), jnp.where(lo, zq, q)], axis=0)
    u = u_ref[...]
    row = lax.broadcasted_iota(jnp.int32, (2 * tq, tq), 0) % tq
    col = lax.broadcasted_iota(jnp.int32, (2 * tq, tq), 1)
    past = col < row

    def tiles(off, n_tiles, carry, acc, diagonal):
        nk = n_tiles * tq
        kblk = k_ref[0, pl.ds(off, nk), :]
        vblk = v_ref[0, pl.ds(off, nk), :]
        z = lax.dot_general(qs, kblk, NT_DIMS, preferred_element_type=F32)
        sp = jnp.maximum(z, 0.0) + jnp.log(1.0 + jnp.exp(-jnp.abs(z)))
        logb = z - sp
        ws = [None] * (nk // ch)
        for c in reversed(range(nk // ch)):
            sl = slice(c * ch, (c + 1) * ch)
            masked = diagonal and c * ch >= nk - tq
            mask = past[:, c * ch - (nk - tq):(c + 1) * ch - (nk - tq)] if masked else None
            s_c = jnp.where(mask, sp[:, sl], 0.0) if masked else sp[:, sl]
            hi = s_c.astype(BF16)
            lo_part = (s_c - hi.astype(F32)).astype(BF16)
            r = jnp.dot(jnp.concatenate([hi, lo_part], axis=1), u, preferred_element_type=F32)
            w = jnp.exp(logb[:, sl] - r[:, :ch] - carry)
            if masked:
                w = jnp.where(mask, w, 0.0)
            ws[c] = w.astype(BF16)
            carry = carry + r[:, ch:]
        wb = jnp.concatenate(ws, axis=1)
        wcat = jnp.concatenate([wb[:tq], wb[tq:]], axis=1)
        lo_v = _lane_index((nk, PAIR), BF16) < HEAD_DIM
        zv = jnp.zeros_like(vblk)
        vcat = jnp.concatenate([jnp.where(lo_v, vblk, zv), jnp.where(lo_v, zv, vblk)], axis=0)
        acc = acc + jnp.dot(wcat, vcat, preferred_element_type=F32)
        return carry, acc

    zero_carry = jnp.zeros((2 * tq, PAIR), F32)
    zero_acc = jnp.zeros((tq, PAIR), F32)

    @pl.when(i == 0)
    def _():
        carry_ref[...], acc_ref[...] = tiles(0, 1, zero_carry, zero_acc, True)

    @pl.when(i > 0)
    def _():
        off = pl.multiple_of((i - 1) * tq, tq)
        carry_ref[...], acc_ref[...] = tiles(off, 2, zero_carry, zero_acc, True)

    def cond(state):
        return (state[0] < i - 1) & state[3]

    def body(state):
        t = state[0]
        off = pl.multiple_of((i - 2 - t) * tq, tq)
        carry, acc = tiles(off, 1, state[1], state[2], False)
        return t + 1, carry, acc, jnp.min(carry) < SB_UNDERFLOW

    carry = carry_ref[...]
    _, _, acc, _ = lax.while_loop(cond, body,
                                  (jnp.int32(0), carry, acc_ref[...], jnp.min(carry) < SB_UNDERFLOW))
    o_ref[0] = acc.astype(o_ref.dtype)


def _sb_attention(proj, lay):
    b, seq, _ = proj.shape
    tq = min(SB_TILE, seq)
    ch = SB_CHUNK
    assert seq % tq == 0 and tq % ch == 0
    q_off, width = lay["qa"]
    k_off, v_off = lay["ka"][0], lay["va"][0]
    pairs = width // PAIR
    tri = np.tril(np.ones((ch, ch), np.float32), -1)
    uu = np.concatenate([tri, np.ones((ch, PAIR), np.float32)], axis=1)
    uu = jnp.asarray(np.concatenate([uu, uu], axis=0), dtype=BF16)
    return pl.pallas_call(
        functools.partial(_sb_kernel, tq=tq),
        grid=(b, pairs, seq // tq),
        in_specs=[
            pl.BlockSpec((1, tq, PAIR), lambda bi, p, i: (bi, i, q_off // PAIR + p)),
            pl.BlockSpec((1, seq, PAIR), lambda bi, p, i: (bi, 0, k_off // PAIR + p)),
            pl.BlockSpec((1, seq, PAIR), lambda bi, p, i: (bi, 0, v_off // PAIR + p)),
            pl.BlockSpec((2 * ch, ch + PAIR), lambda bi, p, i: (0, 0)),
        ],
        out_specs=pl.BlockSpec((1, tq, PAIR), lambda bi, p, i: (bi, i, p)),
        out_shape=jax.ShapeDtypeStruct((b, seq, width), BF16),
        scratch_shapes=[pltpu.VMEM((2 * tq, PAIR), F32), pltpu.VMEM((tq, PAIR), F32)],
        compiler_params=_params("parallel", "parallel", "arbitrary"),
        name="sb_attention",
    )(proj, proj, proj, uu)


def _moba_kernel(q_ref, k_ref, v_ref, o_ref, kmean_ref, *, nb, nbp, group):
    i = pl.program_id(2)
    blk = MOBA_BLOCK
    tq = blk

    @pl.when(i == 0)
    def _():
        kf = k_ref[0].astype(F32).reshape(nb, blk, PAIR)
        km = jnp.sum(kf, axis=1) * (1.0 / blk)
        if nbp > nb:
            km = jnp.concatenate([km, jnp.zeros((nbp - nb, PAIR), F32)], axis=0)
        kmean_ref[...] = km

    q = q_ref[0]
    lo_qb = _lane_index((tq, PAIR), BF16) < HEAD_DIM
    lo_m = _lane_index((nbp, PAIR)) < HEAD_DIM
    jidx = lax.broadcasted_iota(jnp.int32, (nbp, tq), 0)
    valid = jidx < i
    km = kmean_ref[...]
    zq = jnp.zeros_like(q)

    q_plain, q_aug = [], []
    for h in (0, 1):
        head_m = lo_m if h == 0 else jnp.logical_not(lo_m)
        head_q = lo_qb if h == 0 else jnp.logical_not(lo_qb)
        kmh = jnp.where(head_m, km, 0.0)
        a = kmh.astype(BF16)
        r1 = kmh - a.astype(F32)
        b2 = r1.astype(BF16)
        c3 = (r1 - b2.astype(F32)).astype(BF16)
        g3 = lax.dot_general(jnp.concatenate([a, b2, c3], axis=0), q, NT_DIMS,
                             preferred_element_type=F32)
        gate = g3[:nbp] + g3[nbp:2 * nbp] + g3[2 * nbp:]
        gate = jnp.where(valid, gate, -jnp.inf)
        beaten_by = jnp.zeros((nbp, tq), jnp.int32)
        for jp in range(nb):
            other = gate[jp:jp + 1, :]
            beats = (other > gate) | ((other == gate) & (jidx > jp))
            beaten_by = beaten_by + beats.astype(jnp.int32)
        sel = valid & (beaten_by < MOBA_TOPK)
        bias_t = jnp.where(sel, 0.0, NEG)
        top = HEAD_DIM if h == 0 else 0
        pieces = [jnp.zeros((top, tq), F32)] if top else []
        pieces.append(bias_t)
        if PAIR - top - nbp:
            pieces.append(jnp.zeros((PAIR - top - nbp, tq), F32))
        placed = jnp.concatenate(pieces, axis=0).T
        q_plain.append(jnp.where(head_q, q, zq))
        q_aug.append(jnp.where(head_q, q, placed.astype(BF16)))

    lo_q = _lane_index((tq, PAIR)) < HEAD_DIM
    lo_q2 = jnp.concatenate([lo_q, lo_q], axis=1)

    def attend(s0, s1, vblk, m0, m1, acc):
        n0 = jnp.maximum(m0, jnp.max(s0, axis=1, keepdims=True))
        n1 = jnp.maximum(m1, jnp.max(s1, axis=1, keepdims=True))
        p0 = jnp.exp(s0 - n0).astype(BF16)
        p1 = jnp.exp(s1 - n1).astype(BF16)
        alpha = jnp.where(lo_q, jnp.exp(m0 - n0), jnp.exp(m1 - n1))
        rhs = jnp.concatenate([vblk, jnp.ones_like(vblk)], axis=1)
        u0 = jnp.dot(p0, rhs, preferred_element_type=F32)
        u1 = jnp.dot(p1, rhs, preferred_element_type=F32)
        acc = acc * jnp.concatenate([alpha, alpha], axis=1) + jnp.where(lo_q2, u0, u1)
        return n0, n1, acc

    own_off = pl.multiple_of(i * blk, blk)
    k_own = k_ref[0, pl.ds(own_off, blk), :]
    row = lax.broadcasted_iota(jnp.int32, (tq, blk), 0)
    col = lax.broadcasted_iota(jnp.int32, (tq, blk), 1)
    causal = col <= row
    s_own = [jnp.where(causal, lax.dot_general(qh, k_own, NT_DIMS, preferred_element_type=F32), NEG)
             for qh in q_plain]
    m_init = jnp.full((tq, 1), NEG, F32)
    state = attend(s_own[0], s_own[1], v_ref[0, pl.ds(own_off, blk), :],
                   m_init, m_init, jnp.zeros((tq, 2 * PAIR), F32))

    gk = group * blk
    lane_g = _lane_index((gk, PAIR), BF16)
    lo_g = lane_g < HEAD_DIM
    blk_in_group = (lax.broadcasted_iota(jnp.int32, (gk, PAIR), 0) // blk).astype(F32).astype(BF16)

    def body(g, st):
        o = pl.multiple_of(g * gk, gk)
        kg = k_ref[0, pl.ds(o, gk), :]
        one = jnp.ones_like(kg)
        zk = jnp.zeros_like(kg)
        first = jnp.full((1, PAIR), g * group, jnp.int32).astype(F32).astype(BF16)
        blk_id = blk_in_group + first
        k0 = jnp.where(lo_g, kg, jnp.where(lane_g == blk_id + HEAD_DIM, one, zk))
        k1 = jnp.where(lo_g, jnp.where(lane_g == blk_id, one, zk), kg)
        s0 = lax.dot_general(q_aug[0], k0, NT_DIMS, preferred_element_type=F32)
        s1 = lax.dot_general(q_aug[1], k1, NT_DIMS, preferred_element_type=F32)
        return attend(s0, s1, v_ref[0, pl.ds(o, gk), :], *st)

    _, _, acc = lax.fori_loop(0, (i + group - 1) // group, body, state)
    o_ref[0] = (acc[:, :PAIR] / acc[:, PAIR:]).astype(o_ref.dtype)


def _moba_attention(proj, lay):
    b, seq, _ = proj.shape
    tq = MOBA_BLOCK
    assert seq % MOBA_BLOCK == 0
    nb = seq // MOBA_BLOCK
    nbp = -(-nb // 8) * 8
    group = min(MOBA_GROUP, nb)
    assert nbp <= HEAD_DIM
    assert nb % group == 0
    q_off, width = lay["qb"]
    k_off, v_off = lay["kb"][0], lay["vb"][0]
    return pl.pallas_call(
        functools.partial(_moba_kernel, nb=nb, nbp=nbp, group=group),
        grid=(b, width // PAIR, seq // tq),
        in_specs=[
            pl.BlockSpec((1, tq, PAIR), lambda bi, p, i: (bi, i, q_off // PAIR + p)),
            pl.BlockSpec((1, seq, PAIR), lambda bi, p, i: (bi, 0, k_off // PAIR + p)),
            pl.BlockSpec((1, seq, PAIR), lambda bi, p, i: (bi, 0, v_off // PAIR + p)),
        ],
        out_specs=pl.BlockSpec((1, tq, PAIR), lambda bi, p, i: (bi, i, p)),
        out_shape=jax.ShapeDtypeStruct((b, seq, width), BF16),
        scratch_shapes=[pltpu.VMEM((nbp, PAIR), F32)],
        compiler_params=_params("parallel", "parallel", "arbitrary"),
        name="moba_attention",
    )(proj, proj, proj)


def _swa_kernel(sink_ref, q_ref, kp_ref, kc_ref, vp_ref, vc_ref, o_ref, *, n_pairs):
    n = pl.program_id(1)
    w = WINDOW
    k = jnp.concatenate([kp_ref[0], kc_ref[0]], axis=0).astype(F32)
    v = jnp.concatenate([vp_ref[0], vc_ref[0]], axis=0).astype(F32)
    lo_k = lax.broadcasted_iota(jnp.int32, (2 * w, PAIR), 1) < HEAD_DIM
    k_sw = pltpu.roll(k, HEAD_DIM, axis=1)
    v_sw = pltpu.roll(v, HEAD_DIM, axis=1)
    kk = [jnp.where(lo_k, k, k_sw).astype(BF16), jnp.where(lo_k, k_sw, k).astype(BF16)]
    vv = [jnp.where(lo_k, v, v_sw).astype(BF16), jnp.where(lo_k, v_sw, v).astype(BF16)]
    ones = jnp.ones((2 * w, PAIR), BF16)
    rhs = [jnp.concatenate([vg, ones], axis=1) for vg in vv]

    row = lax.broadcasted_iota(jnp.int32, (w, 2 * w), 0)
    col = lax.broadcasted_iota(jnp.int32, (w, 2 * w), 1)
    delta = row + w - col
    valid = (delta >= 0) & (delta < w) & ((n - 1) * w + col >= 0)
    lo_q = _lane_index((w, PAIR)) < HEAD_DIM
    lo_qb = _lane_index((w, PAIR), BF16) < HEAD_DIM
    pairs_per_kv = SWA_GROUP // 2

    for p in range(n_pairs):
        g = p // pairs_per_kv
        qp = q_ref[0, :, p * PAIR:(p + 1) * PAIR]
        zq = jnp.zeros_like(qp)
        outs = []
        for h in (0, 1):
            qh = jnp.where(lo_qb, qp, zq) if h == 0 else jnp.where(lo_qb, zq, qp)
            s = lax.dot_general(qh, kk[g], NT_DIMS, preferred_element_type=F32)
            s = jnp.where(valid, s, NEG)
            sink = sink_ref[2 * p + h]
            m = jnp.maximum(jnp.max(s, axis=1, keepdims=True), sink)
            pr = jnp.exp(s - m).astype(BF16)
            o2 = jnp.dot(pr, rhs[g], preferred_element_type=F32)
            outs.append(o2[:, :PAIR] / (o2[:, PAIR:] + jnp.exp(sink - m)))
        o_ref[0, :, p * PAIR:(p + 1) * PAIR] = jnp.where(lo_q, outs[0], outs[1]).astype(o_ref.dtype)


def _swa_attention(proj, sinks, lay):
    b, seq, _ = proj.shape
    w = WINDOW
    q_off, width = lay["qc"]
    k_off, v_off = lay["kc"][0], lay["vc"][0]
    assert q_off % width == 0 and seq % w == 0
    return pl.pallas_call(
        functools.partial(_swa_kernel, n_pairs=width // PAIR),
        grid=(b, seq // w),
        in_specs=[
            pl.BlockSpec(memory_space=pltpu.SMEM),
            pl.BlockSpec((1, w, width), lambda bi, n: (bi, n, q_off // width)),
            pl.BlockSpec((1, w, PAIR), lambda bi, n: (bi, jnp.maximum(n - 1, 0), k_off // PAIR)),
            pl.BlockSpec((1, w, PAIR), lambda bi, n: (bi, n, k_off // PAIR)),
            pl.BlockSpec((1, w, PAIR), lambda bi, n: (bi, jnp.maximum(n - 1, 0), v_off // PAIR)),
            pl.BlockSpec((1, w, PAIR), lambda bi, n: (bi, n, v_off // PAIR)),
        ],
        out_specs=pl.BlockSpec((1, w, width), lambda bi, n: (bi, n, 0)),
        out_shape=jax.ShapeDtypeStruct((b, seq, width), BF16),
        compiler_params=_params("parallel", "arbitrary"),
        name="swa_attention",
    )(sinks.astype(F32), proj, proj, proj, proj, proj)


def _out_proj_kernel(ya_ref, yb_ref, yc_ref, ga_ref, gb_ref, gc_ref, w_ref, x_ref, o_ref, mix_ref,
                     *, tn):
    start = 0
    for y_ref, g_ref in ((ya_ref, ga_ref), (yb_ref, gb_ref), (yc_ref, gc_ref)):
        y = y_ref[...].astype(F32)
        ms = jnp.mean(y * y, axis=-1, keepdims=True)
        width = y.shape[1]
        mix_ref[:, start:start + width] = ((y * lax.rsqrt(ms + EPS)) * g_ref[...]).astype(BF16)
        start += width
    for c in range(o_ref.shape[1] // tn):
        cols = slice(c * tn, (c + 1) * tn)
        o_ref[:, cols] = x_ref[:, cols] + jnp.dot(mix_ref[...], w_ref[:, cols],
                                                  preferred_element_type=F32)


def _out_proj(ya, yb, yc, ga, gb, gc, w, layer, xt, tiles):
    n_tok, d = xt.shape
    tm, tn = tiles["tm"], tiles["tn_out"]
    wa, wb, wc = ya.shape[1], yb.shape[1], yc.shape[1]
    mix_w = wa + wb + wc
    assert d % tn == 0 and w.shape[1:] == (mix_w, d)
    return pl.pallas_call(
        functools.partial(_out_proj_kernel, tn=tn),
        grid=(n_tok // tm,),
        in_specs=[
            pl.BlockSpec((tm, wa), lambda i: (i, 0)),
            pl.BlockSpec((tm, wb), lambda i: (i, 0)),
            pl.BlockSpec((tm, wc), lambda i: (i, 0)),
            pl.BlockSpec((1, wa), lambda i: (0, 0)),
            pl.BlockSpec((1, wb), lambda i: (0, 0)),
            pl.BlockSpec((1, wc), lambda i: (0, 0)),
            pl.BlockSpec((None, mix_w, d), lambda i: (layer, 0, 0), pipeline_mode=pl.Buffered(1)),
            pl.BlockSpec((tm, d), lambda i: (i, 0)),
        ],
        out_specs=pl.BlockSpec((tm, d), lambda i: (i, 0)),
        out_shape=jax.ShapeDtypeStruct((n_tok, d), F32),
        scratch_shapes=[pltpu.VMEM((tm, mix_w), BF16)],
        compiler_params=_params("arbitrary"),
        name="out_proj",
    )(ya, yb, yc, ga.reshape(1, wa), gb.reshape(1, wb), gc.reshape(1, wc), w, xt)


def _mlp_kernel(x_ref, g_ref, wu_ref, wd_ref, gf_ref, o_ref, h_ref, *, final):
    f = pl.program_id(1)

    @pl.when(f == 0)
    def _():
        x = x_ref[...]
        ms = jnp.mean(x * x, axis=-1, keepdims=True)
        h_ref[...] = ((x * lax.rsqrt(ms + EPS)) * g_ref[...]).astype(BF16)
        o_ref[...] = x

    u = jnp.maximum(jnp.dot(h_ref[...], wu_ref[...], preferred_element_type=F32), 0.0)
    o_ref[...] += jnp.dot((u * u).astype(BF16), wd_ref[...], preferred_element_type=F32)

    if final:
        @pl.when(f == pl.num_programs(1) - 1)
        def _():
            y = o_ref[...]
            ms = jnp.mean(y * y, axis=-1, keepdims=True)
            o_ref[...] = (y * lax.rsqrt(ms + EPS)) * gf_ref[...]


def _mlp(xt, gain, w_up, w_down, layer, final_gain, tiles, final):
    n_tok, d = xt.shape
    d_ff = w_up.shape[2]
    tm, tf = tiles["tm_mlp"], tiles["tf"]
    assert d_ff % tf == 0 and n_tok % tm == 0
    return pl.pallas_call(
        functools.partial(_mlp_kernel, final=final),
        grid=(n_tok // tm, d_ff // tf),
        in_specs=[
            pl.BlockSpec((tm, d), lambda i, f: (i, 0)),
            pl.BlockSpec((1, d), lambda i, f: (0, 0)),
            pl.BlockSpec((None, d, tf), lambda i, f: (layer, 0, f)),
            pl.BlockSpec((None, tf, d), lambda i, f: (layer, f, 0)),
            pl.BlockSpec((1, d), lambda i, f: (0, 0)),
        ],
        out_specs=pl.BlockSpec((tm, d), lambda i, f: (i, 0)),
        out_shape=jax.ShapeDtypeStruct((n_tok, d), F32),
        scratch_shapes=[pltpu.VMEM((tm, d), BF16)],
        compiler_params=_params("parallel", "arbitrary"),
        name="mlp",
    )(xt, gain.reshape(1, d), w_up, w_down, final_gain.reshape(1, d))


def kernel(x, attn_norm, w_in, sinks, gn_sb, gn_moba, gn_swa, w_out, mlp_norm, w_up, w_down, final_norm):
    b, seq, d = x.shape
    depth = w_in.shape[0]
    n_tok = b * seq
    lay = _layout(d)
    tiles = _tiles(n_tok, seq)
    cos_t, sin_t = _rope_tables(seq)
    xt = x.reshape(n_tok, d)
    w_in, w_out, w_up, w_down = (w.astype(BF16) for w in (w_in, w_out, w_up, w_down))
    for l in range(depth):
        proj = _in_proj(xt, attn_norm[l], w_in, l, cos_t, sin_t, lay, seq, tiles)
        proj = proj.reshape(b, seq, lay["in_width"])
        ya = _sb_attention(proj, lay).reshape(n_tok, -1)
        yb = _moba_attention(proj, lay).reshape(n_tok, -1)
        yc = _swa_attention(proj, sinks[l], lay).reshape(n_tok, -1)
        xt = _out_proj(ya, yb, yc, gn_sb[l], gn_moba[l], gn_swa[l], w_out, l, xt, tiles)
        xt = _mlp(xt, mlp_norm[l], w_up, w_down, l, final_norm, tiles, final=(l == depth - 1))
    return xt.reshape(b, seq, d)
```

```python
import functools

import numpy as np
import jax
import jax.numpy as jnp
from jax import lax
from jax.experimental import pallas as pl
from jax.experimental.pallas import tpu as pltpu

F32 = jnp.float32
BF16 = jnp.bfloat16

HEAD_DIM = 64
PAIR = 2 * HEAD_DIM
ROPE_HALF = HEAD_DIM // 2
MOBA_BLOCK = 256
MOBA_TOPK = 3
WINDOW = 128
SWA_GROUP = 8
ROPE_THETA = 10000.0
EPS = 1e-6
NEG = -1e30
Q_SCALE = HEAD_DIM ** -0.5
MOBA_GROUP = 4
MOBA_PAIRS_PER_STEP = 2
SB_TILE = 256
SB_CHUNK = PAIR
SB_PAIRS_PER_STEP = 2
SB_UNDERFLOW = 104.0

VMEM_LIMIT_BYTES = 56 * 1024 * 1024

NT_DIMS = (((1,), (1,)), ((), ()))


def _params(*semantics):
    return pltpu.CompilerParams(dimension_semantics=semantics,
                                vmem_limit_bytes=VMEM_LIMIT_BYTES)


def _lane_index(shape, dtype=jnp.int32):
    idx = lax.broadcasted_iota(jnp.int32, shape, len(shape) - 1)
    return idx if dtype == jnp.int32 else idx.astype(F32).astype(dtype)


def _layout(d_model):
    sb = d_model // 4
    moba = d_model // 4
    swa_q = d_model // 2
    swa_kv = (swa_q // HEAD_DIM // SWA_GROUP) * HEAD_DIM
    sizes = (sb, sb, sb, moba, moba, moba, swa_q, swa_kv, swa_kv)
    offs = np.concatenate([[0], np.cumsum(sizes)]).astype(int)
    names = ("qa", "ka", "va", "qb", "kb", "vb", "qc", "kc", "vc")
    lay = {n: (int(offs[i]), int(sizes[i])) for i, n in enumerate(names)}
    lay["in_width"] = int(offs[-1])
    assert sb % PAIR == 0 and swa_q % PAIR == 0 and swa_kv == PAIR
    return lay


def _tiles(n_tokens, seq):
    tm = min(512, seq)
    assert seq % tm == 0 and n_tokens % tm == 0
    return dict(tm=tm, tm_mlp=min(1024, n_tokens), tn_in=256, tn_out=512, tf=512)


def _rope_tables(seq):
    inv_freq = ROPE_THETA ** (-jnp.arange(ROPE_HALF, dtype=F32) * 2.0 / HEAD_DIM)
    ang = jnp.arange(seq, dtype=F32)[:, None] * inv_freq[None, :]
    cos, sin = jnp.cos(ang), jnp.sin(ang)
    cos_t = jnp.tile(cos, (1, PAIR // ROPE_HALF))
    sin_t = jnp.tile(jnp.concatenate([-sin, sin], axis=1), (1, PAIR // HEAD_DIM))
    return cos_t, sin_t


def _in_proj_kernel(x_ref, g_ref, w_ref, cos_ref, sin_ref, o_ref, h_ref, *, chunks):
    x = x_ref[...]
    ms = jnp.mean(x * x, axis=-1, keepdims=True)
    h_ref[...] = ((x * lax.rsqrt(ms + EPS)) * g_ref[...]).astype(BF16)
    tm = x.shape[0]
    lane = lax.broadcasted_iota(jnp.int32, (tm, PAIR), 1)
    first = (lane % HEAD_DIM) < ROPE_HALF
    for start, classes in chunks:
        width = len(classes) * PAIR
        acc = jnp.dot(h_ref[...], w_ref[:, start:start + width], preferred_element_type=F32)
        for t, (rope, scale) in enumerate(classes):
            a = acc[:, t * PAIR:(t + 1) * PAIR]
            if rope:
                partner = jnp.where(first, pltpu.roll(a, PAIR - ROPE_HALF, axis=1),
                                    pltpu.roll(a, ROPE_HALF, axis=1))
                a = a * cos_ref[...] + partner * sin_ref[...]
            if scale != 1.0:
                a = a * scale
            o_ref[:, start + t * PAIR:start + (t + 1) * PAIR] = a.astype(o_ref.dtype)


def _in_proj(xt, gain, w, layer, cos_t, sin_t, lay, seq, tiles):
    n_tok, d = xt.shape
    in_w = lay["in_width"]
    tm, tn = tiles["tm"], tiles["tn_in"]
    assert in_w % tn == 0 and tn % PAIR == 0
    rope = np.zeros(in_w // PAIR, bool)
    scale = np.ones(in_w // PAIR, np.float32)
    for name in ("qb", "kb", "qc", "kc"):
        o, s = lay[name]
        rope[o // PAIR:(o + s) // PAIR] = True
    for name in ("qa", "qb", "qc"):
        o, s = lay[name]
        scale[o // PAIR:(o + s) // PAIR] = Q_SCALE
    per = tn // PAIR
    chunks = tuple((c * tn, tuple((bool(rope[c * per + t]), float(scale[c * per + t])) for t in range(per)))
                   for c in range(in_w // tn))
    pos_blocks = seq // tm
    return pl.pallas_call(
        functools.partial(_in_proj_kernel, chunks=chunks),
        grid=(n_tok // tm,),
        in_specs=[
            pl.BlockSpec((tm, d), lambda i: (i, 0)),
            pl.BlockSpec((1, d), lambda i: (0, 0)),
            pl.BlockSpec((None, d, in_w), lambda i: (layer, 0, 0), pipeline_mode=pl.Buffered(1)),
            pl.BlockSpec((tm, PAIR), lambda i: (i % pos_blocks, 0)),
            pl.BlockSpec((tm, PAIR), lambda i: (i % pos_blocks, 0)),
        ],
        out_specs=pl.BlockSpec((tm, in_w), lambda i: (i, 0)),
        out_shape=jax.ShapeDtypeStruct((n_tok, in_w), BF16),
        scratch_shapes=[pltpu.VMEM((tm, d), BF16)],
        compiler_params=_params("arbitrary"),
        name="in_proj",
    )(xt, gain.reshape(1, d), w, cos_t, sin_t)


def _sb_kernel(q_ref, k_ref, v_ref, u_ref, o_ref, carry_ref, acc_ref, *, tq):
    i = pl.program_id(2)
    ch = SB_CHUNK
    n_pairs = q_ref.shape[2] // PAIR

    def lanes(p):
        return slice(p * PAIR, (p + 1) * PAIR)

    lo = _lane_index((tq, PAIR), BF16) < HEAD_DIM
    qs = []
    for p in range(n_pairs):
        q = q_ref[0, :, lanes(p)]
        zq = jnp.zeros_like(q)
        qs.append(jnp.concatenate([jnp.where(lo, q, zq), jnp.where(lo, zq, q)], axis=0))
    u = u_ref[...]
    row = lax.broadcasted_iota(jnp.int32, (2 * tq, tq), 0) % tq
    col = lax.broadcasted_iota(jnp.int32, (2 * tq, tq), 1)
    past = col < row

    def tiles(off, n_tiles, carries, accs, diagonal):
        nk = n_tiles * tq
        kblk = k_ref[0, pl.ds(off, nk), :]
        vblk = v_ref[0, pl.ds(off, nk), :]
        zs = [lax.dot_general(qs[p], kblk[:, lanes(p)], NT_DIMS, preferred_element_type=F32)
              for p in range(n_pairs)]
        sps = [jnp.maximum(z, 0.0) + jnp.log(1.0 + jnp.exp(-jnp.abs(z))) for z in zs]
        ws = [[None] * (nk // ch) for _ in range(n_pairs)]
        carries = list(carries)
        for c in reversed(range(nk // ch)):
            sl = slice(c * ch, (c + 1) * ch)
            masked = diagonal and c * ch >= nk - tq
            mask = past[:, c * ch - (nk - tq):(c + 1) * ch - (nk - tq)] if masked else None
            for p in range(n_pairs):
                s_c = jnp.where(mask, sps[p][:, sl], 0.0) if masked else sps[p][:, sl]
                hi = s_c.astype(BF16)
                lo_part = (s_c - hi.astype(F32)).astype(BF16)
                r = jnp.dot(jnp.concatenate([hi, lo_part], axis=1), u, preferred_element_type=F32)
                w = jnp.exp(zs[p][:, sl] - sps[p][:, sl] - r[:, :ch] - carries[p])
                if masked:
                    w = jnp.where(mask, w, 0.0)
                ws[p][c] = w.astype(BF16)
                carries[p] = carries[p] + r[:, ch:]
        lo_v = _lane_index((nk, PAIR), BF16) < HEAD_DIM
        accs = list(accs)
        for p in range(n_pairs):
            wb = jnp.concatenate(ws[p], axis=1)
            wcat = jnp.concatenate([wb[:tq], wb[tq:]], axis=1)
            vp = vblk[:, lanes(p)]
            zv = jnp.zeros_like(vp)
            vcat = jnp.concatenate([jnp.where(lo_v, vp, zv), jnp.where(lo_v, zv, vp)], axis=0)
            accs[p] = accs[p] + jnp.dot(wcat, vcat, preferred_element_type=F32)
        return carries, accs

    zero_carry = [jnp.zeros((2 * tq, PAIR), F32)] * n_pairs
    zero_acc = [jnp.zeros((tq, PAIR), F32)] * n_pairs

    def first_pass(off, n_tiles):
        carries, accs = tiles(off, n_tiles, zero_carry, zero_acc, True)
        for p in range(n_pairs):
            carry_ref[p] = carries[p]
            acc_ref[p] = accs[p]

    @pl.when(i == 0)
    def _():
        first_pass(0, 1)

    @pl.when(i > 0)
    def _():
        first_pass(pl.multiple_of((i - 1) * tq, tq), 2)

    def unfinished(carries):
        return functools.reduce(jnp.minimum, [jnp.min(c) for c in carries]) < SB_UNDERFLOW

    def cond(state):
        return (state[0] < i - 1) & state[1]

    def body(state):
        t = state[0]
        off = pl.multiple_of((i - 2 - t) * tq, tq)
        carries, accs = tiles(off, 1, state[2:2 + n_pairs], state[2 + n_pairs:], False)
        return (t + 1, unfinished(carries), *carries, *accs)

    carries = [carry_ref[p] for p in range(n_pairs)]
    accs = [acc_ref[p] for p in range(n_pairs)]
    state = lax.while_loop(cond, body, (jnp.int32(0), unfinished(carries), *carries, *accs))
    for p in range(n_pairs):
        o_ref[0, :, lanes(p)] = state[2 + n_pairs + p].astype(o_ref.dtype)


def _sb_attention(proj, lay):
    b, seq, _ = proj.shape
    tq = min(SB_TILE, seq)
    ch = SB_CHUNK
    assert seq % tq == 0 and tq % ch == 0
    q_off, width = lay["qa"]
    k_off, v_off = lay["ka"][0], lay["va"][0]
    n_pairs = SB_PAIRS_PER_STEP
    lw = n_pairs * PAIR
    assert width % lw == 0 and q_off % lw == 0 and k_off % lw == 0 and v_off % lw == 0
    tri = np.tril(np.ones((ch, ch), np.float32), -1)
    uu = np.concatenate([tri, np.ones((ch, PAIR), np.float32)], axis=1)
    uu = jnp.asarray(np.concatenate([uu, uu], axis=0), dtype=BF16)
    return pl.pallas_call(
        functools.partial(_sb_kernel, tq=tq),
        grid=(b, width // lw, seq // tq),
        in_specs=[
            pl.BlockSpec((1, tq, lw), lambda bi, p, i: (bi, i, q_off // lw + p)),
            pl.BlockSpec((1, seq, lw), lambda bi, p, i: (bi, 0, k_off // lw + p)),
            pl.BlockSpec((1, seq, lw), lambda bi, p, i: (bi, 0, v_off // lw + p)),
            pl.BlockSpec((2 * ch, ch + PAIR), lambda bi, p, i: (0, 0)),
        ],
        out_specs=pl.BlockSpec((1, tq, lw), lambda bi, p, i: (bi, i, p)),
        out_shape=jax.ShapeDtypeStruct((b, seq, width), BF16),
        scratch_shapes=[pltpu.VMEM((n_pairs, 2 * tq, PAIR), F32), pltpu.VMEM((n_pairs, tq, PAIR), F32)],
        compiler_params=_params("parallel", "parallel", "arbitrary"),
        name="sb_attention",
    )(proj, proj, proj, uu)


def _moba_kernel(q_ref, k_ref, v_ref, o_ref, kmean_ref, *, nb, nbp, group):
    i = pl.program_id(2)
    blk = MOBA_BLOCK
    tq = blk
    n_pairs = q_ref.shape[2] // PAIR

    def lanes(p):
        return slice(p * PAIR, (p + 1) * PAIR)

    @pl.when(i == 0)
    def _():
        kf = k_ref[0].astype(F32).reshape(nb, blk, n_pairs * PAIR)
        km = jnp.sum(kf, axis=1) * (1.0 / blk)
        if nbp > nb:
            km = jnp.concatenate([km, jnp.zeros((nbp - nb, n_pairs * PAIR), F32)], axis=0)
        kmean_ref[...] = km

    lo_qb = _lane_index((tq, PAIR), BF16) < HEAD_DIM
    lo_m = _lane_index((nbp, PAIR)) < HEAD_DIM
    jidx = lax.broadcasted_iota(jnp.int32, (nbp, tq), 0)
    valid = jidx < i

    def gated_queries(q, km):
        zq = jnp.zeros_like(q)
        plain, aug = [], []
        for h in (0, 1):
            head_m = lo_m if h == 0 else jnp.logical_not(lo_m)
            head_q = lo_qb if h == 0 else jnp.logical_not(lo_qb)
            kmh = jnp.where(head_m, km, 0.0)
            a = kmh.astype(BF16)
            r1 = kmh - a.astype(F32)
            b2 = r1.astype(BF16)
            c3 = (r1 - b2.astype(F32)).astype(BF16)
            g3 = lax.dot_general(jnp.concatenate([a, b2, c3], axis=0), q, NT_DIMS,
                                 preferred_element_type=F32)
            gate = g3[:nbp] + g3[nbp:2 * nbp] + g3[2 * nbp:]
            gate = jnp.where(valid, gate, -jnp.inf)
            beaten_by = jnp.zeros((nbp, tq), jnp.int32)
            for jp in range(nb):
                other = gate[jp:jp + 1, :]
                beats = (other > gate) | ((other == gate) & (jidx > jp))
                beaten_by = beaten_by + beats.astype(jnp.int32)
            sel = valid & (beaten_by < MOBA_TOPK)
            bias_t = jnp.where(sel, 0.0, NEG)
            top = HEAD_DIM if h == 0 else 0
            pieces = [jnp.zeros((top, tq), F32)] if top else []
            pieces.append(bias_t)
            if PAIR - top - nbp:
                pieces.append(jnp.zeros((PAIR - top - nbp, tq), F32))
            placed = jnp.concatenate(pieces, axis=0).T
            plain.append(jnp.where(head_q, q, zq))
            aug.append(jnp.where(head_q, q, placed.astype(BF16)))
        return plain, aug

    q_plain, q_aug = [], []
    for p in range(n_pairs):
        plain, aug = gated_queries(q_ref[0, :, lanes(p)], kmean_ref[:, lanes(p)])
        q_plain.append(plain)
        q_aug.append(aug)

    lo_q = _lane_index((tq, PAIR)) < HEAD_DIM
    lo_q2 = jnp.concatenate([lo_q, lo_q], axis=1)

    def attend(s0, s1, vblk, m0, m1, acc):
        n0 = jnp.maximum(m0, jnp.max(s0, axis=1, keepdims=True))
        n1 = jnp.maximum(m1, jnp.max(s1, axis=1, keepdims=True))
        p0 = jnp.exp(s0 - n0).astype(BF16)
        p1 = jnp.exp(s1 - n1).astype(BF16)
        alpha = jnp.where(lo_q, jnp.exp(m0 - n0), jnp.exp(m1 - n1))
        rhs = jnp.concatenate([vblk, jnp.ones_like(vblk)], axis=1)
        u0 = jnp.dot(p0, rhs, preferred_element_type=F32)
        u1 = jnp.dot(p1, rhs, preferred_element_type=F32)
        acc = acc * jnp.concatenate([alpha, alpha], axis=1) + jnp.where(lo_q2, u0, u1)
        return n0, n1, acc

    own_off = pl.multiple_of(i * blk, blk)
    k_own = k_ref[0, pl.ds(own_off, blk), :]
    v_own = v_ref[0, pl.ds(own_off, blk), :]
    row = lax.broadcasted_iota(jnp.int32, (tq, blk), 0)
    col = lax.broadcasted_iota(jnp.int32, (tq, blk), 1)
    causal = col <= row
    m_init = jnp.full((tq, 1), NEG, F32)
    state = []
    for p in range(n_pairs):
        s_own = [jnp.where(causal, lax.dot_general(qh, k_own[:, lanes(p)], NT_DIMS,
                                                   preferred_element_type=F32), NEG)
                 for qh in q_plain[p]]
        state += attend(s_own[0], s_own[1], v_own[:, lanes(p)],
                        m_init, m_init, jnp.zeros((tq, 2 * PAIR), F32))

    gk = group * blk
    lane_g = _lane_index((gk, PAIR), BF16)
    lo_g = lane_g < HEAD_DIM
    blk_in_group = (lax.broadcasted_iota(jnp.int32, (gk, PAIR), 0) // blk).astype(F32).astype(BF16)
    one = jnp.ones((gk, PAIR), BF16)
    zk = jnp.zeros((gk, PAIR), BF16)

    def body(g, st):
        o = pl.multiple_of(g * gk, gk)
        kg = k_ref[0, pl.ds(o, gk), :]
        vg = v_ref[0, pl.ds(o, gk), :]
        first = jnp.full((1, PAIR), g * group, jnp.int32).astype(F32).astype(BF16)
        blk_id = blk_in_group + first
        ind0 = jnp.where(lane_g == blk_id + HEAD_DIM, one, zk)
        ind1 = jnp.where(lane_g == blk_id, one, zk)
        scores = []
        for p in range(n_pairs):
            kp = kg[:, lanes(p)]
            scores.append((
                lax.dot_general(q_aug[p][0], jnp.where(lo_g, kp, ind0), NT_DIMS, preferred_element_type=F32),
                lax.dot_general(q_aug[p][1], jnp.where(lo_g, ind1, kp), NT_DIMS, preferred_element_type=F32)))
        new = []
        for p in range(n_pairs):
            new += attend(scores[p][0], scores[p][1], vg[:, lanes(p)], *st[3 * p:3 * p + 3])
        return tuple(new)

    state = lax.fori_loop(0, (i + group - 1) // group, body, tuple(state))
    for p in range(n_pairs):
        acc = state[3 * p + 2]
        o_ref[0, :, lanes(p)] = (acc[:, :PAIR] / acc[:, PAIR:]).astype(o_ref.dtype)


def _moba_attention(proj, lay):
    b, seq, _ = proj.shape
    tq = MOBA_BLOCK
    assert seq % MOBA_BLOCK == 0
    nb = seq // MOBA_BLOCK
    nbp = -(-nb // 8) * 8
    group = min(MOBA_GROUP, nb)
    assert nbp <= HEAD_DIM
    assert nb % group == 0
    q_off, width = lay["qb"]
    k_off, v_off = lay["kb"][0], lay["vb"][0]
    lw = MOBA_PAIRS_PER_STEP * PAIR
    assert width % lw == 0 and q_off % lw == 0 and k_off % lw == 0 and v_off % lw == 0
    return pl.pallas_call(
        functools.partial(_moba_kernel, nb=nb, nbp=nbp, group=group),
        grid=(b, width // lw, seq // tq),
        in_specs=[
            pl.BlockSpec((1, tq, lw), lambda bi, p, i: (bi, i, q_off // lw + p)),
            pl.BlockSpec((1, seq, lw), lambda bi, p, i: (bi, 0, k_off // lw + p)),
            pl.BlockSpec((1, seq, lw), lambda bi, p, i: (bi, 0, v_off // lw + p)),
        ],
        out_specs=pl.BlockSpec((1, tq, lw), lambda bi, p, i: (bi, i, p)),
        out_shape=jax.ShapeDtypeStruct((b, seq, width), BF16),
        scratch_shapes=[pltpu.VMEM((nbp, lw), F32)],
        compiler_params=_params("parallel", "parallel", "arbitrary"),
        name="moba_attention",
    )(proj, proj, proj)


def _swa_kernel(sink_ref, q_ref, kp_ref, kc_ref, vp_ref, vc_ref, o_ref, *, n_pairs):
    n = pl.program_id(1)
    w = WINDOW
    k = jnp.concatenate([kp_ref[0], kc_ref[0]], axis=0).astype(F32)
    v = jnp.concatenate([vp_ref[0], vc_ref[0]], axis=0).astype(F32)
    lo_k = lax.broadcasted_iota(jnp.int32, (2 * w, PAIR), 1) < HEAD_DIM
    k_sw = pltpu.roll(k, HEAD_DIM, axis=1)
    v_sw = pltpu.roll(v, HEAD_DIM, axis=1)
    kk = [jnp.where(lo_k, k, k_sw).astype(BF16), jnp.where(lo_k, k_sw, k).astype(BF16)]
    vv = [jnp.where(lo_k, v, v_sw).astype(BF16), jnp.where(lo_k, v_sw, v).astype(BF16)]
    ones = jnp.ones((2 * w, PAIR), BF16)
    rhs = [jnp.concatenate([vg, ones], axis=1) for vg in vv]

    row = lax.broadcasted_iota(jnp.int32, (w, 2 * w), 0)
    col = lax.broadcasted_iota(jnp.int32, (w, 2 * w), 1)
    delta = row + w - col
    valid = (delta >= 0) & (delta < w) & ((n - 1) * w + col >= 0)
    lo_q = _lane_index((w, PAIR)) < HEAD_DIM
    lo_qb = _lane_index((w, PAIR), BF16) < HEAD_DIM
    pairs_per_kv = SWA_GROUP // 2

    for p in range(n_pairs):
        g = p // pairs_per_kv
        qp = q_ref[0, :, p * PAIR:(p + 1) * PAIR]
        zq = jnp.zeros_like(qp)
        outs = []
        for h in (0, 1):
            qh = jnp.where(lo_qb, qp, zq) if h == 0 else jnp.where(lo_qb, zq, qp)
            s = lax.dot_general(qh, kk[g], NT_DIMS, preferred_element_type=F32)
            s = jnp.where(valid, s, NEG)
            sink = sink_ref[2 * p + h]
            m = jnp.maximum(jnp.max(s, axis=1, keepdims=True), sink)
            pr = jnp.exp(s - m).astype(BF16)
            o2 = jnp.dot(pr, rhs[g], preferred_element_type=F32)
            outs.append(o2[:, :PAIR] / (o2[:, PAIR:] + jnp.exp(sink - m)))
        o_ref[0, :, p * PAIR:(p + 1) * PAIR] = jnp.where(lo_q, outs[0], outs[1]).astype(o_ref.dtype)


def _swa_attention(proj, sinks, lay):
    b, seq, _ = proj.shape
    w = WINDOW
    q_off, width = lay["qc"]
    k_off, v_off = lay["kc"][0], lay["vc"][0]
    assert q_off % width == 0 and seq % w == 0
    return pl.pallas_call(
        functools.partial(_swa_kernel, n_pairs=width // PAIR),
        grid=(b, seq // w),
        in_specs=[
            pl.BlockSpec(memory_space=pltpu.SMEM),
            pl.BlockSpec((1, w, width), lambda bi, n: (bi, n, q_off // width)),
            pl.BlockSpec((1, w, PAIR), lambda bi, n: (bi, jnp.maximum(n - 1, 0), k_off // PAIR)),
            pl.BlockSpec((1, w, PAIR), lambda bi, n: (bi, n, k_off // PAIR)),
            pl.BlockSpec((1, w, PAIR), lambda bi, n: (bi, jnp.maximum(n - 1, 0), v_off // PAIR)),
            pl.BlockSpec((1, w, PAIR), lambda bi, n: (bi, n, v_off // PAIR)),
        ],
        out_specs=pl.BlockSpec((1, w, width), lambda bi, n: (bi, n, 0)),
        out_shape=jax.ShapeDtypeStruct((b, seq, width), BF16),
        compiler_params=_params("parallel", "arbitrary"),
        name="swa_attention",
    )(sinks.astype(F32), proj, proj, proj, proj, proj)


def _out_proj_kernel(ya_ref, yb_ref, yc_ref, ga_ref, gb_ref, gc_ref, w_ref, x_ref, o_ref, mix_ref,
                     *, tn):
    start = 0
    for y_ref, g_ref in ((ya_ref, ga_ref), (yb_ref, gb_ref), (yc_ref, gc_ref)):
        y = y_ref[...].astype(F32)
        ms = jnp.mean(y * y, axis=-1, keepdims=True)
        width = y.shape[1]
        mix_ref[:, start:start + width] = ((y * lax.rsqrt(ms + EPS)) * g_ref[...]).astype(BF16)
        start += width
    for c in range(o_ref.shape[1] // tn):
        cols = slice(c * tn, (c + 1) * tn)
        o_ref[:, cols] = x_ref[:, cols] + jnp.dot(mix_ref[...], w_ref[:, cols],
                                                  preferred_element_type=F32)


def _out_proj(ya, yb, yc, ga, gb, gc, w, layer, xt, tiles):
    n_tok, d = xt.shape
    tm, tn = tiles["tm"], tiles["tn_out"]
    wa, wb, wc = ya.shape[1], yb.shape[1], yc.shape[1]
    mix_w = wa + wb + wc
    assert d % tn == 0 and w.shape[1:] == (mix_w, d)
    return pl.pallas_call(
        functools.partial(_out_proj_kernel, tn=tn),
        grid=(n_tok // tm,),
        in_specs=[
            pl.BlockSpec((tm, wa), lambda i: (i, 0)),
            pl.BlockSpec((tm, wb), lambda i: (i, 0)),
            pl.BlockSpec((tm, wc), lambda i: (i, 0)),
            pl.BlockSpec((1, wa), lambda i: (0, 0)),
            pl.BlockSpec((1, wb), lambda i: (0, 0)),
            pl.BlockSpec((1, wc), lambda i: (0, 0)),
            pl.BlockSpec((None, mix_w, d), lambda i: (layer, 0, 0), pipeline_mode=pl.Buffered(1)),
            pl.BlockSpec((tm, d), lambda i: (i, 0)),
        ],
        out_specs=pl.BlockSpec((tm, d), lambda i: (i, 0)),
        out_shape=jax.ShapeDtypeStruct((n_tok, d), F32),
        scratch_shapes=[pltpu.VMEM((tm, mix_w), BF16)],
        compiler_params=_params("arbitrary"),
        name="out_proj",
    )(ya, yb, yc, ga.reshape(1, wa), gb.reshape(1, wb), gc.reshape(1, wc), w, xt)


def _mlp_kernel(x_ref, g_ref, wu_ref, wd_ref, gf_ref, o_ref, h_ref, a_ref, *, final):
    f = pl.program_id(1)
    last = pl.num_programs(1) - 1

    def activation():
        u = jnp.maximum(jnp.dot(h_ref[...], wu_ref[...], preferred_element_type=F32), 0.0)
        return (u * u).astype(BF16)

    @pl.when(f == 0)
    def _():
        x = x_ref[...]
        ms = jnp.mean(x * x, axis=-1, keepdims=True)
        h_ref[...] = ((x * lax.rsqrt(ms + EPS)) * g_ref[...]).astype(BF16)
        o_ref[...] = x
        a_ref[...] = activation()

    @pl.when((f > 0) & (f < last))
    def _():
        a_prev = a_ref[...]
        a_ref[...] = activation()
        o_ref[...] += jnp.dot(a_prev, wd_ref[...], preferred_element_type=F32)

    @pl.when(f == last)
    def _():
        y = o_ref[...] + jnp.dot(a_ref[...], wd_ref[...], preferred_element_type=F32)
        if final:
            ms = jnp.mean(y * y, axis=-1, keepdims=True)
            y = (y * lax.rsqrt(ms + EPS)) * gf_ref[...]
        o_ref[...] = y


def _mlp(xt, gain, w_up, w_down, layer, final_gain, tiles, final):
    n_tok, d = xt.shape
    d_ff = w_up.shape[2]
    tm, tf = tiles["tm_mlp"], tiles["tf"]
    assert d_ff % tf == 0 and n_tok % tm == 0
    nf = d_ff // tf
    return pl.pallas_call(
        functools.partial(_mlp_kernel, final=final),
        grid=(n_tok // tm, nf + 1),
        in_specs=[
            pl.BlockSpec((tm, d), lambda i, f: (i, 0)),
            pl.BlockSpec((1, d), lambda i, f: (0, 0)),
            pl.BlockSpec((None, d, tf), lambda i, f: (layer, 0, jnp.minimum(f, nf - 1))),
            pl.BlockSpec((None, tf, d), lambda i, f: (layer, jnp.maximum(f - 1, 0), 0)),
            pl.BlockSpec((1, d), lambda i, f: (0, 0)),
        ],
        out_specs=pl.BlockSpec((tm, d), lambda i, f: (i, 0)),
        out_shape=jax.ShapeDtypeStruct((n_tok, d), F32),
        scratch_shapes=[pltpu.VMEM((tm, d), BF16), pltpu.VMEM((tm, tf), BF16)],
        compiler_params=_params("parallel", "arbitrary"),
        name="mlp",
    )(xt, gain.reshape(1, d), w_up, w_down, final_gain.reshape(1, d))


def kernel(x, attn_norm, w_in, sinks, gn_sb, gn_moba, gn_swa, w_out, mlp_norm, w_up, w_down, final_norm):
    b, seq, d = x.shape
    depth = w_in.shape[0]
    n_tok = b * seq
    lay = _layout(d)
    tiles = _tiles(n_tok, seq)
    cos_t, sin_t = _rope_tables(seq)
    xt = x.reshape(n_tok, d)
    w_in, w_out, w_up, w_down = (w.astype(BF16) for w in (w_in, w_out, w_up, w_down))
    for l in range(depth):
        proj = _in_proj(xt, attn_norm[l], w_in, l, cos_t, sin_t, lay, seq, tiles)
        proj = proj.reshape(b, seq, lay["in_width"])
        ya = _sb_attention(proj, lay).reshape(n_tok, -1)
        yb = _moba_attention(proj, lay).reshape(n_tok, -1)
        yc = _swa_attention(proj, sinks[l], lay).reshape(n_tok, -1)
        xt = _out_proj(ya, yb, yc, gn_sb[l], gn_moba[l], gn_swa[l], w_out, l, xt, tiles)
        xt = _mlp(xt, mlp_norm[l], w_up, w_down, l, final_norm, tiles, final=(l == depth - 1))
    return xt.reshape(b, seq, d)
```

```python
import functools

import numpy as np
import jax
import jax.numpy as jnp
from jax import lax
from jax.experimental import pallas as pl
from jax.experimental.pallas import tpu as pltpu

F32 = jnp.float32
BF16 = jnp.bfloat16

HEAD_DIM = 64
PAIR = 2 * HEAD_DIM
ROPE_HALF = HEAD_DIM // 2
MOBA_BLOCK = 256
MOBA_TOPK = 3
WINDOW = 128
SWA_GROUP = 8
ROPE_THETA = 10000.0
EPS = 1e-6
NEG = -1e30
Q_SCALE = HEAD_DIM ** -0.5
MOBA_GROUP = 4
MOBA_PAIRS_PER_STEP = 2
SB_TILE = 256
SB_CHUNK = PAIR
SB_PAIRS_PER_STEP = 2
SB_UNDERFLOW = 104.0

VMEM_LIMIT_BYTES = 56 * 1024 * 1024

NT_DIMS = (((1,), (1,)), ((), ()))


def _params(*semantics):
    return pltpu.CompilerParams(dimension_semantics=semantics,
                                vmem_limit_bytes=VMEM_LIMIT_BYTES)


def _lane_index(shape, dtype=jnp.int32):
    idx = lax.broadcasted_iota(jnp.int32, shape, len(shape) - 1)
    return idx if dtype == jnp.int32 else idx.astype(F32).astype(dtype)


def _layout(d_model):
    sb = d_model // 4
    moba = d_model // 4
    swa_q = d_model // 2
    swa_kv = (swa_q // HEAD_DIM // SWA_GROUP) * HEAD_DIM
    sizes = (sb, sb, sb, moba, moba, moba, swa_q, swa_kv, swa_kv)
    offs = np.concatenate([[0], np.cumsum(sizes)]).astype(int)
    names = ("qa", "ka", "va", "qb", "kb", "vb", "qc", "kc", "vc")
    lay = {n: (int(offs[i]), int(sizes[i])) for i, n in enumerate(names)}
    lay["in_width"] = int(offs[-1])
    assert sb % PAIR == 0 and swa_q % PAIR == 0 and swa_kv == PAIR
    return lay


def _tiles(n_tokens, seq):
    tm = min(512, seq)
    assert seq % tm == 0 and n_tokens % tm == 0
    return dict(tm=tm, tm_mlp=min(1024, n_tokens), tn_in=256, tn_out=512, tf=512)


def _rope_tables(seq):
    inv_freq = ROPE_THETA ** (-jnp.arange(ROPE_HALF, dtype=F32) * 2.0 / HEAD_DIM)
    ang = jnp.arange(seq, dtype=F32)[:, None] * inv_freq[None, :]
    cos, sin = jnp.cos(ang), jnp.sin(ang)
    cos_t = jnp.tile(cos, (1, PAIR // ROPE_HALF))
    sin_t = jnp.tile(jnp.concatenate([-sin, sin], axis=1), (1, PAIR // HEAD_DIM))
    return cos_t, sin_t


def _in_proj_kernel(x_ref, g_ref, w_ref, cos_ref, sin_ref, o_ref, h_ref, *, chunks):
    x = x_ref[...]
    ms = jnp.mean(x * x, axis=-1, keepdims=True)
    h_ref[...] = ((x * lax.rsqrt(ms + EPS)) * g_ref[...]).astype(BF16)
    tm = x.shape[0]
    lane = lax.broadcasted_iota(jnp.int32, (tm, PAIR), 1)
    first = (lane % HEAD_DIM) < ROPE_HALF
    for start, classes in chunks:
        width = len(classes) * PAIR
        acc = jnp.dot(h_ref[...], w_ref[:, start:start + width], preferred_element_type=F32)
        for t, (rope, scale) in enumerate(classes):
            a = acc[:, t * PAIR:(t + 1) * PAIR]
            if rope:
                partner = jnp.where(first, pltpu.roll(a, PAIR - ROPE_HALF, axis=1),
                                    pltpu.roll(a, ROPE_HALF, axis=1))
                a = a * cos_ref[...] + partner * sin_ref[...]
            if scale != 1.0:
                a = a * scale
            o_ref[:, start + t * PAIR:start + (t + 1) * PAIR] = a.astype(o_ref.dtype)


def _in_proj(xt, gain, w, layer, cos_t, sin_t, lay, seq, tiles):
    n_tok, d = xt.shape
    in_w = lay["in_width"]
    tm, tn = tiles["tm"], tiles["tn_in"]
    assert in_w % tn == 0 and tn % PAIR == 0
    rope = np.zeros(in_w // PAIR, bool)
    scale = np.ones(in_w // PAIR, np.float32)
    for name in ("qb", "kb", "qc", "kc"):
        o, s = lay[name]
        rope[o // PAIR:(o + s) // PAIR] = True
    for name in ("qa", "qb", "qc"):
        o, s = lay[name]
        scale[o // PAIR:(o + s) // PAIR] = Q_SCALE
    per = tn // PAIR
    chunks = tuple((c * tn, tuple((bool(rope[c * per + t]), float(scale[c * per + t])) for t in range(per)))
                   for c in range(in_w // tn))
    pos_blocks = seq // tm
    return pl.pallas_call(
        functools.partial(_in_proj_kernel, chunks=chunks),
        grid=(n_tok // tm,),
        in_specs=[
            pl.BlockSpec((tm, d), lambda i: (i, 0)),
            pl.BlockSpec((1, d), lambda i: (0, 0)),
            pl.BlockSpec((None, d, in_w), lambda i: (layer, 0, 0), pipeline_mode=pl.Buffered(1)),
            pl.BlockSpec((tm, PAIR), lambda i: (i % pos_blocks, 0)),
            pl.BlockSpec((tm, PAIR), lambda i: (i % pos_blocks, 0)),
        ],
        out_specs=pl.BlockSpec((tm, in_w), lambda i: (i, 0)),
        out_shape=jax.ShapeDtypeStruct((n_tok, in_w), BF16),
        scratch_shapes=[pltpu.VMEM((tm, d), BF16)],
        compiler_params=_params("arbitrary"),
        name="in_proj",
    )(xt, gain.reshape(1, d), w, cos_t, sin_t)


def _sb_kernel(q_ref, k_ref, v_ref, u_ref, o_ref, carry_ref, acc_ref, *, tq):
    i = pl.program_id(2)
    ch = SB_CHUNK
    n_pairs = q_ref.shape[2] // PAIR

    def lanes(p):
        return slice(p * PAIR, (p + 1) * PAIR)

    lo = _lane_index((tq, PAIR), BF16) < HEAD_DIM
    qs = []
    for p in range(n_pairs):
        q = q_ref[0, :, lanes(p)]
        zq = jnp.zeros_like(q)
        qs.append(jnp.concatenate([jnp.where(lo, q, zq), jnp.where(lo, zq, q)], axis=0))
    u = u_ref[...]
    row = lax.broadcasted_iota(jnp.int32, (2 * tq, tq), 0) % tq
    col = lax.broadcasted_iota(jnp.int32, (2 * tq, tq), 1)
    past = col < row

    def tiles(off, n_tiles, carries, accs, diagonal):
        nk = n_tiles * tq
        kblk = k_ref[0, pl.ds(off, nk), :]
        vblk = v_ref[0, pl.ds(off, nk), :]
        zs = [lax.dot_general(qs[p], kblk[:, lanes(p)], NT_DIMS, preferred_element_type=F32)
              for p in range(n_pairs)]
        sps = [jnp.maximum(z, 0.0) + jnp.log(1.0 + jnp.exp(-jnp.abs(z))) for z in zs]
        ws = [[None] * (nk // ch) for _ in range(n_pairs)]
        carries = list(carries)
        for c in reversed(range(nk // ch)):
            sl = slice(c * ch, (c + 1) * ch)
            masked = diagonal and c * ch >= nk - tq
            mask = past[:, c * ch - (nk - tq):(c + 1) * ch - (nk - tq)] if masked else None
            for p in range(n_pairs):
                s_c = jnp.where(mask, sps[p][:, sl], 0.0) if masked else sps[p][:, sl]
                hi = s_c.astype(BF16)
                lo_part = (s_c - hi.astype(F32)).astype(BF16)
                r = jnp.dot(jnp.concatenate([hi, lo_part], axis=1), u, preferred_element_type=F32)
                w = jnp.exp(zs[p][:, sl] - sps[p][:, sl] - r[:, :ch] - carries[p])
                if masked:
                    w = jnp.where(mask, w, 0.0)
                ws[p][c] = w.astype(BF16)
                carries[p] = carries[p] + r[:, ch:]
        lo_v = _lane_index((nk, PAIR), BF16) < HEAD_DIM
        accs = list(accs)
        for p in range(n_pairs):
            wb = jnp.concatenate(ws[p], axis=1)
            wcat = jnp.concatenate([wb[:tq], wb[tq:]], axis=1)
            vp = vblk[:, lanes(p)]
            zv = jnp.zeros_like(vp)
            vcat = jnp.concatenate([jnp.where(lo_v, vp, zv), jnp.where(lo_v, zv, vp)], axis=0)
            accs[p] = accs[p] + jnp.dot(wcat, vcat, preferred_element_type=F32)
        return carries, accs

    zero_carry = [jnp.zeros((2 * tq, PAIR), F32)] * n_pairs
    zero_acc = [jnp.zeros((tq, PAIR), F32)] * n_pairs

    def first_pass(off, n_tiles):
        carries, accs = tiles(off, n_tiles, zero_carry, zero_acc, True)
        for p in range(n_pairs):
            carry_ref[p] = carries[p]
            acc_ref[p] = accs[p]

    @pl.when(i == 0)
    def _():
        first_pass(0, 1)

    @pl.when(i > 0)
    def _():
        first_pass(pl.multiple_of((i - 1) * tq, tq), 2)

    def unfinished(carries):
        return functools.reduce(jnp.minimum, [jnp.min(c) for c in carries]) < SB_UNDERFLOW

    def cond(state):
        return (state[0] < i - 1) & state[1]

    def body(state):
        t = state[0]
        off = pl.multiple_of((i - 2 - t) * tq, tq)
        carries, accs = tiles(off, 1, state[2:2 + n_pairs], state[2 + n_pairs:], False)
        return (t + 1, unfinished(carries), *carries, *accs)

    carries = [carry_ref[p] for p in range(n_pairs)]
    accs = [acc_ref[p] for p in range(n_pairs)]
    state = lax.while_loop(cond, body, (jnp.int32(0), unfinished(carries), *carries, *accs))
    for p in range(n_pairs):
        o_ref[0, :, lanes(p)] = state[2 + n_pairs + p].astype(o_ref.dtype)


def _sb_attention(proj, lay):
    b, seq, _ = proj.shape
    tq = min(SB_TILE, seq)
    ch = SB_CHUNK
    assert seq % tq == 0 and tq % ch == 0
    q_off, width = lay["qa"]
    k_off, v_off = lay["ka"][0], lay["va"][0]
    n_pairs = SB_PAIRS_PER_STEP
    lw = n_pairs * PAIR
    assert width % lw == 0 and q_off % lw == 0 and k_off % lw == 0 and v_off % lw == 0
    tri = np.tril(np.ones((ch, ch), np.float32), -1)
    uu = np.concatenate([tri, np.ones((ch, PAIR), np.float32)], axis=1)
    uu = jnp.asarray(np.concatenate([uu, uu], axis=0), dtype=BF16)
    return pl.pallas_call(
        functools.partial(_sb_kernel, tq=tq),
        grid=(b, width // lw, seq // tq),
        in_specs=[
            pl.BlockSpec((1, tq, lw), lambda bi, p, i: (bi, i, q_off // lw + p)),
            pl.BlockSpec((1, seq, lw), lambda bi, p, i: (bi, 0, k_off // lw + p)),
            pl.BlockSpec((1, seq, lw), lambda bi, p, i: (bi, 0, v_off // lw + p)),
            pl.BlockSpec((2 * ch, ch + PAIR), lambda bi, p, i: (0, 0)),
        ],
        out_specs=pl.BlockSpec((1, tq, lw), lambda bi, p, i: (bi, i, p)),
        out_shape=jax.ShapeDtypeStruct((b, seq, width), BF16),
        scratch_shapes=[pltpu.VMEM((n_pairs, 2 * tq, PAIR), F32), pltpu.VMEM((n_pairs, tq, PAIR), F32)],
        compiler_params=_params("parallel", "parallel", "arbitrary"),
        name="sb_attention",
    )(proj, proj, proj, uu)


def _moba_kernel(q_ref, k_ref, v_ref, o_ref, kmean_ref, *, nb, nbp, group):
    i = pl.program_id(2)
    blk = MOBA_BLOCK
    tq = blk
    n_pairs = q_ref.shape[2] // PAIR

    def lanes(p):
        return slice(p * PAIR, (p + 1) * PAIR)

    @pl.when(i == 0)
    def _():
        kf = k_ref[0].astype(F32).reshape(nb, blk, n_pairs * PAIR)
        km = jnp.sum(kf, axis=1) * (1.0 / blk)
        if nbp > nb:
            km = jnp.concatenate([km, jnp.zeros((nbp - nb, n_pairs * PAIR), F32)], axis=0)
        kmean_ref[...] = km

    lo_qb = _lane_index((tq, PAIR), BF16) < HEAD_DIM
    lo_m = _lane_index((nbp, PAIR)) < HEAD_DIM
    jidx = lax.broadcasted_iota(jnp.int32, (nbp, tq), 0)
    valid = jidx < i

    def gated_queries(q, km):
        zq = jnp.zeros_like(q)
        plain, aug = [], []
        for h in (0, 1):
            head_m = lo_m if h == 0 else jnp.logical_not(lo_m)
            head_q = lo_qb if h == 0 else jnp.logical_not(lo_qb)
            kmh = jnp.where(head_m, km, 0.0)
            a = kmh.astype(BF16)
            r1 = kmh - a.astype(F32)
            b2 = r1.astype(BF16)
            c3 = (r1 - b2.astype(F32)).astype(BF16)
            g3 = lax.dot_general(jnp.concatenate([a, b2, c3], axis=0), q, NT_DIMS,
                                 preferred_element_type=F32)
            gate = g3[:nbp] + g3[nbp:2 * nbp] + g3[2 * nbp:]
            gate = jnp.where(valid, gate, -jnp.inf)
            beaten_by = jnp.zeros((nbp, tq), jnp.int32)
            for jp in range(nb):
                other = gate[jp:jp + 1, :]
                beats = (other > gate) | ((other == gate) & (jidx > jp))
                beaten_by = beaten_by + beats.astype(jnp.int32)
            sel = valid & (beaten_by < MOBA_TOPK)
            bias_t = jnp.where(sel, 0.0, NEG)
            top = HEAD_DIM if h == 0 else 0
            pieces = [jnp.zeros((top, tq), F32)] if top else []
            pieces.append(bias_t)
            if PAIR - top - nbp:
                pieces.append(jnp.zeros((PAIR - top - nbp, tq), F32))
            placed = jnp.concatenate(pieces, axis=0).T
            plain.append(jnp.where(head_q, q, zq))
            aug.append(jnp.where(head_q, q, placed.astype(BF16)))
        return plain, aug

    q_plain, q_aug = [], []
    for p in range(n_pairs):
        plain, aug = gated_queries(q_ref[0, :, lanes(p)], kmean_ref[:, lanes(p)])
        q_plain.append(plain)
        q_aug.append(aug)

    lo_q = _lane_index((tq, PAIR)) < HEAD_DIM
    lo_q2 = jnp.concatenate([lo_q, lo_q], axis=1)

    def attend(s0, s1, vblk, m0, m1, acc):
        n0 = jnp.maximum(m0, jnp.max(s0, axis=1, keepdims=True))
        n1 = jnp.maximum(m1, jnp.max(s1, axis=1, keepdims=True))
        p0 = jnp.exp(s0 - n0).astype(BF16)
        p1 = jnp.exp(s1 - n1).astype(BF16)
        alpha = jnp.where(lo_q, jnp.exp(m0 - n0), jnp.exp(m1 - n1))
        rhs = jnp.concatenate([vblk, jnp.ones_like(vblk)], axis=1)
        u0 = jnp.dot(p0, rhs, preferred_element_type=F32)
        u1 = jnp.dot(p1, rhs, preferred_element_type=F32)
        acc = acc * jnp.concatenate([alpha, alpha], axis=1) + jnp.where(lo_q2, u0, u1)
        return n0, n1, acc

    own_off = pl.multiple_of(i * blk, blk)
    k_own = k_ref[0, pl.ds(own_off, blk), :]
    v_own = v_ref[0, pl.ds(own_off, blk), :]
    row = lax.broadcasted_iota(jnp.int32, (tq, blk), 0)
    col = lax.broadcasted_iota(jnp.int32, (tq, blk), 1)
    causal = col <= row
    m_init = jnp.full((tq, 1), NEG, F32)
    state = []
    for p in range(n_pairs):
        s_own = [jnp.where(causal, lax.dot_general(qh, k_own[:, lanes(p)], NT_DIMS,
                                                   preferred_element_type=F32), NEG)
                 for qh in q_plain[p]]
        state += attend(s_own[0], s_own[1], v_own[:, lanes(p)],
                        m_init, m_init, jnp.zeros((tq, 2 * PAIR), F32))

    gk = group * blk
    lane_g = _lane_index((gk, PAIR), BF16)
    lo_g = lane_g < HEAD_DIM
    blk_in_group = (lax.broadcasted_iota(jnp.int32, (gk, PAIR), 0) // blk).astype(F32).astype(BF16)
    one = jnp.ones((gk, PAIR), BF16)
    zk = jnp.zeros((gk, PAIR), BF16)

    def body(g, st):
        o = pl.multiple_of(g * gk, gk)
        kg = k_ref[0, pl.ds(o, gk), :]
        vg = v_ref[0, pl.ds(o, gk), :]
        first = jnp.full((1, PAIR), g * group, jnp.int32).astype(F32).astype(BF16)
        blk_id = blk_in_group + first
        ind0 = jnp.where(lane_g == blk_id + HEAD_DIM, one, zk)
        ind1 = jnp.where(lane_g == blk_id, one, zk)
        scores = []
        for p in range(n_pairs):
            kp = kg[:, lanes(p)]
            scores.append((
                lax.dot_general(q_aug[p][0], jnp.where(lo_g, kp, ind0), NT_DIMS, preferred_element_type=F32),
                lax.dot_general(q_aug[p][1], jnp.where(lo_g, ind1, kp), NT_DIMS, preferred_element_type=F32)))
        new = []
        for p in range(n_pairs):
            new += attend(scores[p][0], scores[p][1], vg[:, lanes(p)], *st[3 * p:3 * p + 3])
        return tuple(new)

    state = lax.fori_loop(0, (i + group - 1) // group, body, tuple(state))
    for p in range(n_pairs):
        acc = state[3 * p + 2]
        o_ref[0, :, lanes(p)] = (acc[:, :PAIR] / acc[:, PAIR:]).astype(o_ref.dtype)


def _moba_attention(proj, lay):
    b, seq, _ = proj.shape
    tq = MOBA_BLOCK
    assert seq % MOBA_BLOCK == 0
    nb = seq // MOBA_BLOCK
    nbp = -(-nb // 8) * 8
    group = min(MOBA_GROUP, nb)
    assert nbp <= HEAD_DIM
    assert nb % group == 0
    q_off, width = lay["qb"]
    k_off, v_off = lay["kb"][0], lay["vb"][0]
    lw = MOBA_PAIRS_PER_STEP * PAIR
    assert width % lw == 0 and q_off % lw == 0 and k_off % lw == 0 and v_off % lw == 0
    return pl.pallas_call(
        functools.partial(_moba_kernel, nb=nb, nbp=nbp, group=group),
        grid=(b, width // lw, seq // tq),
        in_specs=[
            pl.BlockSpec((1, tq, lw), lambda bi, p, i: (bi, i, q_off // lw + p)),
            pl.BlockSpec((1, seq, lw), lambda bi, p, i: (bi, 0, k_off // lw + p)),
            pl.BlockSpec((1, seq, lw), lambda bi, p, i: (bi, 0, v_off // lw + p)),
        ],
        out_specs=pl.BlockSpec((1, tq, lw), lambda bi, p, i: (bi, i, p)),
        out_shape=jax.ShapeDtypeStruct((b, seq, width), BF16),
        scratch_shapes=[pltpu.VMEM((nbp, lw), F32)],
        compiler_params=_params("parallel", "parallel", "arbitrary"),
        name="moba_attention",
    )(proj, proj, proj)


def _swa_kernel(sink_ref, q_ref, kp_ref, kc_ref, vp_ref, vc_ref, o_ref, *, n_pairs):
    n = pl.program_id(1)
    w = WINDOW
    k = jnp.concatenate([kp_ref[0], kc_ref[0]], axis=0).astype(F32)
    v = jnp.concatenate([vp_ref[0], vc_ref[0]], axis=0).astype(F32)
    lo_k = lax.broadcasted_iota(jnp.int32, (2 * w, PAIR), 1) < HEAD_DIM
    k_sw = pltpu.roll(k, HEAD_DIM, axis=1)
    v_sw = pltpu.roll(v, HEAD_DIM, axis=1)
    kk = [jnp.where(lo_k, k, k_sw).astype(BF16), jnp.where(lo_k, k_sw, k).astype(BF16)]
    vv = [jnp.where(lo_k, v, v_sw).astype(BF16), jnp.where(lo_k, v_sw, v).astype(BF16)]
    ones = jnp.ones((2 * w, PAIR), BF16)
    rhs = [jnp.concatenate([vg, ones], axis=1) for vg in vv]

    row = lax.broadcasted_iota(jnp.int32, (w, 2 * w), 0)
    col = lax.broadcasted_iota(jnp.int32, (w, 2 * w), 1)
    delta = row + w - col
    valid = (delta >= 0) & (delta < w) & ((n - 1) * w + col >= 0)
    lo_q = _lane_index((w, PAIR)) < HEAD_DIM
    lo_qb = _lane_index((w, PAIR), BF16) < HEAD_DIM
    pairs_per_kv = SWA_GROUP // 2

    for p in range(n_pairs):
        g = p // pairs_per_kv
        qp = q_ref[0, :, p * PAIR:(p + 1) * PAIR]
        zq = jnp.zeros_like(qp)
        outs = []
        for h in (0, 1):
            qh = jnp.where(lo_qb, qp, zq) if h == 0 else jnp.where(lo_qb, zq, qp)
            s = lax.dot_general(qh, kk[g], NT_DIMS, preferred_element_type=F32)
            s = jnp.where(valid, s, NEG)
            sink = sink_ref[2 * p + h]
            m = jnp.maximum(jnp.max(s, axis=1, keepdims=True), sink)
            pr = jnp.exp(s - m).astype(BF16)
            o2 = jnp.dot(pr, rhs[g], preferred_element_type=F32)
            outs.append(o2[:, :PAIR] / (o2[:, PAIR:] + jnp.exp(sink - m)))
        o_ref[0, :, p * PAIR:(p + 1) * PAIR] = jnp.where(lo_q, outs[0], outs[1]).astype(o_ref.dtype)


def _swa_attention(proj, sinks, lay):
    b, seq, _ = proj.shape
    w = WINDOW
    q_off, width = lay["qc"]
    k_off, v_off = lay["kc"][0], lay["vc"][0]
    assert q_off % width == 0 and seq % w == 0
    return pl.pallas_call(
        functools.partial(_swa_kernel, n_pairs=width // PAIR),
        grid=(b, seq // w),
        in_specs=[
            pl.BlockSpec(memory_space=pltpu.SMEM),
            pl.BlockSpec((1, w, width), lambda bi, n: (bi, n, q_off // width)),
            pl.BlockSpec((1, w, PAIR), lambda bi, n: (bi, jnp.maximum(n - 1, 0), k_off // PAIR)),
            pl.BlockSpec((1, w, PAIR), lambda bi, n: (bi, n, k_off // PAIR)),
            pl.BlockSpec((1, w, PAIR), lambda bi, n: (bi, jnp.maximum(n - 1, 0), v_off // PAIR)),
            pl.BlockSpec((1, w, PAIR), lambda bi, n: (bi, n, v_off // PAIR)),
        ],
        out_specs=pl.BlockSpec((1, w, width), lambda bi, n: (bi, n, 0)),
        out_shape=jax.ShapeDtypeStruct((b, seq, width), BF16),
        compiler_params=_params("parallel", "arbitrary"),
        name="swa_attention",
    )(sinks.astype(F32), proj, proj, proj, proj, proj)


def _out_proj_kernel(ya_ref, yb_ref, yc_ref, ga_ref, gb_ref, gc_ref, w_ref, x_ref, o_ref, mix_ref,
                     *, tn):
    start = 0
    for y_ref, g_ref in ((ya_ref, ga_ref), (yb_ref, gb_ref), (yc_ref, gc_ref)):
        y = y_ref[...].astype(F32)
        ms = jnp.mean(y * y, axis=-1, keepdims=True)
        width = y.shape[1]
        mix_ref[:, start:start + width] = ((y * lax.rsqrt(ms + EPS)) * g_ref[...]).astype(BF16)
        start += width
    for c in range(o_ref.shape[1] // tn):
        cols = slice(c * tn, (c + 1) * tn)
        o_ref[:, cols] = x_ref[:, cols] + jnp.dot(mix_ref[...], w_ref[:, cols],
                                                  preferred_element_type=F32)


def _out_proj(ya, yb, yc, ga, gb, gc, w, layer, xt, tiles):
    n_tok, d = xt.shape
    tm, tn = tiles["tm"], tiles["tn_out"]
    wa, wb, wc = ya.shape[1], yb.shape[1], yc.shape[1]
    mix_w = wa + wb + wc
    assert d % tn == 0 and w.shape[1:] == (mix_w, d)
    return pl.pallas_call(
        functools.partial(_out_proj_kernel, tn=tn),
        grid=(n_tok // tm,),
        in_specs=[
            pl.BlockSpec((tm, wa), lambda i: (i, 0)),
            pl.BlockSpec((tm, wb), lambda i: (i, 0)),
            pl.BlockSpec((tm, wc), lambda i: (i, 0)),
            pl.BlockSpec((1, wa), lambda i: (0, 0)),
            pl.BlockSpec((1, wb), lambda i: (0, 0)),
            pl.BlockSpec((1, wc), lambda i: (0, 0)),
            pl.BlockSpec((None, mix_w, d), lambda i: (layer, 0, 0), pipeline_mode=pl.Buffered(1)),
            pl.BlockSpec((tm, d), lambda i: (i, 0)),
        ],
        out_specs=pl.BlockSpec((tm, d), lambda i: (i, 0)),
        out_shape=jax.ShapeDtypeStruct((n_tok, d), F32),
        scratch_shapes=[pltpu.VMEM((tm, mix_w), BF16)],
        compiler_params=_params("arbitrary"),
        name="out_proj",
    )(ya, yb, yc, ga.reshape(1, wa), gb.reshape(1, wb), gc.reshape(1, wc), w, xt)


def _mlp_kernel(x_ref, g_ref, wu_ref, wd_ref, gf_ref, o_ref, h_ref, *, final):
    f = pl.program_id(1)

    @pl.when(f == 0)
    def _():
        x = x_ref[...]
        ms = jnp.mean(x * x, axis=-1, keepdims=True)
        h_ref[...] = ((x * lax.rsqrt(ms + EPS)) * g_ref[...]).astype(BF16)
        o_ref[...] = x

    u = jnp.maximum(jnp.dot(h_ref[...], wu_ref[...], preferred_element_type=F32), 0.0)
    o_ref[...] += jnp.dot((u * u).astype(BF16), wd_ref[...], preferred_element_type=F32)

    if final:
        @pl.when(f == pl.num_programs(1) - 1)
        def _():
            y = o_ref[...]
            ms = jnp.mean(y * y, axis=-1, keepdims=True)
            o_ref[...] = (y * lax.rsqrt(ms + EPS)) * gf_ref[...]


def _mlp(xt, gain, w_up, w_down, layer, final_gain, tiles, final):
    n_tok, d = xt.shape
    d_ff = w_up.shape[2]
    tm, tf = tiles["tm_mlp"], tiles["tf"]
    assert d_ff % tf == 0 and n_tok % tm == 0
    return pl.pallas_call(
        functools.partial(_mlp_kernel, final=final),
        grid=(n_tok // tm, d_ff // tf),
        in_specs=[
            pl.BlockSpec((tm, d), lambda i, f: (i, 0)),
            pl.BlockSpec((1, d), lambda i, f: (0, 0)),
            pl.BlockSpec((None, d, tf), lambda i, f: (layer, 0, f)),
            pl.BlockSpec((None, tf, d), lambda i, f: (layer, f, 0)),
            pl.BlockSpec((1, d), lambda i, f: (0, 0)),
        ],
        out_specs=pl.BlockSpec((tm, d), lambda i, f: (i, 0)),
        out_shape=jax.ShapeDtypeStruct((n_tok, d), F32),
        scratch_shapes=[pltpu.VMEM((tm, d), BF16)],
        compiler_params=_params("parallel", "arbitrary"),
        name="mlp",
    )(xt, gain.reshape(1, d), w_up, w_down, final_gain.reshape(1, d))


def kernel(x, attn_norm, w_in, sinks, gn_sb, gn_moba, gn_swa, w_out, mlp_norm, w_up, w_down, final_norm):
    b, seq, d = x.shape
    depth = w_in.shape[0]
    n_tok = b * seq
    lay = _layout(d)
    tiles = _tiles(n_tok, seq)
    cos_t, sin_t = _rope_tables(seq)
    xt = x.reshape(n_tok, d)
    w_in, w_out, w_up, w_down = (w.astype(BF16) for w in (w_in, w_out, w_up, w_down))
    for l in range(depth):
        proj = _in_proj(xt, attn_norm[l], w_in, l, cos_t, sin_t, lay, seq, tiles)
        proj = proj.reshape(b, seq, lay["in_width"])
        ya = _sb_attention(proj, lay).reshape(n_tok, -1)
        yb = _moba_attention(proj, lay).reshape(n_tok, -1)
        yc = _swa_attention(proj, sinks[l], lay).reshape(n_tok, -1)
        xt = _out_proj(ya, yb, yc, gn_sb[l], gn_moba[l], gn_swa[l], w_out, l, xt, tiles)
        xt = _mlp(xt, mlp_norm[l], w_up, w_down, l, final_norm, tiles, final=(l == depth - 1))
    return xt.reshape(b, seq, d)
```

```python
import functools

import numpy as np
import jax
import jax.numpy as jnp
from jax import lax
from jax.experimental import pallas as pl
from jax.experimental.pallas import tpu as pltpu

F32 = jnp.float32
BF16 = jnp.bfloat16

HEAD_DIM = 64
PAIR = 2 * HEAD_DIM
ROPE_HALF = HEAD_DIM // 2
MOBA_BLOCK = 256
MOBA_TOPK = 3
WINDOW = 128
SWA_GROUP = 8
ROPE_THETA = 10000.0
EPS = 1e-6
NEG = -1e30
Q_SCALE = HEAD_DIM ** -0.5
MOBA_GROUP = 4
MOBA_PAIRS_PER_STEP = 2
SB_TILE = 256
SB_CHUNK = PAIR
SB_PAIRS_PER_STEP = 2
SB_UNDERFLOW = 104.0

VMEM_LIMIT_BYTES = 56 * 1024 * 1024

NT_DIMS = (((1,), (1,)), ((), ()))


def _params(*semantics):
    return pltpu.CompilerParams(dimension_semantics=semantics,
                                vmem_limit_bytes=VMEM_LIMIT_BYTES)


def _lane_index(shape, dtype=jnp.int32):
    idx = lax.broadcasted_iota(jnp.int32, shape, len(shape) - 1)
    return idx if dtype == jnp.int32 else idx.astype(F32).astype(dtype)


def _layout(d_model):
    sb = d_model // 4
    moba = d_model // 4
    swa_q = d_model // 2
    swa_kv = (swa_q // HEAD_DIM // SWA_GROUP) * HEAD_DIM
    sizes = (sb, sb, sb, moba, moba, moba, swa_q, swa_kv, swa_kv)
    offs = np.concatenate([[0], np.cumsum(sizes)]).astype(int)
    names = ("qa", "ka", "va", "qb", "kb", "vb", "qc", "kc", "vc")
    lay = {n: (int(offs[i]), int(sizes[i])) for i, n in enumerate(names)}
    lay["in_width"] = int(offs[-1])
    assert sb % PAIR == 0 and swa_q % PAIR == 0 and swa_kv == PAIR
    return lay


def _tiles(n_tokens, seq):
    tm = min(512, seq)
    assert seq % tm == 0 and n_tokens % tm == 0
    return dict(tm=tm, tm_mlp=min(1024, n_tokens), tn_in=256, tn_out=512, tf=1024)


def _rope_tables(seq):
    inv_freq = ROPE_THETA ** (-jnp.arange(ROPE_HALF, dtype=F32) * 2.0 / HEAD_DIM)
    ang = jnp.arange(seq, dtype=F32)[:, None] * inv_freq[None, :]
    cos, sin = jnp.cos(ang), jnp.sin(ang)
    cos_t = jnp.tile(cos, (1, PAIR // ROPE_HALF))
    sin_t = jnp.tile(jnp.concatenate([-sin, sin], axis=1), (1, PAIR // HEAD_DIM))
    return cos_t, sin_t


def _in_proj_kernel(x_ref, g_ref, w_ref, cos_ref, sin_ref, o_ref, h_ref, *, chunks):
    x = x_ref[...]
    ms = jnp.mean(x * x, axis=-1, keepdims=True)
    h_ref[...] = ((x * lax.rsqrt(ms + EPS)) * g_ref[...]).astype(BF16)
    tm = x.shape[0]
    lane = lax.broadcasted_iota(jnp.int32, (tm, PAIR), 1)
    first = (lane % HEAD_DIM) < ROPE_HALF
    for start, classes in chunks:
        width = len(classes) * PAIR
        acc = jnp.dot(h_ref[...], w_ref[:, start:start + width], preferred_element_type=F32)
        for t, (rope, scale) in enumerate(classes):
            a = acc[:, t * PAIR:(t + 1) * PAIR]
            if rope:
                partner = jnp.where(first, pltpu.roll(a, PAIR - ROPE_HALF, axis=1),
                                    pltpu.roll(a, ROPE_HALF, axis=1))
                a = a * cos_ref[...] + partner * sin_ref[...]
            if scale != 1.0:
                a = a * scale
            o_ref[:, start + t * PAIR:start + (t + 1) * PAIR] = a.astype(o_ref.dtype)


def _in_proj(xt, gain, w, layer, cos_t, sin_t, lay, seq, tiles):
    n_tok, d = xt.shape
    in_w = lay["in_width"]
    tm, tn = tiles["tm"], tiles["tn_in"]
    assert in_w % tn == 0 and tn % PAIR == 0
    rope = np.zeros(in_w // PAIR, bool)
    scale = np.ones(in_w // PAIR, np.float32)
    for name in ("qb", "kb", "qc", "kc"):
        o, s = lay[name]
        rope[o // PAIR:(o + s) // PAIR] = True
    for name in ("qa", "qb", "qc"):
        o, s = lay[name]
        scale[o // PAIR:(o + s) // PAIR] = Q_SCALE
    per = tn // PAIR
    chunks = tuple((c * tn, tuple((bool(rope[c * per + t]), float(scale[c * per + t])) for t in range(per)))
                   for c in range(in_w // tn))
    pos_blocks = seq // tm
    return pl.pallas_call(
        functools.partial(_in_proj_kernel, chunks=chunks),
        grid=(n_tok // tm,),
        in_specs=[
            pl.BlockSpec((tm, d), lambda i: (i, 0)),
            pl.BlockSpec((1, d), lambda i: (0, 0)),
            pl.BlockSpec((None, d, in_w), lambda i: (layer, 0, 0), pipeline_mode=pl.Buffered(1)),
            pl.BlockSpec((tm, PAIR), lambda i: (i % pos_blocks, 0)),
            pl.BlockSpec((tm, PAIR), lambda i: (i % pos_blocks, 0)),
        ],
        out_specs=pl.BlockSpec((tm, in_w), lambda i: (i, 0)),
        out_shape=jax.ShapeDtypeStruct((n_tok, in_w), BF16),
        scratch_shapes=[pltpu.VMEM((tm, d), BF16)],
        compiler_params=_params("arbitrary"),
        name="in_proj",
    )(xt, gain.reshape(1, d), w, cos_t, sin_t)


def _sb_kernel(q_ref, k_ref, v_ref, u_ref, o_ref, carry_ref, acc_ref, *, tq):
    i = pl.program_id(2)
    ch = SB_CHUNK
    n_pairs = q_ref.shape[2] // PAIR

    def lanes(p):
        return slice(p * PAIR, (p + 1) * PAIR)

    lo = _lane_index((tq, PAIR), BF16) < HEAD_DIM
    qs = []
    for p in range(n_pairs):
        q = q_ref[0, :, lanes(p)]
        zq = jnp.zeros_like(q)
        qs.append(jnp.concatenate([jnp.where(lo, q, zq), jnp.where(lo, zq, q)], axis=0))
    u = u_ref[...]
    row = lax.broadcasted_iota(jnp.int32, (2 * tq, tq), 0) % tq
    col = lax.broadcasted_iota(jnp.int32, (2 * tq, tq), 1)
    past = col < row

    def tiles(off, n_tiles, carries, accs, diagonal):
        nk = n_tiles * tq
        kblk = k_ref[0, pl.ds(off, nk), :]
        vblk = v_ref[0, pl.ds(off, nk), :]
        zs = [lax.dot_general(qs[p], kblk[:, lanes(p)], NT_DIMS, preferred_element_type=F32)
              for p in range(n_pairs)]
        sps = [jnp.maximum(z, 0.0) + jnp.log(1.0 + jnp.exp(-jnp.abs(z))) for z in zs]
        ws = [[None] * (nk // ch) for _ in range(n_pairs)]
        carries = list(carries)
        for c in reversed(range(nk // ch)):
            sl = slice(c * ch, (c + 1) * ch)
            masked = diagonal and c * ch >= nk - tq
            mask = past[:, c * ch - (nk - tq):(c + 1) * ch - (nk - tq)] if masked else None
            for p in range(n_pairs):
                s_c = jnp.where(mask, sps[p][:, sl], 0.0) if masked else sps[p][:, sl]
                hi = s_c.astype(BF16)
                lo_part = (s_c - hi.astype(F32)).astype(BF16)
                r = jnp.dot(jnp.concatenate([hi, lo_part], axis=1), u, preferred_element_type=F32)
                w = jnp.exp(zs[p][:, sl] - sps[p][:, sl] - r[:, :ch] - carries[p])
                if masked:
                    w = jnp.where(mask, w, 0.0)
                ws[p][c] = w.astype(BF16)
                carries[p] = carries[p] + r[:, ch:]
        lo_v = _lane_index((nk, PAIR), BF16) < HEAD_DIM
        accs = list(accs)
        for p in range(n_pairs):
            wb = jnp.concatenate(ws[p], axis=1)
            wcat = jnp.concatenate([wb[:tq], wb[tq:]], axis=1)
            vp = vblk[:, lanes(p)]
            zv = jnp.zeros_like(vp)
            vcat = jnp.concatenate([jnp.where(lo_v, vp, zv), jnp.where(lo_v, zv, vp)], axis=0)
            accs[p] = accs[p] + jnp.dot(wcat, vcat, preferred_element_type=F32)
        return carries, accs

    zero_carry = [jnp.zeros((2 * tq, PAIR), F32)] * n_pairs
    zero_acc = [jnp.zeros((tq, PAIR), F32)] * n_pairs

    def first_pass(off, n_tiles):
        carries, accs = tiles(off, n_tiles, zero_carry, zero_acc, True)
        for p in range(n_pairs):
            carry_ref[p] = carries[p]
            acc_ref[p] = accs[p]

    @pl.when(i == 0)
    def _():
        first_pass(0, 1)

    @pl.when(i > 0)
    def _():
        first_pass(pl.multiple_of((i - 1) * tq, tq), 2)

    def unfinished(carries):
        return functools.reduce(jnp.minimum, [jnp.min(c) for c in carries]) < SB_UNDERFLOW

    def cond(state):
        return (state[0] < i - 1) & state[1]

    def body(state):
        t = state[0]
        off = pl.multiple_of((i - 2 - t) * tq, tq)
        carries, accs = tiles(off, 1, state[2:2 + n_pairs], state[2 + n_pairs:], False)
        return (t + 1, unfinished(carries), *carries, *accs)

    carries = [carry_ref[p] for p in range(n_pairs)]
    accs = [acc_ref[p] for p in range(n_pairs)]
    state = lax.while_loop(cond, body, (jnp.int32(0), unfinished(carries), *carries, *accs))
    for p in range(n_pairs):
        o_ref[0, :, lanes(p)] = state[2 + n_pairs + p].astype(o_ref.dtype)


def _sb_attention(proj, lay):
    b, seq, _ = proj.shape
    tq = min(SB_TILE, seq)
    ch = SB_CHUNK
    assert seq % tq == 0 and tq % ch == 0
    q_off, width = lay["qa"]
    k_off, v_off = lay["ka"][0], lay["va"][0]
    n_pairs = SB_PAIRS_PER_STEP
    lw = n_pairs * PAIR
    assert width % lw == 0 and q_off % lw == 0 and k_off % lw == 0 and v_off % lw == 0
    tri = np.tril(np.ones((ch, ch), np.float32), -1)
    uu = np.concatenate([tri, np.ones((ch, PAIR), np.float32)], axis=1)
    uu = jnp.asarray(np.concatenate([uu, uu], axis=0), dtype=BF16)
    return pl.pallas_call(
        functools.partial(_sb_kernel, tq=tq),
        grid=(b, width // lw, seq // tq),
        in_specs=[
            pl.BlockSpec((1, tq, lw), lambda bi, p, i: (bi, i, q_off // lw + p)),
            pl.BlockSpec((1, seq, lw), lambda bi, p, i: (bi, 0, k_off // lw + p)),
            pl.BlockSpec((1, seq, lw), lambda bi, p, i: (bi, 0, v_off // lw + p)),
            pl.BlockSpec((2 * ch, ch + PAIR), lambda bi, p, i: (0, 0)),
        ],
        out_specs=pl.BlockSpec((1, tq, lw), lambda bi, p, i: (bi, i, p)),
        out_shape=jax.ShapeDtypeStruct((b, seq, width), BF16),
        scratch_shapes=[pltpu.VMEM((n_pairs, 2 * tq, PAIR), F32), pltpu.VMEM((n_pairs, tq, PAIR), F32)],
        compiler_params=_params("parallel", "parallel", "arbitrary"),
        name="sb_attention",
    )(proj, proj, proj, uu)


def _moba_kernel(q_ref, k_ref, v_ref, o_ref, kmean_ref, *, nb, nbp, group):
    i = pl.program_id(2)
    blk = MOBA_BLOCK
    tq = blk
    n_pairs = q_ref.shape[2] // PAIR

    def lanes(p):
        return slice(p * PAIR, (p + 1) * PAIR)

    @pl.when(i == 0)
    def _():
        kf = k_ref[0].astype(F32).reshape(nb, blk, n_pairs * PAIR)
        km = jnp.sum(kf, axis=1) * (1.0 / blk)
        if nbp > nb:
            km = jnp.concatenate([km, jnp.zeros((nbp - nb, n_pairs * PAIR), F32)], axis=0)
        kmean_ref[...] = km

    lo_qb = _lane_index((tq, PAIR), BF16) < HEAD_DIM
    lo_m = _lane_index((nbp, PAIR)) < HEAD_DIM
    jidx = lax.broadcasted_iota(jnp.int32, (nbp, tq), 0)
    valid = jidx < i

    def gated_queries(q, km):
        zq = jnp.zeros_like(q)
        plain, aug = [], []
        for h in (0, 1):
            head_m = lo_m if h == 0 else jnp.logical_not(lo_m)
            head_q = lo_qb if h == 0 else jnp.logical_not(lo_qb)
            kmh = jnp.where(head_m, km, 0.0)
            a = kmh.astype(BF16)
            r1 = kmh - a.astype(F32)
            b2 = r1.astype(BF16)
            c3 = (r1 - b2.astype(F32)).astype(BF16)
            g3 = lax.dot_general(jnp.concatenate([a, b2, c3], axis=0), q, NT_DIMS,
                                 preferred_element_type=F32)
            gate = g3[:nbp] + g3[nbp:2 * nbp] + g3[2 * nbp:]
            gate = jnp.where(valid, gate, -jnp.inf)
            beaten_by = jnp.zeros((nbp, tq), jnp.int32)
            for jp in range(nb):
                other = gate[jp:jp + 1, :]
                beats = (other > gate) | ((other == gate) & (jidx > jp))
                beaten_by = beaten_by + beats.astype(jnp.int32)
            sel = valid & (beaten_by < MOBA_TOPK)
            bias_t = jnp.where(sel, 0.0, NEG)
            top = HEAD_DIM if h == 0 else 0
            pieces = [jnp.zeros((top, tq), F32)] if top else []
            pieces.append(bias_t)
            if PAIR - top - nbp:
                pieces.append(jnp.zeros((PAIR - top - nbp, tq), F32))
            placed = jnp.concatenate(pieces, axis=0).T
            plain.append(jnp.where(head_q, q, zq))
            aug.append(jnp.where(head_q, q, placed.astype(BF16)))
        return plain, aug

    q_plain, q_aug = [], []
    for p in range(n_pairs):
        plain, aug = gated_queries(q_ref[0, :, lanes(p)], kmean_ref[:, lanes(p)])
        q_plain.append(plain)
        q_aug.append(aug)

    lo_q = _lane_index((tq, PAIR)) < HEAD_DIM
    lo_q2 = jnp.concatenate([lo_q, lo_q], axis=1)

    def attend(s0, s1, vblk, m0, m1, acc):
        n0 = jnp.maximum(m0, jnp.max(s0, axis=1, keepdims=True))
        n1 = jnp.maximum(m1, jnp.max(s1, axis=1, keepdims=True))
        p0 = jnp.exp(s0 - n0).astype(BF16)
        p1 = jnp.exp(s1 - n1).astype(BF16)
        alpha = jnp.where(lo_q, jnp.exp(m0 - n0), jnp.exp(m1 - n1))
        rhs = jnp.concatenate([vblk, jnp.ones_like(vblk)], axis=1)
        u0 = jnp.dot(p0, rhs, preferred_element_type=F32)
        u1 = jnp.dot(p1, rhs, preferred_element_type=F32)
        acc = acc * jnp.concatenate([alpha, alpha], axis=1) + jnp.where(lo_q2, u0, u1)
        return n0, n1, acc

    own_off = pl.multiple_of(i * blk, blk)
    k_own = k_ref[0, pl.ds(own_off, blk), :]
    v_own = v_ref[0, pl.ds(own_off, blk), :]
    row = lax.broadcasted_iota(jnp.int32, (tq, blk), 0)
    col = lax.broadcasted_iota(jnp.int32, (tq, blk), 1)
    causal = col <= row
    m_init = jnp.full((tq, 1), NEG, F32)
    state = []
    for p in range(n_pairs):
        s_own = [jnp.where(causal, lax.dot_general(qh, k_own[:, lanes(p)], NT_DIMS,
                                                   preferred_element_type=F32), NEG)
                 for qh in q_plain[p]]
        state += attend(s_own[0], s_own[1], v_own[:, lanes(p)],
                        m_init, m_init, jnp.zeros((tq, 2 * PAIR), F32))

    gk = group * blk
    lane_g = _lane_index((gk, PAIR), BF16)
    lo_g = lane_g < HEAD_DIM
    blk_in_group = (lax.broadcasted_iota(jnp.int32, (gk, PAIR), 0) // blk).astype(F32).astype(BF16)
    one = jnp.ones((gk, PAIR), BF16)
    zk = jnp.zeros((gk, PAIR), BF16)

    def body(g, st):
        o = pl.multiple_of(g * gk, gk)
        kg = k_ref[0, pl.ds(o, gk), :]
        vg = v_ref[0, pl.ds(o, gk), :]
        first = jnp.full((1, PAIR), g * group, jnp.int32).astype(F32).astype(BF16)
        blk_id = blk_in_group + first
        ind0 = jnp.where(lane_g == blk_id + HEAD_DIM, one, zk)
        ind1 = jnp.where(lane_g == blk_id, one, zk)
        scores = []
        for p in range(n_pairs):
            kp = kg[:, lanes(p)]
            scores.append((
                lax.dot_general(q_aug[p][0], jnp.where(lo_g, kp, ind0), NT_DIMS, preferred_element_type=F32),
                lax.dot_general(q_aug[p][1], jnp.where(lo_g, ind1, kp), NT_DIMS, preferred_element_type=F32)))
        new = []
        for p in range(n_pairs):
            new += attend(scores[p][0], scores[p][1], vg[:, lanes(p)], *st[3 * p:3 * p + 3])
        return tuple(new)

    state = lax.fori_loop(0, (i + group - 1) // group, body, tuple(state))
    for p in range(n_pairs):
        acc = state[3 * p + 2]
        o_ref[0, :, lanes(p)] = (acc[:, :PAIR] / acc[:, PAIR:]).astype(o_ref.dtype)


def _moba_attention(proj, lay):
    b, seq, _ = proj.shape
    tq = MOBA_BLOCK
    assert seq % MOBA_BLOCK == 0
    nb = seq // MOBA_BLOCK
    nbp = -(-nb // 8) * 8
    group = min(MOBA_GROUP, nb)
    assert nbp <= HEAD_DIM
    assert nb % group == 0
    q_off, width = lay["qb"]
    k_off, v_off = lay["kb"][0], lay["vb"][0]
    lw = MOBA_PAIRS_PER_STEP * PAIR
    assert width % lw == 0 and q_off % lw == 0 and k_off % lw == 0 and v_off % lw == 0
    return pl.pallas_call(
        functools.partial(_moba_kernel, nb=nb, nbp=nbp, group=group),
        grid=(b, width // lw, seq // tq),
        in_specs=[
            pl.BlockSpec((1, tq, lw), lambda bi, p, i: (bi, i, q_off // lw + p)),
            pl.BlockSpec((1, seq, lw), lambda bi, p, i: (bi, 0, k_off // lw + p)),
            pl.BlockSpec((1, seq, lw), lambda bi, p, i: (bi, 0, v_off // lw + p)),
        ],
        out_specs=pl.BlockSpec((1, tq, lw), lambda bi, p, i: (bi, i, p)),
        out_shape=jax.ShapeDtypeStruct((b, seq, width), BF16),
        scratch_shapes=[pltpu.VMEM((nbp, lw), F32)],
        compiler_params=_params("parallel", "parallel", "arbitrary"),
        name="moba_attention",
    )(proj, proj, proj)


def _swa_kernel(sink_ref, q_ref, kp_ref, kc_ref, vp_ref, vc_ref, o_ref, *, n_pairs):
    n = pl.program_id(1)
    w = WINDOW
    k = jnp.concatenate([kp_ref[0], kc_ref[0]], axis=0).astype(F32)
    v = jnp.concatenate([vp_ref[0], vc_ref[0]], axis=0).astype(F32)
    lo_k = lax.broadcasted_iota(jnp.int32, (2 * w, PAIR), 1) < HEAD_DIM
    k_sw = pltpu.roll(k, HEAD_DIM, axis=1)
    v_sw = pltpu.roll(v, HEAD_DIM, axis=1)
    kk = [jnp.where(lo_k, k, k_sw).astype(BF16), jnp.where(lo_k, k_sw, k).astype(BF16)]
    vv = [jnp.where(lo_k, v, v_sw).astype(BF16), jnp.where(lo_k, v_sw, v).astype(BF16)]
    ones = jnp.ones((2 * w, PAIR), BF16)
    rhs = [jnp.concatenate([vg, ones], axis=1) for vg in vv]

    row = lax.broadcasted_iota(jnp.int32, (w, 2 * w), 0)
    col = lax.broadcasted_iota(jnp.int32, (w, 2 * w), 1)
    delta = row + w - col
    valid = (delta >= 0) & (delta < w) & ((n - 1) * w + col >= 0)
    lo_q = _lane_index((w, PAIR)) < HEAD_DIM
    lo_qb = _lane_index((w, PAIR), BF16) < HEAD_DIM
    pairs_per_kv = SWA_GROUP // 2

    for p in range(n_pairs):
        g = p // pairs_per_kv
        qp = q_ref[0, :, p * PAIR:(p + 1) * PAIR]
        zq = jnp.zeros_like(qp)
        outs = []
        for h in (0, 1):
            qh = jnp.where(lo_qb, qp, zq) if h == 0 else jnp.where(lo_qb, zq, qp)
            s = lax.dot_general(qh, kk[g], NT_DIMS, preferred_element_type=F32)
            s = jnp.where(valid, s, NEG)
            sink = sink_ref[2 * p + h]
            m = jnp.maximum(jnp.max(s, axis=1, keepdims=True), sink)
            pr = jnp.exp(s - m).astype(BF16)
            o2 = jnp.dot(pr, rhs[g], preferred_element_type=F32)
            outs.append(o2[:, :PAIR] / (o2[:, PAIR:] + jnp.exp(sink - m)))
        o_ref[0, :, p * PAIR:(p + 1) * PAIR] = jnp.where(lo_q, outs[0], outs[1]).astype(o_ref.dtype)


def _swa_attention(proj, sinks, lay):
    b, seq, _ = proj.shape
    w = WINDOW
    q_off, width = lay["qc"]
    k_off, v_off = lay["kc"][0], lay["vc"][0]
    assert q_off % width == 0 and seq % w == 0
    return pl.pallas_call(
        functools.partial(_swa_kernel, n_pairs=width // PAIR),
        grid=(b, seq // w),
        in_specs=[
            pl.BlockSpec(memory_space=pltpu.SMEM),
            pl.BlockSpec((1, w, width), lambda bi, n: (bi, n, q_off // width)),
            pl.BlockSpec((1, w, PAIR), lambda bi, n: (bi, jnp.maximum(n - 1, 0), k_off // PAIR)),
            pl.BlockSpec((1, w, PAIR), lambda bi, n: (bi, n, k_off // PAIR)),
            pl.BlockSpec((1, w, PAIR), lambda bi, n: (bi, jnp.maximum(n - 1, 0), v_off // PAIR)),
            pl.BlockSpec((1, w, PAIR), lambda bi, n: (bi, n, v_off // PAIR)),
        ],
        out_specs=pl.BlockSpec((1, w, width), lambda bi, n: (bi, n, 0)),
        out_shape=jax.ShapeDtypeStruct((b, seq, width), BF16),
        compiler_params=_params("parallel", "arbitrary"),
        name="swa_attention",
    )(sinks.astype(F32), proj, proj, proj, proj, proj)


def _out_proj_kernel(ya_ref, yb_ref, yc_ref, ga_ref, gb_ref, gc_ref, w_ref, x_ref, o_ref, mix_ref,
                     *, tn):
    start = 0
    for y_ref, g_ref in ((ya_ref, ga_ref), (yb_ref, gb_ref), (yc_ref, gc_ref)):
        y = y_ref[...].astype(F32)
        ms = jnp.mean(y * y, axis=-1, keepdims=True)
        width = y.shape[1]
        mix_ref[:, start:start + width] = ((y * lax.rsqrt(ms + EPS)) * g_ref[...]).astype(BF16)
        start += width
    for c in range(o_ref.shape[1] // tn):
        cols = slice(c * tn, (c + 1) * tn)
        o_ref[:, cols] = x_ref[:, cols] + jnp.dot(mix_ref[...], w_ref[:, cols],
                                                  preferred_element_type=F32)


def _out_proj(ya, yb, yc, ga, gb, gc, w, layer, xt, tiles):
    n_tok, d = xt.shape
    tm, tn = tiles["tm"], tiles["tn_out"]
    wa, wb, wc = ya.shape[1], yb.shape[1], yc.shape[1]
    mix_w = wa + wb + wc
    assert d % tn == 0 and w.shape[1:] == (mix_w, d)
    return pl.pallas_call(
        functools.partial(_out_proj_kernel, tn=tn),
        grid=(n_tok // tm,),
        in_specs=[
            pl.BlockSpec((tm, wa), lambda i: (i, 0)),
            pl.BlockSpec((tm, wb), lambda i: (i, 0)),
            pl.BlockSpec((tm, wc), lambda i: (i, 0)),
            pl.BlockSpec((1, wa), lambda i: (0, 0)),
            pl.BlockSpec((1, wb), lambda i: (0, 0)),
            pl.BlockSpec((1, wc), lambda i: (0, 0)),
            pl.BlockSpec((None, mix_w, d), lambda i: (layer, 0, 0), pipeline_mode=pl.Buffered(1)),
            pl.BlockSpec((tm, d), lambda i: (i, 0)),
        ],
        out_specs=pl.BlockSpec((tm, d), lambda i: (i, 0)),
        out_shape=jax.ShapeDtypeStruct((n_tok, d), F32),
        scratch_shapes=[pltpu.VMEM((tm, mix_w), BF16)],
        compiler_params=_params("arbitrary"),
        name="out_proj",
    )(ya, yb, yc, ga.reshape(1, wa), gb.reshape(1, wb), gc.reshape(1, wc), w, xt)


def _mlp_kernel(x_ref, g_ref, wu_ref, wd_ref, gf_ref, o_ref, h_ref, *, final):
    f = pl.program_id(1)

    @pl.when(f == 0)
    def _():
        x = x_ref[...]
        ms = jnp.mean(x * x, axis=-1, keepdims=True)
        h_ref[...] = ((x * lax.rsqrt(ms + EPS)) * g_ref[...]).astype(BF16)
        o_ref[...] = x

    u = jnp.maximum(jnp.dot(h_ref[...], wu_ref[...], preferred_element_type=F32), 0.0)
    o_ref[...] += jnp.dot((u * u).astype(BF16), wd_ref[...], preferred_element_type=F32)

    if final:
        @pl.when(f == pl.num_programs(1) - 1)
        def _():
            y = o_ref[...]
            ms = jnp.mean(y * y, axis=-1, keepdims=True)
            o_ref[...] = (y * lax.rsqrt(ms + EPS)) * gf_ref[...]


def _mlp(xt, gain, w_up, w_down, layer, final_gain, tiles, final):
    n_tok, d = xt.shape
    d_ff = w_up.shape[2]
    tm, tf = tiles["tm_mlp"], tiles["tf"]
    assert d_ff % tf == 0 and n_tok % tm == 0
    return pl.pallas_call(
        functools.partial(_mlp_kernel, final=final),
        grid=(n_tok // tm, d_ff // tf),
        in_specs=[
            pl.BlockSpec((tm, d), lambda i, f: (i, 0), pipeline_mode=pl.Buffered(1)),
            pl.BlockSpec((1, d), lambda i, f: (0, 0)),
            pl.BlockSpec((None, d, tf), lambda i, f: (layer, 0, f)),
            pl.BlockSpec((None, tf, d), lambda i, f: (layer, f, 0)),
            pl.BlockSpec((1, d), lambda i, f: (0, 0)),
        ],
        out_specs=pl.BlockSpec((tm, d), lambda i, f: (i, 0)),
        out_shape=jax.ShapeDtypeStruct((n_tok, d), F32),
        scratch_shapes=[pltpu.VMEM((tm, d), BF16)],
        compiler_params=_params("parallel", "arbitrary"),
        name="mlp",
    )(xt, gain.reshape(1, d), w_up, w_down, final_gain.reshape(1, d))


def kernel(x, attn_norm, w_in, sinks, gn_sb, gn_moba, gn_swa, w_out, mlp_norm, w_up, w_down, final_norm):
    b, seq, d = x.shape
    depth = w_in.shape[0]
    n_tok = b * seq
    lay = _layout(d)
    tiles = _tiles(n_tok, seq)
    cos_t, sin_t = _rope_tables(seq)
    xt = x.reshape(n_tok, d)
    w_in, w_out, w_up, w_down = (w.astype(BF16) for w in (w_in, w_out, w_up, w_down))
    for l in range(depth):
        proj = _in_proj(xt, attn_norm[l], w_in, l, cos_t, sin_t, lay, seq, tiles)
        proj = proj.reshape(b, seq, lay["in_width"])
        ya = _sb_attention(proj, lay).reshape(n_tok, -1)
        yb = _moba_attention(proj, lay).reshape(n_tok, -1)
        yc = _swa_attention(proj, sinks[l], lay).reshape(n_tok, -1)
        xt = _out_proj(ya, yb, yc, gn_sb[l], gn_moba[l], gn_swa[l], w_out, l, xt, tiles)
        xt = _mlp(xt, mlp_norm[l], w_up, w_down, l, final_norm, tiles, final=(l == depth - 1))
    return xt.reshape(b, seq, d)
```

```python
import functools

import numpy as np
import jax
import jax.numpy as jnp
from jax import lax
from jax.experimental import pallas as pl
from jax.experimental.pallas import tpu as pltpu

F32 = jnp.float32
BF16 = jnp.bfloat16

HEAD_DIM = 64
PAIR = 2 * HEAD_DIM
BF16_SUBLANES = 16
ROPE_HALF = HEAD_DIM // 2
MOBA_BLOCK = 256
MOBA_TOPK = 3
WINDOW = 128
SWA_GROUP = 8
ROPE_THETA = 10000.0
EPS = 1e-6
NEG = -1e30
Q_SCALE = HEAD_DIM ** -0.5
MOBA_GROUP = 4
MOBA_PAIRS_PER_STEP = 2
SB_TILE = 256
SB_CHUNK = PAIR
SB_PAIRS_PER_STEP = 2
SB_UNDERFLOW = 104.0

VMEM_LIMIT_BYTES = 56 * 1024 * 1024

NT_DIMS = (((1,), (1,)), ((), ()))


def _params(*semantics):
    return pltpu.CompilerParams(dimension_semantics=semantics,
                                vmem_limit_bytes=VMEM_LIMIT_BYTES)


def _lane_index(shape, dtype=jnp.int32):
    idx = lax.broadcasted_iota(jnp.int32, shape, len(shape) - 1)
    return idx if dtype == jnp.int32 else idx.astype(F32).astype(dtype)


def _layout(d_model):
    sb = d_model // 4
    moba = d_model // 4
    swa_q = d_model // 2
    swa_kv = (swa_q // HEAD_DIM // SWA_GROUP) * HEAD_DIM
    sizes = (sb, sb, sb, moba, moba, moba, swa_q, swa_kv, swa_kv)
    offs = np.concatenate([[0], np.cumsum(sizes)]).astype(int)
    names = ("qa", "ka", "va", "qb", "kb", "vb", "qc", "kc", "vc")
    lay = {n: (int(offs[i]), int(sizes[i])) for i, n in enumerate(names)}
    lay["in_width"] = int(offs[-1])
    assert sb % PAIR == 0 and swa_q % PAIR == 0 and swa_kv == PAIR
    return lay


def _tiles(n_tokens, seq):
    tm = min(512, seq)
    assert seq % tm == 0 and n_tokens % tm == 0
    return dict(tm=tm, tm_mlp=min(1024, n_tokens), tn_in=256, tn_out=512, tf=512)


def _rope_tables(seq):
    inv_freq = ROPE_THETA ** (-jnp.arange(ROPE_HALF, dtype=F32) * 2.0 / HEAD_DIM)
    ang = jnp.arange(seq, dtype=F32)[:, None] * inv_freq[None, :]
    cos, sin = jnp.cos(ang), jnp.sin(ang)
    cos_t = jnp.tile(cos, (1, PAIR // ROPE_HALF))
    sin_t = jnp.tile(jnp.concatenate([-sin, sin], axis=1), (1, PAIR // HEAD_DIM))
    return cos_t, sin_t


def _in_proj_kernel(x_ref, g_ref, w_ref, cos_ref, sin_ref, wu_ref, wd_ref,
                    o_ref, wu_o_ref, wd_o_ref, h_ref, *, chunks):
    wu_o_ref[...] = wu_ref[...].astype(wu_o_ref.dtype)
    wd_o_ref[...] = wd_ref[...].astype(wd_o_ref.dtype)
    x = x_ref[...]
    ms = jnp.mean(x * x, axis=-1, keepdims=True)
    h_ref[...] = ((x * lax.rsqrt(ms + EPS)) * g_ref[...]).astype(BF16)
    tm = x.shape[0]
    lane = lax.broadcasted_iota(jnp.int32, (tm, PAIR), 1)
    first = (lane % HEAD_DIM) < ROPE_HALF
    for start, classes in chunks:
        width = len(classes) * PAIR
        acc = jnp.dot(h_ref[...], w_ref[:, start:start + width], preferred_element_type=F32)
        for t, (rope, scale) in enumerate(classes):
            a = acc[:, t * PAIR:(t + 1) * PAIR]
            if rope:
                partner = jnp.where(first, pltpu.roll(a, PAIR - ROPE_HALF, axis=1),
                                    pltpu.roll(a, ROPE_HALF, axis=1))
                a = a * cos_ref[...] + partner * sin_ref[...]
            if scale != 1.0:
                a = a * scale
            o_ref[:, start + t * PAIR:start + (t + 1) * PAIR] = a.astype(o_ref.dtype)


def _in_proj(xt, gain, w, layer, cos_t, sin_t, w_up, w_down, lay, seq, tiles):
    n_tok, d = xt.shape
    in_w = lay["in_width"]
    tm, tn = tiles["tm"], tiles["tn_in"]
    steps = n_tok // tm
    d_ff = w_up.shape[2]
    up_rows, down_rows = d // steps, d_ff // steps
    assert in_w % tn == 0 and tn % PAIR == 0
    assert d % steps == 0 and d_ff % steps == 0 and up_rows % BF16_SUBLANES == 0 and down_rows % BF16_SUBLANES == 0
    rope = np.zeros(in_w // PAIR, bool)
    scale = np.ones(in_w // PAIR, np.float32)
    for name in ("qb", "kb", "qc", "kc"):
        o, s = lay[name]
        rope[o // PAIR:(o + s) // PAIR] = True
    for name in ("qa", "qb", "qc"):
        o, s = lay[name]
        scale[o // PAIR:(o + s) // PAIR] = Q_SCALE
    per = tn // PAIR
    chunks = tuple((c * tn, tuple((bool(rope[c * per + t]), float(scale[c * per + t])) for t in range(per)))
                   for c in range(in_w // tn))
    pos_blocks = seq // tm
    return pl.pallas_call(
        functools.partial(_in_proj_kernel, chunks=chunks),
        grid=(steps,),
        in_specs=[
            pl.BlockSpec((tm, d), lambda i: (i, 0)),
            pl.BlockSpec((1, d), lambda i: (0, 0)),
            pl.BlockSpec((None, d, in_w), lambda i: (layer, 0, 0), pipeline_mode=pl.Buffered(1)),
            pl.BlockSpec((tm, PAIR), lambda i: (i % pos_blocks, 0)),
            pl.BlockSpec((tm, PAIR), lambda i: (i % pos_blocks, 0)),
            pl.BlockSpec((None, up_rows, d_ff), lambda i: (layer, i, 0)),
            pl.BlockSpec((None, down_rows, d), lambda i: (layer, i, 0)),
        ],
        out_specs=[
            pl.BlockSpec((tm, in_w), lambda i: (i, 0)),
            pl.BlockSpec((up_rows, d_ff), lambda i: (i, 0)),
            pl.BlockSpec((down_rows, d), lambda i: (i, 0)),
        ],
        out_shape=[
            jax.ShapeDtypeStruct((n_tok, in_w), BF16),
            jax.ShapeDtypeStruct((d, d_ff), BF16),
            jax.ShapeDtypeStruct((d_ff, d), BF16),
        ],
        scratch_shapes=[pltpu.VMEM((tm, d), BF16)],
        compiler_params=_params("arbitrary"),
        name="in_proj",
    )(xt, gain.reshape(1, d), w, cos_t, sin_t, w_up, w_down)


def _sb_kernel(q_ref, k_ref, v_ref, u_ref, o_ref, carry_ref, acc_ref, *, tq):
    i = pl.program_id(2)
    ch = SB_CHUNK
    n_pairs = q_ref.shape[2] // PAIR

    def lanes(p):
        return slice(p * PAIR, (p + 1) * PAIR)

    lo = _lane_index((tq, PAIR), BF16) < HEAD_DIM
    qs = []
    for p in range(n_pairs):
        q = q_ref[0, :, lanes(p)]
        zq = jnp.zeros_like(q)
        qs.append(jnp.concatenate([jnp.where(lo, q, zq), jnp.where(lo, zq, q)], axis=0))
    u = u_ref[...]
    row = lax.broadcasted_iota(jnp.int32, (2 * tq, tq), 0) % tq
    col = lax.broadcasted_iota(jnp.int32, (2 * tq, tq), 1)
    past = col < row

    def tiles(off, n_tiles, carries, accs, diagonal):
        nk = n_tiles * tq
        kblk = k_ref[0, pl.ds(off, nk), :]
        vblk = v_ref[0, pl.ds(off, nk), :]
        zs = [lax.dot_general(qs[p], kblk[:, lanes(p)], NT_DIMS, preferred_element_type=F32)
              for p in range(n_pairs)]
        sps = [jnp.maximum(z, 0.0) + jnp.log(1.0 + jnp.exp(-jnp.abs(z))) for z in zs]
        ws = [[None] * (nk // ch) for _ in range(n_pairs)]
        carries = list(carries)
        for c in reversed(range(nk // ch)):
            sl = slice(c * ch, (c + 1) * ch)
            masked = diagonal and c * ch >= nk - tq
            mask = past[:, c * ch - (nk - tq):(c + 1) * ch - (nk - tq)] if masked else None
            for p in range(n_pairs):
                s_c = jnp.where(mask, sps[p][:, sl], 0.0) if masked else sps[p][:, sl]
                hi = s_c.astype(BF16)
                lo_part = (s_c - hi.astype(F32)).astype(BF16)
                r = jnp.dot(jnp.concatenate([hi, lo_part], axis=1), u, preferred_element_type=F32)
                w = jnp.exp(zs[p][:, sl] - sps[p][:, sl] - r[:, :ch] - carries[p])
                if masked:
                    w = jnp.where(mask, w, 0.0)
                ws[p][c] = w.astype(BF16)
                carries[p] = carries[p] + r[:, ch:]
        lo_v = _lane_index((nk, PAIR), BF16) < HEAD_DIM
        accs = list(accs)
        for p in range(n_pairs):
            wb = jnp.concatenate(ws[p], axis=1)
            wcat = jnp.concatenate([wb[:tq], wb[tq:]], axis=1)
            vp = vblk[:, lanes(p)]
            zv = jnp.zeros_like(vp)
            vcat = jnp.concatenate([jnp.where(lo_v, vp, zv), jnp.where(lo_v, zv, vp)], axis=0)
            accs[p] = accs[p] + jnp.dot(wcat, vcat, preferred_element_type=F32)
        return carries, accs

    zero_carry = [jnp.zeros((2 * tq, PAIR), F32)] * n_pairs
    zero_acc = [jnp.zeros((tq, PAIR), F32)] * n_pairs

    def first_pass(off, n_tiles):
        carries, accs = tiles(off, n_tiles, zero_carry, zero_acc, True)
        for p in range(n_pairs):
            carry_ref[p] = carries[p]
            acc_ref[p] = accs[p]

    @pl.when(i == 0)
    def _():
        first_pass(0, 1)

    @pl.when(i > 0)
    def _():
        first_pass(pl.multiple_of((i - 1) * tq, tq), 2)

    def unfinished(carries):
        return functools.reduce(jnp.minimum, [jnp.min(c) for c in carries]) < SB_UNDERFLOW

    def cond(state):
        return (state[0] < i - 1) & state[1]

    def body(state):
        t = state[0]
        off = pl.multiple_of((i - 2 - t) * tq, tq)
        carries, accs = tiles(off, 1, state[2:2 + n_pairs], state[2 + n_pairs:], False)
        return (t + 1, unfinished(carries), *carries, *accs)

    carries = [carry_ref[p] for p in range(n_pairs)]
    accs = [acc_ref[p] for p in range(n_pairs)]
    state = lax.while_loop(cond, body, (jnp.int32(0), unfinished(carries), *carries, *accs))
    for p in range(n_pairs):
        o_ref[0, :, lanes(p)] = state[2 + n_pairs + p].astype(o_ref.dtype)


def _sb_attention(proj, lay):
    b, seq, _ = proj.shape
    tq = min(SB_TILE, seq)
    ch = SB_CHUNK
    assert seq % tq == 0 and tq % ch == 0
    q_off, width = lay["qa"]
    k_off, v_off = lay["ka"][0], lay["va"][0]
    n_pairs = SB_PAIRS_PER_STEP
    lw = n_pairs * PAIR
    assert width % lw == 0 and q_off % lw == 0 and k_off % lw == 0 and v_off % lw == 0
    tri = np.tril(np.ones((ch, ch), np.float32), -1)
    uu = np.concatenate([tri, np.ones((ch, PAIR), np.float32)], axis=1)
    uu = jnp.asarray(np.concatenate([uu, uu], axis=0), dtype=BF16)
    return pl.pallas_call(
        functools.partial(_sb_kernel, tq=tq),
        grid=(b, width // lw, seq // tq),
        in_specs=[
            pl.BlockSpec((1, tq, lw), lambda bi, p, i: (bi, i, q_off // lw + p)),
            pl.BlockSpec((1, seq, lw), lambda bi, p, i: (bi, 0, k_off // lw + p)),
            pl.BlockSpec((1, seq, lw), lambda bi, p, i: (bi, 0, v_off // lw + p)),
            pl.BlockSpec((2 * ch, ch + PAIR), lambda bi, p, i: (0, 0)),
        ],
        out_specs=pl.BlockSpec((1, tq, lw), lambda bi, p, i: (bi, i, p)),
        out_shape=jax.ShapeDtypeStruct((b, seq, width), BF16),
        scratch_shapes=[pltpu.VMEM((n_pairs, 2 * tq, PAIR), F32), pltpu.VMEM((n_pairs, tq, PAIR), F32)],
        compiler_params=_params("parallel", "parallel", "arbitrary"),
        name="sb_attention",
    )(proj, proj, proj, uu)


def _moba_kernel(q_ref, k_ref, v_ref, o_ref, kmean_ref, *, nb, nbp, group):
    i = pl.program_id(2)
    blk = MOBA_BLOCK
    tq = blk
    n_pairs = q_ref.shape[2] // PAIR

    def lanes(p):
        return slice(p * PAIR, (p + 1) * PAIR)

    @pl.when(i == 0)
    def _():
        kf = k_ref[0].astype(F32).reshape(nb, blk, n_pairs * PAIR)
        km = jnp.sum(kf, axis=1) * (1.0 / blk)
        if nbp > nb:
            km = jnp.concatenate([km, jnp.zeros((nbp - nb, n_pairs * PAIR), F32)], axis=0)
        kmean_ref[...] = km

    lo_qb = _lane_index((tq, PAIR), BF16) < HEAD_DIM
    lo_m = _lane_index((nbp, PAIR)) < HEAD_DIM
    jidx = lax.broadcasted_iota(jnp.int32, (nbp, tq), 0)
    valid = jidx < i

    def gated_queries(q, km):
        zq = jnp.zeros_like(q)
        plain, aug = [], []
        for h in (0, 1):
            head_m = lo_m if h == 0 else jnp.logical_not(lo_m)
            head_q = lo_qb if h == 0 else jnp.logical_not(lo_qb)
            kmh = jnp.where(head_m, km, 0.0)
            a = kmh.astype(BF16)
            r1 = kmh - a.astype(F32)
            b2 = r1.astype(BF16)
            c3 = (r1 - b2.astype(F32)).astype(BF16)
            g3 = lax.dot_general(jnp.concatenate([a, b2, c3], axis=0), q, NT_DIMS,
                                 preferred_element_type=F32)
            gate = g3[:nbp] + g3[nbp:2 * nbp] + g3[2 * nbp:]
            gate = jnp.where(valid, gate, -jnp.inf)
            beaten_by = jnp.zeros((nbp, tq), jnp.int32)
            for jp in range(nb):
                other = gate[jp:jp + 1, :]
                beats = (other > gate) | ((other == gate) & (jidx > jp))
                beaten_by = beaten_by + beats.astype(jnp.int32)
            sel = valid & (beaten_by < MOBA_TOPK)
            bias_t = jnp.where(sel, 0.0, NEG)
            top = HEAD_DIM if h == 0 else 0
            pieces = [jnp.zeros((top, tq), F32)] if top else []
            pieces.append(bias_t)
            if PAIR - top - nbp:
                pieces.append(jnp.zeros((PAIR - top - nbp, tq), F32))
            placed = jnp.concatenate(pieces, axis=0).T
            plain.append(jnp.where(head_q, q, zq))
            aug.append(jnp.where(head_q, q, placed.astype(BF16)))
        return plain, aug

    q_plain, q_aug = [], []
    for p in range(n_pairs):
        plain, aug = gated_queries(q_ref[0, :, lanes(p)], kmean_ref[:, lanes(p)])
        q_plain.append(plain)
        q_aug.append(aug)

    lo_q = _lane_index((tq, PAIR)) < HEAD_DIM
    lo_q2 = jnp.concatenate([lo_q, lo_q], axis=1)

    def attend(s0, s1, vblk, m0, m1, acc):
        n0 = jnp.maximum(m0, jnp.max(s0, axis=1, keepdims=True))
        n1 = jnp.maximum(m1, jnp.max(s1, axis=1, keepdims=True))
        p0 = jnp.exp(s0 - n0).astype(BF16)
        p1 = jnp.exp(s1 - n1).astype(BF16)
        alpha = jnp.where(lo_q, jnp.exp(m0 - n0), jnp.exp(m1 - n1))
        rhs = jnp.concatenate([vblk, jnp.ones_like(vblk)], axis=1)
        u0 = jnp.dot(p0, rhs, preferred_element_type=F32)
        u1 = jnp.dot(p1, rhs, preferred_element_type=F32)
        acc = acc * jnp.concatenate([alpha, alpha], axis=1) + jnp.where(lo_q2, u0, u1)
        return n0, n1, acc

    own_off = pl.multiple_of(i * blk, blk)
    k_own = k_ref[0, pl.ds(own_off, blk), :]
    v_own = v_ref[0, pl.ds(own_off, blk), :]
    row = lax.broadcasted_iota(jnp.int32, (tq, blk), 0)
    col = lax.broadcasted_iota(jnp.int32, (tq, blk), 1)
    causal = col <= row
    m_init = jnp.full((tq, 1), NEG, F32)
    state = []
    for p in range(n_pairs):
        s_own = [jnp.where(causal, lax.dot_general(qh, k_own[:, lanes(p)], NT_DIMS,
                                                   preferred_element_type=F32), NEG)
                 for qh in q_plain[p]]
        state += attend(s_own[0], s_own[1], v_own[:, lanes(p)],
                        m_init, m_init, jnp.zeros((tq, 2 * PAIR), F32))

    gk = group * blk
    lane_g = _lane_index((gk, PAIR), BF16)
    lo_g = lane_g < HEAD_DIM
    blk_in_group = (lax.broadcasted_iota(jnp.int32, (gk, PAIR), 0) // blk).astype(F32).astype(BF16)
    one = jnp.ones((gk, PAIR), BF16)
    zk = jnp.zeros((gk, PAIR), BF16)

    def body(g, st):
        o = pl.multiple_of(g * gk, gk)
        kg = k_ref[0, pl.ds(o, gk), :]
        vg = v_ref[0, pl.ds(o, gk), :]
        first = jnp.full((1, PAIR), g * group, jnp.int32).astype(F32).astype(BF16)
        blk_id = blk_in_group + first
        ind0 = jnp.where(lane_g == blk_id + HEAD_DIM, one, zk)
        ind1 = jnp.where(lane_g == blk_id, one, zk)
        scores = []
        for p in range(n_pairs):
            kp = kg[:, lanes(p)]
            scores.append((
                lax.dot_general(q_aug[p][0], jnp.where(lo_g, kp, ind0), NT_DIMS, preferred_element_type=F32),
                lax.dot_general(q_aug[p][1], jnp.where(lo_g, ind1, kp), NT_DIMS, preferred_element_type=F32)))
        new = []
        for p in range(n_pairs):
            new += attend(scores[p][0], scores[p][1], vg[:, lanes(p)], *st[3 * p:3 * p + 3])
        return tuple(new)

    state = lax.fori_loop(0, (i + group - 1) // group, body, tuple(state))
    for p in range(n_pairs):
        acc = state[3 * p + 2]
        o_ref[0, :, lanes(p)] = (acc[:, :PAIR] / acc[:, PAIR:]).astype(o_ref.dtype)


def _moba_attention(proj, lay):
    b, seq, _ = proj.shape
    tq = MOBA_BLOCK
    assert seq % MOBA_BLOCK == 0
    nb = seq // MOBA_BLOCK
    nbp = -(-nb // 8) * 8
    group = min(MOBA_GROUP, nb)
    assert nbp <= HEAD_DIM
    assert nb % group == 0
    q_off, width = lay["qb"]
    k_off, v_off = lay["kb"][0], lay["vb"][0]
    lw = MOBA_PAIRS_PER_STEP * PAIR
    assert width % lw == 0 and q_off % lw == 0 and k_off % lw == 0 and v_off % lw == 0
    return pl.pallas_call(
        functools.partial(_moba_kernel, nb=nb, nbp=nbp, group=group),
        grid=(b, width // lw, seq // tq),
        in_specs=[
            pl.BlockSpec((1, tq, lw), lambda bi, p, i: (bi, i, q_off // lw + p)),
            pl.BlockSpec((1, seq, lw), lambda bi, p, i: (bi, 0, k_off // lw + p)),
            pl.BlockSpec((1, seq, lw), lambda bi, p, i: (bi, 0, v_off // lw + p)),
        ],
        out_specs=pl.BlockSpec((1, tq, lw), lambda bi, p, i: (bi, i, p)),
        out_shape=jax.ShapeDtypeStruct((b, seq, width), BF16),
        scratch_shapes=[pltpu.VMEM((nbp, lw), F32)],
        compiler_params=_params("parallel", "parallel", "arbitrary"),
        name="moba_attention",
    )(proj, proj, proj)


def _swa_kernel(sink_ref, q_ref, kp_ref, kc_ref, vp_ref, vc_ref, o_ref, *, n_pairs):
    n = pl.program_id(1)
    w = WINDOW
    k = jnp.concatenate([kp_ref[0], kc_ref[0]], axis=0).astype(F32)
    v = jnp.concatenate([vp_ref[0], vc_ref[0]], axis=0).astype(F32)
    lo_k = lax.broadcasted_iota(jnp.int32, (2 * w, PAIR), 1) < HEAD_DIM
    k_sw = pltpu.roll(k, HEAD_DIM, axis=1)
    v_sw = pltpu.roll(v, HEAD_DIM, axis=1)
    kk = [jnp.where(lo_k, k, k_sw).astype(BF16), jnp.where(lo_k, k_sw, k).astype(BF16)]
    vv = [jnp.where(lo_k, v, v_sw).astype(BF16), jnp.where(lo_k, v_sw, v).astype(BF16)]
    ones = jnp.ones((2 * w, PAIR), BF16)
    rhs = [jnp.concatenate([vg, ones], axis=1) for vg in vv]

    row = lax.broadcasted_iota(jnp.int32, (w, 2 * w), 0)
    col = lax.broadcasted_iota(jnp.int32, (w, 2 * w), 1)
    delta = row + w - col
    valid = (delta >= 0) & (delta < w) & ((n - 1) * w + col >= 0)
    lo_q = _lane_index((w, PAIR)) < HEAD_DIM
    lo_qb = _lane_index((w, PAIR), BF16) < HEAD_DIM
    pairs_per_kv = SWA_GROUP // 2

    for p in range(n_pairs):
        g = p // pairs_per_kv
        qp = q_ref[0, :, p * PAIR:(p + 1) * PAIR]
        zq = jnp.zeros_like(qp)
        outs = []
        for h in (0, 1):
            qh = jnp.where(lo_qb, qp, zq) if h == 0 else jnp.where(lo_qb, zq, qp)
            s = lax.dot_general(qh, kk[g], NT_DIMS, preferred_element_type=F32)
            s = jnp.where(valid, s, NEG)
            sink = sink_ref[2 * p + h]
            m = jnp.maximum(jnp.max(s, axis=1, keepdims=True), sink)
            pr = jnp.exp(s - m).astype(BF16)
            o2 = jnp.dot(pr, rhs[g], preferred_element_type=F32)
            outs.append(o2[:, :PAIR] / (o2[:, PAIR:] + jnp.exp(sink - m)))
        o_ref[0, :, p * PAIR:(p + 1) * PAIR] = jnp.where(lo_q, outs[0], outs[1]).astype(o_ref.dtype)


def _swa_attention(proj, sinks, lay):
    b, seq, _ = proj.shape
    w = WINDOW
    q_off, width = lay["qc"]
    k_off, v_off = lay["kc"][0], lay["vc"][0]
    assert q_off % width == 0 and seq % w == 0
    return pl.pallas_call(
        functools.partial(_swa_kernel, n_pairs=width // PAIR),
        grid=(b, seq // w),
        in_specs=[
            pl.BlockSpec(memory_space=pltpu.SMEM),
            pl.BlockSpec((1, w, width), lambda bi, n: (bi, n, q_off // width)),
            pl.BlockSpec((1, w, PAIR), lambda bi, n: (bi, jnp.maximum(n - 1, 0), k_off // PAIR)),
            pl.BlockSpec((1, w, PAIR), lambda bi, n: (bi, n, k_off // PAIR)),
            pl.BlockSpec((1, w, PAIR), lambda bi, n: (bi, jnp.maximum(n - 1, 0), v_off // PAIR)),
            pl.BlockSpec((1, w, PAIR), lambda bi, n: (bi, n, v_off // PAIR)),
        ],
        out_specs=pl.BlockSpec((1, w, width), lambda bi, n: (bi, n, 0)),
        out_shape=jax.ShapeDtypeStruct((b, seq, width), BF16),
        compiler_params=_params("parallel", "arbitrary"),
        name="swa_attention",
    )(sinks.astype(F32), proj, proj, proj, proj, proj)


def _out_proj_kernel(ya_ref, yb_ref, yc_ref, ga_ref, gb_ref, gc_ref, w_ref, x_ref, o_ref, mix_ref,
                     *, tn):
    start = 0
    for y_ref, g_ref in ((ya_ref, ga_ref), (yb_ref, gb_ref), (yc_ref, gc_ref)):
        y = y_ref[...].astype(F32)
        ms = jnp.mean(y * y, axis=-1, keepdims=True)
        width = y.shape[1]
        mix_ref[:, start:start + width] = ((y * lax.rsqrt(ms + EPS)) * g_ref[...]).astype(BF16)
        start += width
    for c in range(o_ref.shape[1] // tn):
        cols = slice(c * tn, (c + 1) * tn)
        o_ref[:, cols] = x_ref[:, cols] + jnp.dot(mix_ref[...], w_ref[:, cols],
                                                  preferred_element_type=F32)


def _out_proj(ya, yb, yc, ga, gb, gc, w, layer, xt, tiles):
    n_tok, d = xt.shape
    tm, tn = tiles["tm"], tiles["tn_out"]
    wa, wb, wc = ya.shape[1], yb.shape[1], yc.shape[1]
    mix_w = wa + wb + wc
    assert d % tn == 0 and w.shape[1:] == (mix_w, d)
    return pl.pallas_call(
        functools.partial(_out_proj_kernel, tn=tn),
        grid=(n_tok // tm,),
        in_specs=[
            pl.BlockSpec((tm, wa), lambda i: (i, 0)),
            pl.BlockSpec((tm, wb), lambda i: (i, 0)),
            pl.BlockSpec((tm, wc), lambda i: (i, 0)),
            pl.BlockSpec((1, wa), lambda i: (0, 0)),
            pl.BlockSpec((1, wb), lambda i: (0, 0)),
            pl.BlockSpec((1, wc), lambda i: (0, 0)),
            pl.BlockSpec((None, mix_w, d), lambda i: (layer, 0, 0), pipeline_mode=pl.Buffered(1)),
            pl.BlockSpec((tm, d), lambda i: (i, 0)),
        ],
        out_specs=pl.BlockSpec((tm, d), lambda i: (i, 0)),
        out_shape=jax.ShapeDtypeStruct((n_tok, d), F32),
        scratch_shapes=[pltpu.VMEM((tm, mix_w), BF16)],
        compiler_params=_params("arbitrary"),
        name="out_proj",
    )(ya, yb, yc, ga.reshape(1, wa), gb.reshape(1, wb), gc.reshape(1, wc), w, xt)


def _mlp_kernel(x_ref, g_ref, wu_ref, wd_ref, gf_ref, o_ref, h_ref, *, final):
    f = pl.program_id(1)

    @pl.when(f == 0)
    def _():
        x = x_ref[...]
        ms = jnp.mean(x * x, axis=-1, keepdims=True)
        h_ref[...] = ((x * lax.rsqrt(ms + EPS)) * g_ref[...]).astype(BF16)
        o_ref[...] = x

    u = jnp.maximum(jnp.dot(h_ref[...], wu_ref[...], preferred_element_type=F32), 0.0)
    o_ref[...] += jnp.dot((u * u).astype(BF16), wd_ref[...], preferred_element_type=F32)

    if final:
        @pl.when(f == pl.num_programs(1) - 1)
        def _():
            y = o_ref[...]
            ms = jnp.mean(y * y, axis=-1, keepdims=True)
            o_ref[...] = (y * lax.rsqrt(ms + EPS)) * gf_ref[...]


def _mlp(xt, gain, w_up, w_down, final_gain, tiles, final):
    n_tok, d = xt.shape
    d_ff = w_up.shape[1]
    tm, tf = tiles["tm_mlp"], tiles["tf"]
    assert d_ff % tf == 0 and n_tok % tm == 0
    return pl.pallas_call(
        functools.partial(_mlp_kernel, final=final),
        grid=(n_tok // tm, d_ff // tf),
        in_specs=[
            pl.BlockSpec((tm, d), lambda i, f: (i, 0)),
            pl.BlockSpec((1, d), lambda i, f: (0, 0)),
            pl.BlockSpec((d, tf), lambda i, f: (0, f)),
            pl.BlockSpec((tf, d), lambda i, f: (f, 0)),
            pl.BlockSpec((1, d), lambda i, f: (0, 0)),
        ],
        out_specs=pl.BlockSpec((tm, d), lambda i, f: (i, 0)),
        out_shape=jax.ShapeDtypeStruct((n_tok, d), F32),
        scratch_shapes=[pltpu.VMEM((tm, d), BF16)],
        compiler_params=_params("parallel", "arbitrary"),
        name="mlp",
    )(xt, gain.reshape(1, d), w_up, w_down, final_gain.reshape(1, d))


def kernel(x, attn_norm, w_in, sinks, gn_sb, gn_moba, gn_swa, w_out, mlp_norm, w_up, w_down, final_norm):
    b, seq, d = x.shape
    depth = w_in.shape[0]
    n_tok = b * seq
    lay = _layout(d)
    tiles = _tiles(n_tok, seq)
    cos_t, sin_t = _rope_tables(seq)
    xt = x.reshape(n_tok, d)
    w_in, w_out = w_in.astype(BF16), w_out.astype(BF16)
    for l in range(depth):
        proj, w_up_l, w_down_l = _in_proj(xt, attn_norm[l], w_in, l, cos_t, sin_t, w_up, w_down,
                                          lay, seq, tiles)
        proj = proj.reshape(b, seq, lay["in_width"])
        ya = _sb_attention(proj, lay).reshape(n_tok, -1)
        yb = _moba_attention(proj, lay).reshape(n_tok, -1)
        yc = _swa_attention(proj, sinks[l], lay).reshape(n_tok, -1)
        xt = _out_proj(ya, yb, yc, gn_sb[l], gn_moba[l], gn_swa[l], w_out, l, xt, tiles)
        xt = _mlp(xt, mlp_norm[l], w_up_l, w_down_l, final_norm, tiles, final=(l == depth - 1))
    return xt.reshape(b, seq, d)
```

```python
import functools

import numpy as np
import jax
import jax.numpy as jnp
from jax import lax
from jax.experimental import pallas as pl
from jax.experimental.pallas import tpu as pltpu

F32 = jnp.float32
BF16 = jnp.bfloat16

HEAD_DIM = 64
PAIR = 2 * HEAD_DIM
BF16_SUBLANES = 16
ROPE_HALF = HEAD_DIM // 2
MOBA_BLOCK = 256
MOBA_TOPK = 3
WINDOW = 128
SWA_GROUP = 8
ROPE_THETA = 10000.0
EPS = 1e-6
NEG = -1e30
Q_SCALE = HEAD_DIM ** -0.5
MOBA_GROUP = 4
MOBA_PAIRS_PER_STEP = 2
SB_TILE = 256
SB_CHUNK = PAIR
SB_PAIRS_PER_STEP = 2
SB_UNDERFLOW = 104.0

VMEM_LIMIT_BYTES = 56 * 1024 * 1024

NT_DIMS = (((1,), (1,)), ((), ()))


def _params(*semantics):
    return pltpu.CompilerParams(dimension_semantics=semantics,
                                vmem_limit_bytes=VMEM_LIMIT_BYTES)


def _lane_index(shape, dtype=jnp.int32):
    idx = lax.broadcasted_iota(jnp.int32, shape, len(shape) - 1)
    return idx if dtype == jnp.int32 else idx.astype(F32).astype(dtype)


def _layout(d_model):
    sb = d_model // 4
    moba = d_model // 4
    swa_q = d_model // 2
    swa_kv = (swa_q // HEAD_DIM // SWA_GROUP) * HEAD_DIM
    sizes = (sb, sb, sb, moba, moba, moba, swa_q, swa_kv, swa_kv)
    offs = np.concatenate([[0], np.cumsum(sizes)]).astype(int)
    names = ("qa", "ka", "va", "qb", "kb", "vb", "qc", "kc", "vc")
    lay = {n: (int(offs[i]), int(sizes[i])) for i, n in enumerate(names)}
    lay["in_width"] = int(offs[-1])
    assert sb % PAIR == 0 and swa_q % PAIR == 0 and swa_kv == PAIR
    return lay


def _tiles(n_tokens, seq):
    tm = min(512, seq)
    assert seq % tm == 0 and n_tokens % tm == 0
    return dict(tm=tm, tm_mlp=min(1024, n_tokens), tn_in=256, tn_out=512, tf=512)


def _rope_tables(seq):
    inv_freq = ROPE_THETA ** (-jnp.arange(ROPE_HALF, dtype=F32) * 2.0 / HEAD_DIM)
    ang = jnp.arange(seq, dtype=F32)[:, None] * inv_freq[None, :]
    cos, sin = jnp.cos(ang), jnp.sin(ang)
    cos_t = jnp.tile(cos, (1, PAIR // ROPE_HALF))
    sin_t = jnp.tile(jnp.concatenate([-sin, sin], axis=1), (1, PAIR // HEAD_DIM))
    return cos_t, sin_t


def _in_proj_kernel(x_ref, g_ref, w_ref, cos_ref, sin_ref, wu_ref, wd_ref,
                    o_ref, wu_o_ref, wd_o_ref, h_ref, *, chunks):
    wu_o_ref[...] = wu_ref[...].astype(wu_o_ref.dtype)
    wd_o_ref[...] = wd_ref[...].astype(wd_o_ref.dtype)
    x = x_ref[...]
    ms = jnp.mean(x * x, axis=-1, keepdims=True)
    h_ref[...] = ((x * lax.rsqrt(ms + EPS)) * g_ref[...]).astype(BF16)
    tm = x.shape[0]
    lane = lax.broadcasted_iota(jnp.int32, (tm, PAIR), 1)
    first = (lane % HEAD_DIM) < ROPE_HALF
    for start, classes in chunks:
        width = len(classes) * PAIR
        acc = jnp.dot(h_ref[...], w_ref[:, start:start + width], preferred_element_type=F32)
        for t, (rope, scale) in enumerate(classes):
            a = acc[:, t * PAIR:(t + 1) * PAIR]
            if rope:
                partner = jnp.where(first, pltpu.roll(a, PAIR - ROPE_HALF, axis=1),
                                    pltpu.roll(a, ROPE_HALF, axis=1))
                a = a * cos_ref[...] + partner * sin_ref[...]
            if scale != 1.0:
                a = a * scale
            o_ref[:, start + t * PAIR:start + (t + 1) * PAIR] = a.astype(o_ref.dtype)


def _in_proj(xt, gain, w, layer, cos_t, sin_t, w_up, w_down, lay, seq, tiles):
    n_tok, d = xt.shape
    in_w = lay["in_width"]
    tm, tn = tiles["tm"], tiles["tn_in"]
    steps = n_tok // tm
    d_ff = w_up.shape[2]
    up_rows, down_rows = d // steps, d_ff // steps
    assert in_w % tn == 0 and tn % PAIR == 0
    assert d % steps == 0 and d_ff % steps == 0 and up_rows % BF16_SUBLANES == 0 and down_rows % BF16_SUBLANES == 0
    rope = np.zeros(in_w // PAIR, bool)
    scale = np.ones(in_w // PAIR, np.float32)
    for name in ("qb", "kb", "qc", "kc"):
        o, s = lay[name]
        rope[o // PAIR:(o + s) // PAIR] = True
    for name in ("qa", "qb", "qc"):
        o, s = lay[name]
        scale[o // PAIR:(o + s) // PAIR] = Q_SCALE
    per = tn // PAIR
    chunks = tuple((c * tn, tuple((bool(rope[c * per + t]), float(scale[c * per + t])) for t in range(per)))
                   for c in range(in_w // tn))
    pos_blocks = seq // tm
    return pl.pallas_call(
        functools.partial(_in_proj_kernel, chunks=chunks),
        grid=(steps,),
        in_specs=[
            pl.BlockSpec((tm, d), lambda i: (i, 0)),
            pl.BlockSpec((1, d), lambda i: (0, 0)),
            pl.BlockSpec((None, d, in_w), lambda i: (layer, 0, 0), pipeline_mode=pl.Buffered(1)),
            pl.BlockSpec((tm, PAIR), lambda i: (i % pos_blocks, 0)),
            pl.BlockSpec((tm, PAIR), lambda i: (i % pos_blocks, 0)),
            pl.BlockSpec((None, up_rows, d_ff), lambda i: (layer, i, 0)),
            pl.BlockSpec((None, down_rows, d), lambda i: (layer, i, 0)),
        ],
        out_specs=[
            pl.BlockSpec((tm, in_w), lambda i: (i, 0)),
            pl.BlockSpec((up_rows, d_ff), lambda i: (i, 0)),
            pl.BlockSpec((down_rows, d), lambda i: (i, 0)),
        ],
        out_shape=[
            jax.ShapeDtypeStruct((n_tok, in_w), BF16),
            jax.ShapeDtypeStruct((d, d_ff), BF16),
            jax.ShapeDtypeStruct((d_ff, d), BF16),
        ],
        scratch_shapes=[pltpu.VMEM((tm, d), BF16)],
        compiler_params=_params("arbitrary"),
        name="in_proj",
    )(xt, gain.reshape(1, d), w, cos_t, sin_t, w_up, w_down)


def _sb_kernel(q_ref, k_ref, v_ref, u_ref, o_ref, carry_ref, acc_ref, *, tq):
    i = pl.program_id(2)
    ch = SB_CHUNK
    n_pairs = q_ref.shape[2] // PAIR

    def lanes(p):
        return slice(p * PAIR, (p + 1) * PAIR)

    lo = _lane_index((tq, PAIR), BF16) < HEAD_DIM
    qs = []
    for p in range(n_pairs):
        q = q_ref[0, :, lanes(p)]
        zq = jnp.zeros_like(q)
        qs.append(jnp.concatenate([jnp.where(lo, q, zq), jnp.where(lo, zq, q)], axis=0))
    u = u_ref[...]
    row = lax.broadcasted_iota(jnp.int32, (2 * tq, tq), 0) % tq
    col = lax.broadcasted_iota(jnp.int32, (2 * tq, tq), 1)
    past = col < row

    def tiles(off, n_tiles, carries, accs, diagonal):
        nk = n_tiles * tq
        kblk = k_ref[0, pl.ds(off, nk), :]
        vblk = v_ref[0, pl.ds(off, nk), :]
        zs = [lax.dot_general(qs[p], kblk[:, lanes(p)], NT_DIMS, preferred_element_type=F32)
              for p in range(n_pairs)]
        sps = [jnp.maximum(z, 0.0) + jnp.log(1.0 + jnp.exp(-jnp.abs(z))) for z in zs]
        ws = [[None] * (nk // ch) for _ in range(n_pairs)]
        carries = list(carries)
        for c in reversed(range(nk // ch)):
            sl = slice(c * ch, (c + 1) * ch)
            masked = diagonal and c * ch >= nk - tq
            mask = past[:, c * ch - (nk - tq):(c + 1) * ch - (nk - tq)] if masked else None
            for p in range(n_pairs):
                s_c = jnp.where(mask, sps[p][:, sl], 0.0) if masked else sps[p][:, sl]
                hi = s_c.astype(BF16)
                lo_part = (s_c - hi.astype(F32)).astype(BF16)
                r = jnp.dot(jnp.concatenate([hi, lo_part], axis=1), u, preferred_element_type=F32)
                w = jnp.exp(zs[p][:, sl] - sps[p][:, sl] - r[:, :ch] - carries[p])
                if masked:
                    w = jnp.where(mask, w, 0.0)
                ws[p][c] = w.astype(BF16)
                carries[p] = carries[p] + r[:, ch:]
        lo_v = _lane_index((nk, PAIR), BF16) < HEAD_DIM
        accs = list(accs)
        for p in range(n_pairs):
            wb = jnp.concatenate(ws[p], axis=1)
            wcat = jnp.concatenate([wb[:tq], wb[tq:]], axis=1)
            vp = vblk[:, lanes(p)]
            zv = jnp.zeros_like(vp)
            vcat = jnp.concatenate([jnp.where(lo_v, vp, zv), jnp.where(lo_v, zv, vp)], axis=0)
            accs[p] = accs[p] + jnp.dot(wcat, vcat, preferred_element_type=F32)
        return carries, accs

    zero_carry = [jnp.zeros((2 * tq, PAIR), F32)] * n_pairs
    zero_acc = [jnp.zeros((tq, PAIR), F32)] * n_pairs

    def first_pass(off, n_tiles):
        carries, accs = tiles(off, n_tiles, zero_carry, zero_acc, True)
        for p in range(n_pairs):
            carry_ref[p] = carries[p]
            acc_ref[p] = accs[p]

    @pl.when(i == 0)
    def _():
        first_pass(0, 1)

    @pl.when(i > 0)
    def _():
        first_pass(pl.multiple_of((i - 1) * tq, tq), 2)

    def unfinished(carries):
        return functools.reduce(jnp.minimum, [jnp.min(c) for c in carries]) < SB_UNDERFLOW

    def cond(state):
        return (state[0] < i - 1) & state[1]

    def body(state):
        t = state[0]
        off = pl.multiple_of((i - 2 - t) * tq, tq)
        carries, accs = tiles(off, 1, state[2:2 + n_pairs], state[2 + n_pairs:], False)
        return (t + 1, unfinished(carries), *carries, *accs)

    carries = [carry_ref[p] for p in range(n_pairs)]
    accs = [acc_ref[p] for p in range(n_pairs)]
    state = lax.while_loop(cond, body, (jnp.int32(0), unfinished(carries), *carries, *accs))
    for p in range(n_pairs):
        o_ref[0, :, lanes(p)] = state[2 + n_pairs + p].astype(o_ref.dtype)


def _sb_attention(proj, lay):
    b, seq, _ = proj.shape
    tq = min(SB_TILE, seq)
    ch = SB_CHUNK
    assert seq % tq == 0 and tq % ch == 0
    q_off, width = lay["qa"]
    k_off, v_off = lay["ka"][0], lay["va"][0]
    n_pairs = SB_PAIRS_PER_STEP
    lw = n_pairs * PAIR
    assert width % lw == 0 and q_off % lw == 0 and k_off % lw == 0 and v_off % lw == 0
    tri = np.tril(np.ones((ch, ch), np.float32), -1)
    uu = np.concatenate([tri, np.ones((ch, PAIR), np.float32)], axis=1)
    uu = jnp.asarray(np.concatenate([uu, uu], axis=0), dtype=BF16)
    return pl.pallas_call(
        functools.partial(_sb_kernel, tq=tq),
        grid=(b, width // lw, seq // tq),
        in_specs=[
            pl.BlockSpec((1, tq, lw), lambda bi, p, i: (bi, i, q_off // lw + p)),
            pl.BlockSpec((1, seq, lw), lambda bi, p, i: (bi, 0, k_off // lw + p)),
            pl.BlockSpec((1, seq, lw), lambda bi, p, i: (bi, 0, v_off // lw + p)),
            pl.BlockSpec((2 * ch, ch + PAIR), lambda bi, p, i: (0, 0)),
        ],
        out_specs=pl.BlockSpec((1, tq, lw), lambda bi, p, i: (bi, i, p)),
        out_shape=jax.ShapeDtypeStruct((b, seq, width), BF16),
        scratch_shapes=[pltpu.VMEM((n_pairs, 2 * tq, PAIR), F32), pltpu.VMEM((n_pairs, tq, PAIR), F32)],
        compiler_params=_params("parallel", "parallel", "arbitrary"),
        name="sb_attention",
    )(proj, proj, proj, uu)


def _moba_kernel(q_ref, k_ref, v_ref, o_ref, kmean_ref, qaug_ref, m_ref, acc_ref, *, nb, nbp, group):
    i = pl.program_id(2)
    blk = MOBA_BLOCK
    tq = blk
    n_pairs = q_ref.shape[2] // PAIR

    def lanes(p):
        return slice(p * PAIR, (p + 1) * PAIR)

    @pl.when(i == 0)
    def _():
        kf = k_ref[0].astype(F32).reshape(nb, blk, n_pairs * PAIR)
        km = jnp.sum(kf, axis=1) * (1.0 / blk)
        if nbp > nb:
            km = jnp.concatenate([km, jnp.zeros((nbp - nb, n_pairs * PAIR), F32)], axis=0)
        kmean_ref[...] = km

    lo_qb = _lane_index((tq, PAIR), BF16) < HEAD_DIM
    lo_m = _lane_index((nbp, PAIR)) < HEAD_DIM
    jidx = lax.broadcasted_iota(jnp.int32, (nbp, tq), 0)
    valid = jidx < i

    def plain_queries(q):
        zq = jnp.zeros_like(q)
        return [jnp.where(lo_qb, q, zq), jnp.where(lo_qb, zq, q)]

    def gated_queries(q, km):
        aug = []
        for h in (0, 1):
            head_m = lo_m if h == 0 else jnp.logical_not(lo_m)
            head_q = lo_qb if h == 0 else jnp.logical_not(lo_qb)
            kmh = jnp.where(head_m, km, 0.0)
            a = kmh.astype(BF16)
            r1 = kmh - a.astype(F32)
            b2 = r1.astype(BF16)
            c3 = (r1 - b2.astype(F32)).astype(BF16)
            g3 = lax.dot_general(jnp.concatenate([a, b2, c3], axis=0), q, NT_DIMS,
                                 preferred_element_type=F32)
            gate = g3[:nbp] + g3[nbp:2 * nbp] + g3[2 * nbp:]
            gate = jnp.where(valid, gate, -jnp.inf)
            beaten_by = jnp.zeros((nbp, tq), jnp.int32)
            for jp in range(nb):
                other = gate[jp:jp + 1, :]
                beats = (other > gate) | ((other == gate) & (jidx > jp))
                beaten_by = beaten_by + beats.astype(jnp.int32)
            sel = valid & (beaten_by < MOBA_TOPK)
            bias_t = jnp.where(sel, 0.0, NEG)
            top = HEAD_DIM if h == 0 else 0
            pieces = [jnp.zeros((top, tq), F32)] if top else []
            pieces.append(bias_t)
            if PAIR - top - nbp:
                pieces.append(jnp.zeros((PAIR - top - nbp, tq), F32))
            placed = jnp.concatenate(pieces, axis=0).T
            aug.append(jnp.where(head_q, q, placed.astype(BF16)))
        return aug

    lo_q = _lane_index((tq, PAIR)) < HEAD_DIM
    lo_q2 = jnp.concatenate([lo_q, lo_q], axis=1)

    def attend(s0, s1, vblk, m0, m1, acc):
        n0 = jnp.maximum(m0, jnp.max(s0, axis=1, keepdims=True))
        n1 = jnp.maximum(m1, jnp.max(s1, axis=1, keepdims=True))
        p0 = jnp.exp(s0 - n0).astype(BF16)
        p1 = jnp.exp(s1 - n1).astype(BF16)
        alpha = jnp.where(lo_q, jnp.exp(m0 - n0), jnp.exp(m1 - n1))
        rhs = jnp.concatenate([vblk, jnp.ones_like(vblk)], axis=1)
        u0 = jnp.dot(p0, rhs, preferred_element_type=F32)
        u1 = jnp.dot(p1, rhs, preferred_element_type=F32)
        acc = acc * jnp.concatenate([alpha, alpha], axis=1) + jnp.where(lo_q2, u0, u1)
        return n0, n1, acc

    def own_block(q_plain):
        own_off = pl.multiple_of(i * blk, blk)
        k_own = k_ref[0, pl.ds(own_off, blk), :]
        v_own = v_ref[0, pl.ds(own_off, blk), :]
        row = lax.broadcasted_iota(jnp.int32, (tq, blk), 0)
        col = lax.broadcasted_iota(jnp.int32, (tq, blk), 1)
        causal = col <= row
        m_init = jnp.full((tq, 1), NEG, F32)
        state = []
        for p in range(n_pairs):
            s_own = [jnp.where(causal, lax.dot_general(qh, k_own[:, lanes(p)], NT_DIMS,
                                                       preferred_element_type=F32), NEG)
                     for qh in q_plain[p]]
            state += attend(s_own[0], s_own[1], v_own[:, lanes(p)],
                            m_init, m_init, jnp.zeros((tq, 2 * PAIR), F32))
        return tuple(state)

    gk = group * blk
    lane_g = _lane_index((gk, PAIR), BF16)
    lo_g = lane_g < HEAD_DIM
    blk_in_group = (lax.broadcasted_iota(jnp.int32, (gk, PAIR), 0) // blk).astype(F32).astype(BF16)
    one = jnp.ones((gk, PAIR), BF16)
    zk = jnp.zeros((gk, PAIR), BF16)

    def earlier_group(g, st, q_aug):
        o = pl.multiple_of(g * gk, gk)
        kg = k_ref[0, pl.ds(o, gk), :]
        vg = v_ref[0, pl.ds(o, gk), :]
        first = jnp.full((1, PAIR), g * group, jnp.int32).astype(F32).astype(BF16)
        blk_id = blk_in_group + first
        ind0 = jnp.where(lane_g == blk_id + HEAD_DIM, one, zk)
        ind1 = jnp.where(lane_g == blk_id, one, zk)
        scores = []
        for p in range(n_pairs):
            kp = kg[:, lanes(p)]
            scores.append((
                lax.dot_general(q_aug[p][0], jnp.where(lo_g, kp, ind0), NT_DIMS, preferred_element_type=F32),
                lax.dot_general(q_aug[p][1], jnp.where(lo_g, ind1, kp), NT_DIMS, preferred_element_type=F32)))
        new = []
        for p in range(n_pairs):
            new += attend(scores[p][0], scores[p][1], vg[:, lanes(p)], *st[3 * p:3 * p + 3])
        return tuple(new)

    def save(state):
        for p in range(n_pairs):
            m_ref[2 * p], m_ref[2 * p + 1], acc_ref[p] = state[3 * p:3 * p + 3]

    @pl.when(i == 0)
    def _():
        save(own_block([plain_queries(q_ref[0, :, lanes(p)]) for p in range(n_pairs)]))

    @pl.when(i > 0)
    def _():
        q_plain, q_aug = [], []
        for p in range(n_pairs):
            q = q_ref[0, :, lanes(p)]
            q_plain.append(plain_queries(q))
            q_aug.append(gated_queries(q, kmean_ref[:, lanes(p)]))
            qaug_ref[2 * p], qaug_ref[2 * p + 1] = q_aug[p]
        save(earlier_group(0, own_block(q_plain), q_aug))

    q_aug = [[qaug_ref[2 * p], qaug_ref[2 * p + 1]] for p in range(n_pairs)]
    state = []
    for p in range(n_pairs):
        state += [m_ref[2 * p], m_ref[2 * p + 1], acc_ref[p]]
    state = lax.fori_loop(1, (i + group - 1) // group,
                          lambda g, st: earlier_group(g, st, q_aug), tuple(state))
    for p in range(n_pairs):
        acc = state[3 * p + 2]
        o_ref[0, :, lanes(p)] = (acc[:, :PAIR] / acc[:, PAIR:]).astype(o_ref.dtype)


def _moba_attention(proj, lay):
    b, seq, _ = proj.shape
    tq = MOBA_BLOCK
    assert seq % MOBA_BLOCK == 0
    nb = seq // MOBA_BLOCK
    nbp = -(-nb // 8) * 8
    group = min(MOBA_GROUP, nb)
    assert nbp <= HEAD_DIM
    assert nb % group == 0
    q_off, width = lay["qb"]
    k_off, v_off = lay["kb"][0], lay["vb"][0]
    lw = MOBA_PAIRS_PER_STEP * PAIR
    assert width % lw == 0 and q_off % lw == 0 and k_off % lw == 0 and v_off % lw == 0
    return pl.pallas_call(
        functools.partial(_moba_kernel, nb=nb, nbp=nbp, group=group),
        grid=(b, width // lw, seq // tq),
        in_specs=[
            pl.BlockSpec((1, tq, lw), lambda bi, p, i: (bi, i, q_off // lw + p)),
            pl.BlockSpec((1, seq, lw), lambda bi, p, i: (bi, 0, k_off // lw + p)),
            pl.BlockSpec((1, seq, lw), lambda bi, p, i: (bi, 0, v_off // lw + p)),
        ],
        out_specs=pl.BlockSpec((1, tq, lw), lambda bi, p, i: (bi, i, p)),
        out_shape=jax.ShapeDtypeStruct((b, seq, width), BF16),
        scratch_shapes=[pltpu.VMEM((nbp, lw), F32),
                        pltpu.VMEM((2 * lw // PAIR, tq, PAIR), BF16),
                        pltpu.VMEM((2 * lw // PAIR, tq, 1), F32),
                        pltpu.VMEM((lw // PAIR, tq, 2 * PAIR), F32)],
        compiler_params=_params("parallel", "parallel", "arbitrary"),
        name="moba_attention",
    )(proj, proj, proj)


def _swa_kernel(sink_ref, q_ref, kp_ref, kc_ref, vp_ref, vc_ref, o_ref, *, n_pairs):
    n = pl.program_id(1)
    w = WINDOW
    k = jnp.concatenate([kp_ref[0], kc_ref[0]], axis=0).astype(F32)
    v = jnp.concatenate([vp_ref[0], vc_ref[0]], axis=0).astype(F32)
    lo_k = lax.broadcasted_iota(jnp.int32, (2 * w, PAIR), 1) < HEAD_DIM
    k_sw = pltpu.roll(k, HEAD_DIM, axis=1)
    v_sw = pltpu.roll(v, HEAD_DIM, axis=1)
    kk = [jnp.where(lo_k, k, k_sw).astype(BF16), jnp.where(lo_k, k_sw, k).astype(BF16)]
    vv = [jnp.where(lo_k, v, v_sw).astype(BF16), jnp.where(lo_k, v_sw, v).astype(BF16)]
    ones = jnp.ones((2 * w, PAIR), BF16)
    rhs = [jnp.concatenate([vg, ones], axis=1) for vg in vv]

    row = lax.broadcasted_iota(jnp.int32, (w, 2 * w), 0)
    col = lax.broadcasted_iota(jnp.int32, (w, 2 * w), 1)
    delta = row + w - col
    valid = (delta >= 0) & (delta < w) & ((n - 1) * w + col >= 0)
    lo_q = _lane_index((w, PAIR)) < HEAD_DIM
    lo_qb = _lane_index((w, PAIR), BF16) < HEAD_DIM
    pairs_per_kv = SWA_GROUP // 2

    for p in range(n_pairs):
        g = p // pairs_per_kv
        qp = q_ref[0, :, p * PAIR:(p + 1) * PAIR]
        zq = jnp.zeros_like(qp)
        outs = []
        for h in (0, 1):
            qh = jnp.where(lo_qb, qp, zq) if h == 0 else jnp.where(lo_qb, zq, qp)
            s = lax.dot_general(qh, kk[g], NT_DIMS, preferred_element_type=F32)
            s = jnp.where(valid, s, NEG)
            sink = sink_ref[2 * p + h]
            m = jnp.maximum(jnp.max(s, axis=1, keepdims=True), sink)
            pr = jnp.exp(s - m).astype(BF16)
            o2 = jnp.dot(pr, rhs[g], preferred_element_type=F32)
            outs.append(o2[:, :PAIR] / (o2[:, PAIR:] + jnp.exp(sink - m)))
        o_ref[0, :, p * PAIR:(p + 1) * PAIR] = jnp.where(lo_q, outs[0], outs[1]).astype(o_ref.dtype)


def _swa_attention(proj, sinks, lay):
    b, seq, _ = proj.shape
    w = WINDOW
    q_off, width = lay["qc"]
    k_off, v_off = lay["kc"][0], lay["vc"][0]
    assert q_off % width == 0 and seq % w == 0
    return pl.pallas_call(
        functools.partial(_swa_kernel, n_pairs=width // PAIR),
        grid=(b, seq // w),
        in_specs=[
            pl.BlockSpec(memory_space=pltpu.SMEM),
            pl.BlockSpec((1, w, width), lambda bi, n: (bi, n, q_off // width)),
            pl.BlockSpec((1, w, PAIR), lambda bi, n: (bi, jnp.maximum(n - 1, 0), k_off // PAIR)),
            pl.BlockSpec((1, w, PAIR), lambda bi, n: (bi, n, k_off // PAIR)),
            pl.BlockSpec((1, w, PAIR), lambda bi, n: (bi, jnp.maximum(n - 1, 0), v_off // PAIR)),
            pl.BlockSpec((1, w, PAIR), lambda bi, n: (bi, n, v_off // PAIR)),
        ],
        out_specs=pl.BlockSpec((1, w, width), lambda bi, n: (bi, n, 0)),
        out_shape=jax.ShapeDtypeStruct((b, seq, width), BF16),
        compiler_params=_params("parallel", "arbitrary"),
        name="swa_attention",
    )(sinks.astype(F32), proj, proj, proj, proj, proj)


def _out_proj_kernel(ya_ref, yb_ref, yc_ref, ga_ref, gb_ref, gc_ref, w_ref, x_ref, o_ref, mix_ref,
                     *, tn):
    start = 0
    for y_ref, g_ref in ((ya_ref, ga_ref), (yb_ref, gb_ref), (yc_ref, gc_ref)):
        y = y_ref[...].astype(F32)
        ms = jnp.mean(y * y, axis=-1, keepdims=True)
        width = y.shape[1]
        mix_ref[:, start:start + width] = ((y * lax.rsqrt(ms + EPS)) * g_ref[...]).astype(BF16)
        start += width
    for c in range(o_ref.shape[1] // tn):
        cols = slice(c * tn, (c + 1) * tn)
        o_ref[:, cols] = x_ref[:, cols] + jnp.dot(mix_ref[...], w_ref[:, cols],
                                                  preferred_element_type=F32)


def _out_proj(ya, yb, yc, ga, gb, gc, w, layer, xt, tiles):
    n_tok, d = xt.shape
    tm, tn = tiles["tm"], tiles["tn_out"]
    wa, wb, wc = ya.shape[1], yb.shape[1], yc.shape[1]
    mix_w = wa + wb + wc
    assert d % tn == 0 and w.shape[1:] == (mix_w, d)
    return pl.pallas_call(
        functools.partial(_out_proj_kernel, tn=tn),
        grid=(n_tok // tm,),
        in_specs=[
            pl.BlockSpec((tm, wa), lambda i: (i, 0)),
            pl.BlockSpec((tm, wb), lambda i: (i, 0)),
            pl.BlockSpec((tm, wc), lambda i: (i, 0)),
            pl.BlockSpec((1, wa), lambda i: (0, 0)),
            pl.BlockSpec((1, wb), lambda i: (0, 0)),
            pl.BlockSpec((1, wc), lambda i: (0, 0)),
            pl.BlockSpec((None, mix_w, d), lambda i: (layer, 0, 0), pipeline_mode=pl.Buffered(1)),
            pl.BlockSpec((tm, d), lambda i: (i, 0)),
        ],
        out_specs=pl.BlockSpec((tm, d), lambda i: (i, 0)),
        out_shape=jax.ShapeDtypeStruct((n_tok, d), F32),
        scratch_shapes=[pltpu.VMEM((tm, mix_w), BF16)],
        compiler_params=_params("arbitrary"),
        name="out_proj",
    )(ya, yb, yc, ga.reshape(1, wa), gb.reshape(1, wb), gc.reshape(1, wc), w, xt)


def _mlp_kernel(x_ref, g_ref, wu_ref, wd_ref, gf_ref, o_ref, h_ref, *, final):
    f = pl.program_id(1)

    @pl.when(f == 0)
    def _():
        x = x_ref[...]
        ms = jnp.mean(x * x, axis=-1, keepdims=True)
        h_ref[...] = ((x * lax.rsqrt(ms + EPS)) * g_ref[...]).astype(BF16)
        o_ref[...] = x

    u = jnp.maximum(jnp.dot(h_ref[...], wu_ref[...], preferred_element_type=F32), 0.0)
    o_ref[...] += jnp.dot((u * u).astype(BF16), wd_ref[...], preferred_element_type=F32)

    if final:
        @pl.when(f == pl.num_programs(1) - 1)
        def _():
            y = o_ref[...]
            ms = jnp.mean(y * y, axis=-1, keepdims=True)
            o_ref[...] = (y * lax.rsqrt(ms + EPS)) * gf_ref[...]


def _mlp(xt, gain, w_up, w_down, final_gain, tiles, final):
    n_tok, d = xt.shape
    d_ff = w_up.shape[1]
    tm, tf = tiles["tm_mlp"], tiles["tf"]
    assert d_ff % tf == 0 and n_tok % tm == 0
    return pl.pallas_call(
        functools.partial(_mlp_kernel, final=final),
        grid=(n_tok // tm, d_ff // tf),
        in_specs=[
            pl.BlockSpec((tm, d), lambda i, f: (i, 0)),
            pl.BlockSpec((1, d), lambda i, f: (0, 0)),
            pl.BlockSpec((d, tf), lambda i, f: (0, f)),
            pl.BlockSpec((tf, d), lambda i, f: (f, 0)),
            pl.BlockSpec((1, d), lambda i, f: (0, 0)),
        ],
        out_specs=pl.BlockSpec((tm, d), lambda i, f: (i, 0)),
        out_shape=jax.ShapeDtypeStruct((n_tok, d), F32),
        scratch_shapes=[pltpu.VMEM((tm, d), BF16)],
        compiler_params=_params("parallel", "arbitrary"),
        name="mlp",
    )(xt, gain.reshape(1, d), w_up, w_down, final_gain.reshape(1, d))


def kernel(x, attn_norm, w_in, sinks, gn_sb, gn_moba, gn_swa, w_out, mlp_norm, w_up, w_down, final_norm):
    b, seq, d = x.shape
    depth = w_in.shape[0]
    n_tok = b * seq
    lay = _layout(d)
    tiles = _tiles(n_tok, seq)
    cos_t, sin_t = _rope_tables(seq)
    xt = x.reshape(n_tok, d)
    w_in, w_out = w_in.astype(BF16), w_out.astype(BF16)
    for l in range(depth):
        proj, w_up_l, w_down_l = _in_proj(xt, attn_norm[l], w_in, l, cos_t, sin_t, w_up, w_down,
                                          lay, seq, tiles)
        proj = proj.reshape(b, seq, lay["in_width"])
        ya = _sb_attention(proj, lay).reshape(n_tok, -1)
        yb = _moba_attention(proj, lay).reshape(n_tok, -1)
        yc = _swa_attention(proj, sinks[l], lay).reshape(n_tok, -1)
        xt = _out_proj(ya, yb, yc, gn_sb[l], gn_moba[l], gn_swa[l], w_out, l, xt, tiles)
        xt = _mlp(xt, mlp_norm[l], w_up_l, w_down_l, final_norm, tiles, final=(l == depth - 1))
    return xt.reshape(b, seq, d)
```

```python
import functools

import numpy as np
import jax
import jax.numpy as jnp
from jax import lax
from jax.experimental import pallas as pl
from jax.experimental.pallas import tpu as pltpu

F32 = jnp.float32
BF16 = jnp.bfloat16

HEAD_DIM = 64
PAIR = 2 * HEAD_DIM
BF16_SUBLANES = 16
ROPE_HALF = HEAD_DIM // 2
MOBA_BLOCK = 256
MOBA_TOPK = 3
WINDOW = 128
SWA_GROUP = 8
ROPE_THETA = 10000.0
EPS = 1e-6
NEG = -1e30
Q_SCALE = HEAD_DIM ** -0.5
MOBA_GROUP = 4
MOBA_PAIRS_PER_STEP = 2
SB_TILE = 256
SB_CHUNK = PAIR
SB_PAIRS_PER_STEP = 2
SB_UNDERFLOW = 104.0

VMEM_LIMIT_BYTES = 56 * 1024 * 1024

NT_DIMS = (((1,), (1,)), ((), ()))


def _params(*semantics):
    return pltpu.CompilerParams(dimension_semantics=semantics,
                                vmem_limit_bytes=VMEM_LIMIT_BYTES)


def _lane_index(shape, dtype=jnp.int32):
    idx = lax.broadcasted_iota(jnp.int32, shape, len(shape) - 1)
    return idx if dtype == jnp.int32 else idx.astype(F32).astype(dtype)


def _layout(d_model):
    sb = d_model // 4
    moba = d_model // 4
    swa_q = d_model // 2
    swa_kv = (swa_q // HEAD_DIM // SWA_GROUP) * HEAD_DIM
    sizes = (sb, sb, sb, moba, moba, moba, swa_q, swa_kv, swa_kv)
    offs = np.concatenate([[0], np.cumsum(sizes)]).astype(int)
    names = ("qa", "ka", "va", "qb", "kb", "vb", "qc", "kc", "vc")
    lay = {n: (int(offs[i]), int(sizes[i])) for i, n in enumerate(names)}
    lay["in_width"] = int(offs[-1])
    assert sb % PAIR == 0 and swa_q % PAIR == 0 and swa_kv == PAIR
    return lay


def _tiles(n_tokens, seq):
    tm = min(512, seq)
    assert seq % tm == 0 and n_tokens % tm == 0
    return dict(tm=tm, tm_mlp=min(1024, n_tokens), tn_in=256, tn_out=512, tf=512)


def _rope_tables(seq):
    inv_freq = ROPE_THETA ** (-jnp.arange(ROPE_HALF, dtype=F32) * 2.0 / HEAD_DIM)
    ang = jnp.arange(seq, dtype=F32)[:, None] * inv_freq[None, :]
    cos, sin = jnp.cos(ang), jnp.sin(ang)
    cos_t = jnp.tile(cos, (1, PAIR // ROPE_HALF))
    sin_t = jnp.tile(jnp.concatenate([-sin, sin], axis=1), (1, PAIR // HEAD_DIM))
    return cos_t, sin_t


def _cast_specs(weights, layer, steps):
    in_specs, out_specs, out_shapes = [], [], []
    for w in weights:
        _, rows, cols = w.shape
        slab = rows // steps
        assert rows % steps == 0 and slab % BF16_SUBLANES == 0
        in_specs.append(pl.BlockSpec((None, slab, cols), lambda i: (layer, i, 0)))
        out_specs.append(pl.BlockSpec((slab, cols), lambda i: (i, 0)))
        out_shapes.append(jax.ShapeDtypeStruct((rows, cols), BF16))
    return in_specs, out_specs, out_shapes


def _cast_slabs(src_refs, dst_refs):
    for src, dst in zip(src_refs, dst_refs):
        dst[...] = src[...].astype(dst.dtype)


def _in_proj_kernel(x_ref, g_ref, w_ref, cos_ref, sin_ref, *refs, chunks, n_cast):
    o_ref, h_ref = refs[n_cast], refs[-1]
    _cast_slabs(refs[:n_cast], refs[n_cast + 1:-1])
    x = x_ref[...]
    ms = jnp.mean(x * x, axis=-1, keepdims=True)
    h_ref[...] = ((x * lax.rsqrt(ms + EPS)) * g_ref[...]).astype(BF16)
    tm = x.shape[0]
    lane = lax.broadcasted_iota(jnp.int32, (tm, PAIR), 1)
    first = (lane % HEAD_DIM) < ROPE_HALF
    for start, classes in chunks:
        width = len(classes) * PAIR
        acc = jnp.dot(h_ref[...], w_ref[:, start:start + width], preferred_element_type=F32)
        for t, (rope, scale) in enumerate(classes):
            a = acc[:, t * PAIR:(t + 1) * PAIR]
            if rope:
                partner = jnp.where(first, pltpu.roll(a, PAIR - ROPE_HALF, axis=1),
                                    pltpu.roll(a, ROPE_HALF, axis=1))
                a = a * cos_ref[...] + partner * sin_ref[...]
            if scale != 1.0:
                a = a * scale
            o_ref[:, start + t * PAIR:start + (t + 1) * PAIR] = a.astype(o_ref.dtype)


def _in_proj(xt, gain, w, cos_t, sin_t, to_cast, layer, lay, seq, tiles):
    n_tok, d = xt.shape
    in_w = lay["in_width"]
    tm, tn = tiles["tm"], tiles["tn_in"]
    steps = n_tok // tm
    cast_in, cast_out, cast_shapes = _cast_specs(to_cast, layer, steps)
    assert in_w % tn == 0 and tn % PAIR == 0
    rope = np.zeros(in_w // PAIR, bool)
    scale = np.ones(in_w // PAIR, np.float32)
    for name in ("qb", "kb", "qc", "kc"):
        o, s = lay[name]
        rope[o // PAIR:(o + s) // PAIR] = True
    for name in ("qa", "qb", "qc"):
        o, s = lay[name]
        scale[o // PAIR:(o + s) // PAIR] = Q_SCALE
    per = tn // PAIR
    chunks = tuple((c * tn, tuple((bool(rope[c * per + t]), float(scale[c * per + t])) for t in range(per)))
                   for c in range(in_w // tn))
    pos_blocks = seq // tm
    return pl.pallas_call(
        functools.partial(_in_proj_kernel, chunks=chunks, n_cast=len(to_cast)),
        grid=(steps,),
        in_specs=[
            pl.BlockSpec((tm, d), lambda i: (i, 0)),
            pl.BlockSpec((1, d), lambda i: (0, 0)),
            pl.BlockSpec((d, in_w), lambda i: (0, 0), pipeline_mode=pl.Buffered(1)),
            pl.BlockSpec((tm, PAIR), lambda i: (i % pos_blocks, 0)),
            pl.BlockSpec((tm, PAIR), lambda i: (i % pos_blocks, 0)),
        ] + cast_in,
        out_specs=[pl.BlockSpec((tm, in_w), lambda i: (i, 0))] + cast_out,
        out_shape=[jax.ShapeDtypeStruct((n_tok, in_w), BF16)] + cast_shapes,
        scratch_shapes=[pltpu.VMEM((tm, d), BF16)],
        compiler_params=_params("arbitrary"),
        name="in_proj",
    )(xt, gain.reshape(1, d), w, cos_t, sin_t, *to_cast)


def _sb_kernel(q_ref, k_ref, v_ref, u_ref, o_ref, carry_ref, acc_ref, *, tq):
    i = pl.program_id(2)
    ch = SB_CHUNK
    n_pairs = q_ref.shape[2] // PAIR

    def lanes(p):
        return slice(p * PAIR, (p + 1) * PAIR)

    lo = _lane_index((tq, PAIR), BF16) < HEAD_DIM
    qs = []
    for p in range(n_pairs):
        q = q_ref[0, :, lanes(p)]
        zq = jnp.zeros_like(q)
        qs.append(jnp.concatenate([jnp.where(lo, q, zq), jnp.where(lo, zq, q)], axis=0))
    u = u_ref[...]
    row = lax.broadcasted_iota(jnp.int32, (2 * tq, tq), 0) % tq
    col = lax.broadcasted_iota(jnp.int32, (2 * tq, tq), 1)
    past = col < row

    def tiles(off, n_tiles, carries, accs, diagonal):
        nk = n_tiles * tq
        kblk = k_ref[0, pl.ds(off, nk), :]
        vblk = v_ref[0, pl.ds(off, nk), :]
        zs = [lax.dot_general(qs[p], kblk[:, lanes(p)], NT_DIMS, preferred_element_type=F32)
              for p in range(n_pairs)]
        sps = [jnp.maximum(z, 0.0) + jnp.log(1.0 + jnp.exp(-jnp.abs(z))) for z in zs]
        ws = [[None] * (nk // ch) for _ in range(n_pairs)]
        carries = list(carries)
        for c in reversed(range(nk // ch)):
            sl = slice(c * ch, (c + 1) * ch)
            masked = diagonal and c * ch >= nk - tq
            mask = past[:, c * ch - (nk - tq):(c + 1) * ch - (nk - tq)] if masked else None
            for p in range(n_pairs):
                s_c = jnp.where(mask, sps[p][:, sl], 0.0) if masked else sps[p][:, sl]
                hi = s_c.astype(BF16)
                lo_part = (s_c - hi.astype(F32)).astype(BF16)
                r = jnp.dot(jnp.concatenate([hi, lo_part], axis=1), u, preferred_element_type=F32)
                w = jnp.exp(zs[p][:, sl] - sps[p][:, sl] - r[:, :ch] - carries[p])
                if masked:
                    w = jnp.where(mask, w, 0.0)
                ws[p][c] = w.astype(BF16)
                carries[p] = carries[p] + r[:, ch:]
        lo_v = _lane_index((nk, PAIR), BF16) < HEAD_DIM
        accs = list(accs)
        for p in range(n_pairs):
            wb = jnp.concatenate(ws[p], axis=1)
            wcat = jnp.concatenate([wb[:tq], wb[tq:]], axis=1)
            vp = vblk[:, lanes(p)]
            zv = jnp.zeros_like(vp)
            vcat = jnp.concatenate([jnp.where(lo_v, vp, zv), jnp.where(lo_v, zv, vp)], axis=0)
            accs[p] = accs[p] + jnp.dot(wcat, vcat, preferred_element_type=F32)
        return carries, accs

    zero_carry = [jnp.zeros((2 * tq, PAIR), F32)] * n_pairs
    zero_acc = [jnp.zeros((tq, PAIR), F32)] * n_pairs

    def first_pass(off, n_tiles):
        carries, accs = tiles(off, n_tiles, zero_carry, zero_acc, True)
        for p in range(n_pairs):
            carry_ref[p] = carries[p]
            acc_ref[p] = accs[p]

    @pl.when(i == 0)
    def _():
        first_pass(0, 1)

    @pl.when(i > 0)
    def _():
        first_pass(pl.multiple_of((i - 1) * tq, tq), 2)

    def unfinished(carries):
        return functools.reduce(jnp.minimum, [jnp.min(c) for c in carries]) < SB_UNDERFLOW

    def cond(state):
        return (state[0] < i - 1) & state[1]

    def body(state):
        t = state[0]
        off = pl.multiple_of((i - 2 - t) * tq, tq)
        carries, accs = tiles(off, 1, state[2:2 + n_pairs], state[2 + n_pairs:], False)
        return (t + 1, unfinished(carries), *carries, *accs)

    carries = [carry_ref[p] for p in range(n_pairs)]
    accs = [acc_ref[p] for p in range(n_pairs)]
    state = lax.while_loop(cond, body, (jnp.int32(0), unfinished(carries), *carries, *accs))
    for p in range(n_pairs):
        o_ref[0, :, lanes(p)] = state[2 + n_pairs + p].astype(o_ref.dtype)


def _sb_attention(proj, lay):
    b, seq, _ = proj.shape
    tq = min(SB_TILE, seq)
    ch = SB_CHUNK
    assert seq % tq == 0 and tq % ch == 0
    q_off, width = lay["qa"]
    k_off, v_off = lay["ka"][0], lay["va"][0]
    n_pairs = SB_PAIRS_PER_STEP
    lw = n_pairs * PAIR
    assert width % lw == 0 and q_off % lw == 0 and k_off % lw == 0 and v_off % lw == 0
    tri = np.tril(np.ones((ch, ch), np.float32), -1)
    uu = np.concatenate([tri, np.ones((ch, PAIR), np.float32)], axis=1)
    uu = jnp.asarray(np.concatenate([uu, uu], axis=0), dtype=BF16)
    return pl.pallas_call(
        functools.partial(_sb_kernel, tq=tq),
        grid=(b, width // lw, seq // tq),
        in_specs=[
            pl.BlockSpec((1, tq, lw), lambda bi, p, i: (bi, i, q_off // lw + p)),
            pl.BlockSpec((1, seq, lw), lambda bi, p, i: (bi, 0, k_off // lw + p)),
            pl.BlockSpec((1, seq, lw), lambda bi, p, i: (bi, 0, v_off // lw + p)),
            pl.BlockSpec((2 * ch, ch + PAIR), lambda bi, p, i: (0, 0)),
        ],
        out_specs=pl.BlockSpec((1, tq, lw), lambda bi, p, i: (bi, i, p)),
        out_shape=jax.ShapeDtypeStruct((b, seq, width), BF16),
        scratch_shapes=[pltpu.VMEM((n_pairs, 2 * tq, PAIR), F32), pltpu.VMEM((n_pairs, tq, PAIR), F32)],
        compiler_params=_params("parallel", "parallel", "arbitrary"),
        name="sb_attention",
    )(proj, proj, proj, uu)


def _moba_kernel(q_ref, k_ref, v_ref, o_ref, kmean_ref, *, nb, nbp, group):
    i = pl.program_id(2)
    blk = MOBA_BLOCK
    tq = blk
    n_pairs = q_ref.shape[2] // PAIR

    def lanes(p):
        return slice(p * PAIR, (p + 1) * PAIR)

    @pl.when(i == 0)
    def _():
        kf = k_ref[0].astype(F32).reshape(nb, blk, n_pairs * PAIR)
        km = jnp.sum(kf, axis=1) * (1.0 / blk)
        if nbp > nb:
            km = jnp.concatenate([km, jnp.zeros((nbp - nb, n_pairs * PAIR), F32)], axis=0)
        kmean_ref[...] = km

    lo_qb = _lane_index((tq, PAIR), BF16) < HEAD_DIM
    lo_m = _lane_index((nbp, PAIR)) < HEAD_DIM
    jidx = lax.broadcasted_iota(jnp.int32, (nbp, tq), 0)
    valid = jidx < i

    def gated_queries(q, km):
        zq = jnp.zeros_like(q)
        plain, aug = [], []
        for h in (0, 1):
            head_m = lo_m if h == 0 else jnp.logical_not(lo_m)
            head_q = lo_qb if h == 0 else jnp.logical_not(lo_qb)
            kmh = jnp.where(head_m, km, 0.0)
            a = kmh.astype(BF16)
            r1 = kmh - a.astype(F32)
            b2 = r1.astype(BF16)
            c3 = (r1 - b2.astype(F32)).astype(BF16)
            g3 = lax.dot_general(jnp.concatenate([a, b2, c3], axis=0), q, NT_DIMS,
                                 preferred_element_type=F32)
            gate = g3[:nbp] + g3[nbp:2 * nbp] + g3[2 * nbp:]
            gate = jnp.where(valid, gate, -jnp.inf)
            beaten_by = jnp.zeros((nbp, tq), jnp.int32)
            for jp in range(nb):
                other = gate[jp:jp + 1, :]
                beats = (other > gate) | ((other == gate) & (jidx > jp))
                beaten_by = beaten_by + beats.astype(jnp.int32)
            sel = valid & (beaten_by < MOBA_TOPK)
            bias_t = jnp.where(sel, 0.0, NEG)
            top = HEAD_DIM if h == 0 else 0
            pieces = [jnp.zeros((top, tq), F32)] if top else []
            pieces.append(bias_t)
            if PAIR - top - nbp:
                pieces.append(jnp.zeros((PAIR - top - nbp, tq), F32))
            placed = jnp.concatenate(pieces, axis=0).T
            plain.append(jnp.where(head_q, q, zq))
            aug.append(jnp.where(head_q, q, placed.astype(BF16)))
        return plain, aug

    q_plain, q_aug = [], []
    for p in range(n_pairs):
        plain, aug = gated_queries(q_ref[0, :, lanes(p)], kmean_ref[:, lanes(p)])
        q_plain.append(plain)
        q_aug.append(aug)

    lo_q = _lane_index((tq, PAIR)) < HEAD_DIM
    lo_q2 = jnp.concatenate([lo_q, lo_q], axis=1)

    def attend(s0, s1, vblk, m0, m1, acc):
        n0 = jnp.maximum(m0, jnp.max(s0, axis=1, keepdims=True))
        n1 = jnp.maximum(m1, jnp.max(s1, axis=1, keepdims=True))
        p0 = jnp.exp(s0 - n0).astype(BF16)
        p1 = jnp.exp(s1 - n1).astype(BF16)
        alpha = jnp.where(lo_q, jnp.exp(m0 - n0), jnp.exp(m1 - n1))
        rhs = jnp.concatenate([vblk, jnp.ones_like(vblk)], axis=1)
        u0 = jnp.dot(p0, rhs, preferred_element_type=F32)
        u1 = jnp.dot(p1, rhs, preferred_element_type=F32)
        acc = acc * jnp.concatenate([alpha, alpha], axis=1) + jnp.where(lo_q2, u0, u1)
        return n0, n1, acc

    own_off = pl.multiple_of(i * blk, blk)
    k_own = k_ref[0, pl.ds(own_off, blk), :]
    v_own = v_ref[0, pl.ds(own_off, blk), :]
    row = lax.broadcasted_iota(jnp.int32, (tq, blk), 0)
    col = lax.broadcasted_iota(jnp.int32, (tq, blk), 1)
    causal = col <= row
    m_init = jnp.full((tq, 1), NEG, F32)
    state = []
    for p in range(n_pairs):
        s_own = [jnp.where(causal, lax.dot_general(qh, k_own[:, lanes(p)], NT_DIMS,
                                                   preferred_element_type=F32), NEG)
                 for qh in q_plain[p]]
        state += attend(s_own[0], s_own[1], v_own[:, lanes(p)],
                        m_init, m_init, jnp.zeros((tq, 2 * PAIR), F32))

    gk = group * blk
    lane_g = _lane_index((gk, PAIR), BF16)
    lo_g = lane_g < HEAD_DIM
    blk_in_group = (lax.broadcasted_iota(jnp.int32, (gk, PAIR), 0) // blk).astype(F32).astype(BF16)
    one = jnp.ones((gk, PAIR), BF16)
    zk = jnp.zeros((gk, PAIR), BF16)

    def body(g, st):
        o = pl.multiple_of(g * gk, gk)
        kg = k_ref[0, pl.ds(o, gk), :]
        vg = v_ref[0, pl.ds(o, gk), :]
        first = jnp.full((1, PAIR), g * group, jnp.int32).astype(F32).astype(BF16)
        blk_id = blk_in_group + first
        ind0 = jnp.where(lane_g == blk_id + HEAD_DIM, one, zk)
        ind1 = jnp.where(lane_g == blk_id, one, zk)
        scores = []
        for p in range(n_pairs):
            kp = kg[:, lanes(p)]
            scores.append((
                lax.dot_general(q_aug[p][0], jnp.where(lo_g, kp, ind0), NT_DIMS, preferred_element_type=F32),
                lax.dot_general(q_aug[p][1], jnp.where(lo_g, ind1, kp), NT_DIMS, preferred_element_type=F32)))
        new = []
        for p in range(n_pairs):
            new += attend(scores[p][0], scores[p][1], vg[:, lanes(p)], *st[3 * p:3 * p + 3])
        return tuple(new)

    state = lax.fori_loop(0, (i + group - 1) // group, body, tuple(state))
    for p in range(n_pairs):
        acc = state[3 * p + 2]
        o_ref[0, :, lanes(p)] = (acc[:, :PAIR] / acc[:, PAIR:]).astype(o_ref.dtype)


def _moba_attention(proj, lay):
    b, seq, _ = proj.shape
    tq = MOBA_BLOCK
    assert seq % MOBA_BLOCK == 0
    nb = seq // MOBA_BLOCK
    nbp = -(-nb // 8) * 8
    group = min(MOBA_GROUP, nb)
    assert nbp <= HEAD_DIM
    assert nb % group == 0
    q_off, width = lay["qb"]
    k_off, v_off = lay["kb"][0], lay["vb"][0]
    lw = MOBA_PAIRS_PER_STEP * PAIR
    assert width % lw == 0 and q_off % lw == 0 and k_off % lw == 0 and v_off % lw == 0
    return pl.pallas_call(
        functools.partial(_moba_kernel, nb=nb, nbp=nbp, group=group),
        grid=(b, width // lw, seq // tq),
        in_specs=[
            pl.BlockSpec((1, tq, lw), lambda bi, p, i: (bi, i, q_off // lw + p)),
            pl.BlockSpec((1, seq, lw), lambda bi, p, i: (bi, 0, k_off // lw + p)),
            pl.BlockSpec((1, seq, lw), lambda bi, p, i: (bi, 0, v_off // lw + p)),
        ],
        out_specs=pl.BlockSpec((1, tq, lw), lambda bi, p, i: (bi, i, p)),
        out_shape=jax.ShapeDtypeStruct((b, seq, width), BF16),
        scratch_shapes=[pltpu.VMEM((nbp, lw), F32)],
        compiler_params=_params("parallel", "parallel", "arbitrary"),
        name="moba_attention",
    )(proj, proj, proj)


def _swa_kernel(sink_ref, q_ref, kp_ref, kc_ref, vp_ref, vc_ref, o_ref, *, n_pairs):
    n = pl.program_id(1)
    w = WINDOW
    k = jnp.concatenate([kp_ref[0], kc_ref[0]], axis=0).astype(F32)
    v = jnp.concatenate([vp_ref[0], vc_ref[0]], axis=0).astype(F32)
    lo_k = lax.broadcasted_iota(jnp.int32, (2 * w, PAIR), 1) < HEAD_DIM
    k_sw = pltpu.roll(k, HEAD_DIM, axis=1)
    v_sw = pltpu.roll(v, HEAD_DIM, axis=1)
    kk = [jnp.where(lo_k, k, k_sw).astype(BF16), jnp.where(lo_k, k_sw, k).astype(BF16)]
    vv = [jnp.where(lo_k, v, v_sw).astype(BF16), jnp.where(lo_k, v_sw, v).astype(BF16)]
    ones = jnp.ones((2 * w, PAIR), BF16)
    rhs = [jnp.concatenate([vg, ones], axis=1) for vg in vv]

    row = lax.broadcasted_iota(jnp.int32, (w, 2 * w), 0)
    col = lax.broadcasted_iota(jnp.int32, (w, 2 * w), 1)
    delta = row + w - col
    valid = (delta >= 0) & (delta < w) & ((n - 1) * w + col >= 0)
    lo_q = _lane_index((w, PAIR)) < HEAD_DIM
    lo_qb = _lane_index((w, PAIR), BF16) < HEAD_DIM
    pairs_per_kv = SWA_GROUP // 2

    for p in range(n_pairs):
        g = p // pairs_per_kv
        qp = q_ref[0, :, p * PAIR:(p + 1) * PAIR]
        zq = jnp.zeros_like(qp)
        outs = []
        for h in (0, 1):
            qh = jnp.where(lo_qb, qp, zq) if h == 0 else jnp.where(lo_qb, zq, qp)
            s = lax.dot_general(qh, kk[g], NT_DIMS, preferred_element_type=F32)
            s = jnp.where(valid, s, NEG)
            sink = sink_ref[2 * p + h]
            m = jnp.maximum(jnp.max(s, axis=1, keepdims=True), sink)
            pr = jnp.exp(s - m).astype(BF16)
            o2 = jnp.dot(pr, rhs[g], preferred_element_type=F32)
            outs.append(o2[:, :PAIR] / (o2[:, PAIR:] + jnp.exp(sink - m)))
        o_ref[0, :, p * PAIR:(p + 1) * PAIR] = jnp.where(lo_q, outs[0], outs[1]).astype(o_ref.dtype)


def _swa_attention(proj, sinks, lay):
    b, seq, _ = proj.shape
    w = WINDOW
    q_off, width = lay["qc"]
    k_off, v_off = lay["kc"][0], lay["vc"][0]
    assert q_off % width == 0 and seq % w == 0
    return pl.pallas_call(
        functools.partial(_swa_kernel, n_pairs=width // PAIR),
        grid=(b, seq // w),
        in_specs=[
            pl.BlockSpec(memory_space=pltpu.SMEM),
            pl.BlockSpec((1, w, width), lambda bi, n: (bi, n, q_off // width)),
            pl.BlockSpec((1, w, PAIR), lambda bi, n: (bi, jnp.maximum(n - 1, 0), k_off // PAIR)),
            pl.BlockSpec((1, w, PAIR), lambda bi, n: (bi, n, k_off // PAIR)),
            pl.BlockSpec((1, w, PAIR), lambda bi, n: (bi, jnp.maximum(n - 1, 0), v_off // PAIR)),
            pl.BlockSpec((1, w, PAIR), lambda bi, n: (bi, n, v_off // PAIR)),
        ],
        out_specs=pl.BlockSpec((1, w, width), lambda bi, n: (bi, n, 0)),
        out_shape=jax.ShapeDtypeStruct((b, seq, width), BF16),
        compiler_params=_params("parallel", "arbitrary"),
        name="swa_attention",
    )(sinks.astype(F32), proj, proj, proj, proj, proj)


def _out_proj_kernel(ya_ref, yb_ref, yc_ref, ga_ref, gb_ref, gc_ref, w_ref, x_ref, *refs, tn, n_cast):
    o_ref, mix_ref = refs[n_cast], refs[-1]
    _cast_slabs(refs[:n_cast], refs[n_cast + 1:-1])
    start = 0
    for y_ref, g_ref in ((ya_ref, ga_ref), (yb_ref, gb_ref), (yc_ref, gc_ref)):
        y = y_ref[...].astype(F32)
        ms = jnp.mean(y * y, axis=-1, keepdims=True)
        width = y.shape[1]
        mix_ref[:, start:start + width] = ((y * lax.rsqrt(ms + EPS)) * g_ref[...]).astype(BF16)
        start += width
    for c in range(o_ref.shape[1] // tn):
        cols = slice(c * tn, (c + 1) * tn)
        o_ref[:, cols] = x_ref[:, cols] + jnp.dot(mix_ref[...], w_ref[:, cols],
                                                  preferred_element_type=F32)


def _out_proj(ya, yb, yc, ga, gb, gc, w, xt, to_cast, layer, tiles):
    n_tok, d = xt.shape
    tm, tn = tiles["tm"], tiles["tn_out"]
    wa, wb, wc = ya.shape[1], yb.shape[1], yc.shape[1]
    mix_w = wa + wb + wc
    steps = n_tok // tm
    cast_in, cast_out, cast_shapes = _cast_specs(to_cast, layer, steps)
    assert d % tn == 0 and w.shape == (mix_w, d)
    return pl.pallas_call(
        functools.partial(_out_proj_kernel, tn=tn, n_cast=len(to_cast)),
        grid=(steps,),
        in_specs=[
            pl.BlockSpec((tm, wa), lambda i: (i, 0)),
            pl.BlockSpec((tm, wb), lambda i: (i, 0)),
            pl.BlockSpec((tm, wc), lambda i: (i, 0)),
            pl.BlockSpec((1, wa), lambda i: (0, 0)),
            pl.BlockSpec((1, wb), lambda i: (0, 0)),
            pl.BlockSpec((1, wc), lambda i: (0, 0)),
            pl.BlockSpec((mix_w, d), lambda i: (0, 0), pipeline_mode=pl.Buffered(1)),
            pl.BlockSpec((tm, d), lambda i: (i, 0)),
        ] + cast_in,
        out_specs=[pl.BlockSpec((tm, d), lambda i: (i, 0))] + cast_out,
        out_shape=[jax.ShapeDtypeStruct((n_tok, d), F32)] + cast_shapes,
        scratch_shapes=[pltpu.VMEM((tm, mix_w), BF16)],
        compiler_params=_params("arbitrary"),
        name="out_proj",
    )(ya, yb, yc, ga.reshape(1, wa), gb.reshape(1, wb), gc.reshape(1, wc), w, xt, *to_cast)


def _mlp_kernel(x_ref, g_ref, wu_ref, wd_ref, gf_ref, o_ref, h_ref, *, final):
    f = pl.program_id(1)

    @pl.when(f == 0)
    def _():
        x = x_ref[...]
        ms = jnp.mean(x * x, axis=-1, keepdims=True)
        h_ref[...] = ((x * lax.rsqrt(ms + EPS)) * g_ref[...]).astype(BF16)
        o_ref[...] = x

    u = jnp.maximum(jnp.dot(h_ref[...], wu_ref[...], preferred_element_type=F32), 0.0)
    o_ref[...] += jnp.dot((u * u).astype(BF16), wd_ref[...], preferred_element_type=F32)

    if final:
        @pl.when(f == pl.num_programs(1) - 1)
        def _():
            y = o_ref[...]
            ms = jnp.mean(y * y, axis=-1, keepdims=True)
            o_ref[...] = (y * lax.rsqrt(ms + EPS)) * gf_ref[...]


def _mlp(xt, gain, w_up, w_down, final_gain, tiles, final):
    n_tok, d = xt.shape
    d_ff = w_up.shape[1]
    tm, tf = tiles["tm_mlp"], tiles["tf"]
    assert d_ff % tf == 0 and n_tok % tm == 0
    return pl.pallas_call(
        functools.partial(_mlp_kernel, final=final),
        grid=(n_tok // tm, d_ff // tf),
        in_specs=[
            pl.BlockSpec((tm, d), lambda i, f: (i, 0)),
            pl.BlockSpec((1, d), lambda i, f: (0, 0)),
            pl.BlockSpec((d, tf), lambda i, f: (0, f)),
            pl.BlockSpec((tf, d), lambda i, f: (f, 0)),
            pl.BlockSpec((1, d), lambda i, f: (0, 0)),
        ],
        out_specs=pl.BlockSpec((tm, d), lambda i, f: (i, 0)),
        out_shape=jax.ShapeDtypeStruct((n_tok, d), F32),
        scratch_shapes=[pltpu.VMEM((tm, d), BF16)],
        compiler_params=_params("parallel", "arbitrary"),
        name="mlp",
    )(xt, gain.reshape(1, d), w_up, w_down, final_gain.reshape(1, d))


def kernel(x, attn_norm, w_in, sinks, gn_sb, gn_moba, gn_swa, w_out, mlp_norm, w_up, w_down, final_norm):
    b, seq, d = x.shape
    depth = w_in.shape[0]
    n_tok = b * seq
    lay = _layout(d)
    tiles = _tiles(n_tok, seq)
    cos_t, sin_t = _rope_tables(seq)
    xt = x.reshape(n_tok, d)
    w_in_l = w_in[0].astype(BF16)
    for l in range(depth):
        proj, w_up_l, w_down_l, w_out_l = _in_proj(xt, attn_norm[l], w_in_l, cos_t, sin_t,
                                                   (w_up, w_down, w_out), l, lay, seq, tiles)
        proj = proj.reshape(b, seq, lay["in_width"])
        ya = _sb_attention(proj, lay).reshape(n_tok, -1)
        yb = _moba_attention(proj, lay).reshape(n_tok, -1)
        yc = _swa_attention(proj, sinks[l], lay).reshape(n_tok, -1)
        next_w_in = (w_in,) if l + 1 < depth else ()
        xt, *cast = _out_proj(ya, yb, yc, gn_sb[l], gn_moba[l], gn_swa[l], w_out_l, xt,
                              next_w_in, l + 1, tiles)
        if cast:
            w_in_l = cast[0]
        xt = _mlp(xt, mlp_norm[l], w_up_l, w_down_l, final_norm, tiles, final=(l == depth - 1))
    return xt.reshape(b, seq, d)
```

```python
import functools

import numpy as np
import jax
import jax.numpy as jnp
from jax import lax
from jax.experimental import pallas as pl
from jax.experimental.pallas import tpu as pltpu

F32 = jnp.float32
BF16 = jnp.bfloat16

HEAD_DIM = 64
PAIR = 2 * HEAD_DIM
BF16_SUBLANES = 16
ROPE_HALF = HEAD_DIM // 2
MOBA_BLOCK = 256
MOBA_TOPK = 3
WINDOW = 128
SWA_GROUP = 8
ROPE_THETA = 10000.0
EPS = 1e-6
NEG = -1e30
Q_SCALE = HEAD_DIM ** -0.5
MOBA_GROUP = 4
MOBA_PAIRS_PER_STEP = 4
SB_TILE = 256
SB_CHUNK = PAIR
SB_PAIRS_PER_STEP = 4
SB_UNDERFLOW = 104.0

VMEM_LIMIT_BYTES = 56 * 1024 * 1024

NT_DIMS = (((1,), (1,)), ((), ()))


def _params(*semantics):
    return pltpu.CompilerParams(dimension_semantics=semantics,
                                vmem_limit_bytes=VMEM_LIMIT_BYTES)


def _lane_index(shape, dtype=jnp.int32):
    idx = lax.broadcasted_iota(jnp.int32, shape, len(shape) - 1)
    return idx if dtype == jnp.int32 else idx.astype(F32).astype(dtype)


def _layout(d_model):
    sb = d_model // 4
    moba = d_model // 4
    swa_q = d_model // 2
    swa_kv = (swa_q // HEAD_DIM // SWA_GROUP) * HEAD_DIM
    sizes = (sb, sb, sb, moba, moba, moba, swa_q, swa_kv, swa_kv)
    offs = np.concatenate([[0], np.cumsum(sizes)]).astype(int)
    names = ("qa", "ka", "va", "qb", "kb", "vb", "qc", "kc", "vc")
    lay = {n: (int(offs[i]), int(sizes[i])) for i, n in enumerate(names)}
    lay["in_width"] = int(offs[-1])
    assert sb % PAIR == 0 and swa_q % PAIR == 0 and swa_kv == PAIR
    return lay


def _tiles(n_tokens, seq):
    tm = min(512, seq)
    assert seq % tm == 0 and n_tokens % tm == 0
    return dict(tm=tm, tm_mlp=min(512, n_tokens), tn_in=256, tn_out=512, tf=1024)


def _rope_tables(seq):
    inv_freq = ROPE_THETA ** (-jnp.arange(ROPE_HALF, dtype=F32) * 2.0 / HEAD_DIM)
    ang = jnp.arange(seq, dtype=F32)[:, None] * inv_freq[None, :]
    cos, sin = jnp.cos(ang), jnp.sin(ang)
    cos_t = jnp.tile(cos, (1, PAIR // ROPE_HALF))
    sin_t = jnp.tile(jnp.concatenate([-sin, sin], axis=1), (1, PAIR // HEAD_DIM))
    return cos_t, sin_t


def _cast_specs(weights, layer, steps):
    in_specs, out_specs, out_shapes = [], [], []
    for w in weights:
        _, rows, cols = w.shape
        slab = rows // steps
        assert rows % steps == 0 and slab % BF16_SUBLANES == 0
        in_specs.append(pl.BlockSpec((None, slab, cols), lambda i: (layer, i, 0)))
        out_specs.append(pl.BlockSpec((slab, cols), lambda i: (i, 0)))
        out_shapes.append(jax.ShapeDtypeStruct((rows, cols), BF16))
    return in_specs, out_specs, out_shapes


def _cast_slabs(src_refs, dst_refs):
    for src, dst in zip(src_refs, dst_refs):
        dst[...] = src[...].astype(dst.dtype)


def _in_proj_kernel(x_ref, g_ref, w_ref, cos_ref, sin_ref, *refs, chunks, n_cast):
    o_ref, h_ref = refs[n_cast], refs[-1]
    _cast_slabs(refs[:n_cast], refs[n_cast + 1:-1])
    x = x_ref[...]
    ms = jnp.mean(x * x, axis=-1, keepdims=True)
    h_ref[...] = ((x * lax.rsqrt(ms + EPS)) * g_ref[...]).astype(BF16)
    tm = x.shape[0]
    lane = lax.broadcasted_iota(jnp.int32, (tm, PAIR), 1)
    first = (lane % HEAD_DIM) < ROPE_HALF
    for start, classes in chunks:
        width = len(classes) * PAIR
        acc = jnp.dot(h_ref[...], w_ref[:, start:start + width], preferred_element_type=F32)
        for t, (rope, scale) in enumerate(classes):
            a = acc[:, t * PAIR:(t + 1) * PAIR]
            if rope:
                partner = jnp.where(first, pltpu.roll(a, PAIR - ROPE_HALF, axis=1),
                                    pltpu.roll(a, ROPE_HALF, axis=1))
                a = a * cos_ref[...] + partner * sin_ref[...]
            if scale != 1.0:
                a = a * scale
            o_ref[:, start + t * PAIR:start + (t + 1) * PAIR] = a.astype(o_ref.dtype)


def _in_proj(xt, gain, w, cos_t, sin_t, to_cast, layer, lay, seq, tiles):
    n_tok, d = xt.shape
    in_w = lay["in_width"]
    tm, tn = tiles["tm"], tiles["tn_in"]
    steps = n_tok // tm
    cast_in, cast_out, cast_shapes = _cast_specs(to_cast, layer, steps)
    assert in_w % tn == 0 and tn % PAIR == 0
    rope = np.zeros(in_w // PAIR, bool)
    scale = np.ones(in_w // PAIR, np.float32)
    for name in ("qb", "kb", "qc", "kc"):
        o, s = lay[name]
        rope[o // PAIR:(o + s) // PAIR] = True
    for name in ("qa", "qb", "qc"):
        o, s = lay[name]
        scale[o // PAIR:(o + s) // PAIR] = Q_SCALE
    per = tn // PAIR
    chunks = tuple((c * tn, tuple((bool(rope[c * per + t]), float(scale[c * per + t])) for t in range(per)))
                   for c in range(in_w // tn))
    pos_blocks = seq // tm
    return pl.pallas_call(
        functools.partial(_in_proj_kernel, chunks=chunks, n_cast=len(to_cast)),
        grid=(steps,),
        in_specs=[
            pl.BlockSpec((tm, d), lambda i: (i, 0)),
            pl.BlockSpec((1, d), lambda i: (0, 0)),
            pl.BlockSpec((d, in_w), lambda i: (0, 0), pipeline_mode=pl.Buffered(1)),
            pl.BlockSpec((tm, PAIR), lambda i: (i % pos_blocks, 0)),
            pl.BlockSpec((tm, PAIR), lambda i: (i % pos_blocks, 0)),
        ] + cast_in,
        out_specs=[pl.BlockSpec((tm, in_w), lambda i: (i, 0))] + cast_out,
        out_shape=[jax.ShapeDtypeStruct((n_tok, in_w), BF16)] + cast_shapes,
        scratch_shapes=[pltpu.VMEM((tm, d), BF16)],
        compiler_params=_params("arbitrary"),
        name="in_proj",
    )(xt, gain.reshape(1, d), w, cos_t, sin_t, *to_cast)


def _sb_kernel(q_ref, k_ref, v_ref, u_ref, o_ref, carry_ref, acc_ref, *, tq):
    i = pl.program_id(2)
    ch = SB_CHUNK
    n_pairs = q_ref.shape[2] // PAIR

    def lanes(p):
        return slice(p * PAIR, (p + 1) * PAIR)

    lo = _lane_index((tq, PAIR), BF16) < HEAD_DIM
    qs = []
    for p in range(n_pairs):
        q = q_ref[0, :, lanes(p)]
        zq = jnp.zeros_like(q)
        qs.append(jnp.concatenate([jnp.where(lo, q, zq), jnp.where(lo, zq, q)], axis=0))
    u = u_ref[...]
    row = lax.broadcasted_iota(jnp.int32, (2 * tq, tq), 0) % tq
    col = lax.broadcasted_iota(jnp.int32, (2 * tq, tq), 1)
    past = col < row

    def tiles(off, n_tiles, carries, accs, diagonal):
        nk = n_tiles * tq
        kblk = k_ref[0, pl.ds(off, nk), :]
        vblk = v_ref[0, pl.ds(off, nk), :]
        zs = [lax.dot_general(qs[p], kblk[:, lanes(p)], NT_DIMS, preferred_element_type=F32)
              for p in range(n_pairs)]
        sps = [jnp.maximum(z, 0.0) + jnp.log(1.0 + jnp.exp(-jnp.abs(z))) for z in zs]
        ws = [[None] * (nk // ch) for _ in range(n_pairs)]
        carries = list(carries)
        for c in reversed(range(nk // ch)):
            sl = slice(c * ch, (c + 1) * ch)
            masked = diagonal and c * ch >= nk - tq
            mask = past[:, c * ch - (nk - tq):(c + 1) * ch - (nk - tq)] if masked else None
            for p in range(n_pairs):
                s_c = jnp.where(mask, sps[p][:, sl], 0.0) if masked else sps[p][:, sl]
                hi = s_c.astype(BF16)
                lo_part = (s_c - hi.astype(F32)).astype(BF16)
                r = jnp.dot(jnp.concatenate([hi, lo_part], axis=1), u, preferred_element_type=F32)
                w = jnp.exp(zs[p][:, sl] - sps[p][:, sl] - r[:, :ch] - carries[p])
                if masked:
                    w = jnp.where(mask, w, 0.0)
                ws[p][c] = w.astype(BF16)
                carries[p] = carries[p] + r[:, ch:]
        lo_v = _lane_index((nk, PAIR), BF16) < HEAD_DIM
        accs = list(accs)
        for p in range(n_pairs):
            wb = jnp.concatenate(ws[p], axis=1)
            wcat = jnp.concatenate([wb[:tq], wb[tq:]], axis=1)
            vp = vblk[:, lanes(p)]
            zv = jnp.zeros_like(vp)
            vcat = jnp.concatenate([jnp.where(lo_v, vp, zv), jnp.where(lo_v, zv, vp)], axis=0)
            accs[p] = accs[p] + jnp.dot(wcat, vcat, preferred_element_type=F32)
        return carries, accs

    zero_carry = [jnp.zeros((2 * tq, PAIR), F32)] * n_pairs
    zero_acc = [jnp.zeros((tq, PAIR), F32)] * n_pairs

    def first_pass(off, n_tiles):
        carries, accs = tiles(off, n_tiles, zero_carry, zero_acc, True)
        for p in range(n_pairs):
            carry_ref[p] = carries[p]
            acc_ref[p] = accs[p]

    @pl.when(i == 0)
    def _():
        first_pass(0, 1)

    @pl.when(i > 0)
    def _():
        first_pass(pl.multiple_of((i - 1) * tq, tq), 2)

    def unfinished(carries):
        return functools.reduce(jnp.minimum, [jnp.min(c) for c in carries]) < SB_UNDERFLOW

    def cond(state):
        return (state[0] < i - 1) & state[1]

    def body(state):
        t = state[0]
        off = pl.multiple_of((i - 2 - t) * tq, tq)
        carries, accs = tiles(off, 1, state[2:2 + n_pairs], state[2 + n_pairs:], False)
        return (t + 1, unfinished(carries), *carries, *accs)

    carries = [carry_ref[p] for p in range(n_pairs)]
    accs = [acc_ref[p] for p in range(n_pairs)]
    state = lax.while_loop(cond, body, (jnp.int32(0), unfinished(carries), *carries, *accs))
    for p in range(n_pairs):
        o_ref[0, :, lanes(p)] = state[2 + n_pairs + p].astype(o_ref.dtype)


def _sb_attention(proj, lay):
    b, seq, _ = proj.shape
    tq = min(SB_TILE, seq)
    ch = SB_CHUNK
    assert seq % tq == 0 and tq % ch == 0
    q_off, width = lay["qa"]
    k_off, v_off = lay["ka"][0], lay["va"][0]
    n_pairs = SB_PAIRS_PER_STEP
    lw = n_pairs * PAIR
    assert width % lw == 0 and q_off % lw == 0 and k_off % lw == 0 and v_off % lw == 0
    tri = np.tril(np.ones((ch, ch), np.float32), -1)
    uu = np.concatenate([tri, np.ones((ch, PAIR), np.float32)], axis=1)
    uu = jnp.asarray(np.concatenate([uu, uu], axis=0), dtype=BF16)
    return pl.pallas_call(
        functools.partial(_sb_kernel, tq=tq),
        grid=(b, width // lw, seq // tq),
        in_specs=[
            pl.BlockSpec((1, tq, lw), lambda bi, p, i: (bi, i, q_off // lw + p)),
            pl.BlockSpec((1, seq, lw), lambda bi, p, i: (bi, 0, k_off // lw + p)),
            pl.BlockSpec((1, seq, lw), lambda bi, p, i: (bi, 0, v_off // lw + p)),
            pl.BlockSpec((2 * ch, ch + PAIR), lambda bi, p, i: (0, 0)),
        ],
        out_specs=pl.BlockSpec((1, tq, lw), lambda bi, p, i: (bi, i, p)),
        out_shape=jax.ShapeDtypeStruct((b, seq, width), BF16),
        scratch_shapes=[pltpu.VMEM((n_pairs, 2 * tq, PAIR), F32), pltpu.VMEM((n_pairs, tq, PAIR), F32)],
        compiler_params=_params("parallel", "parallel", "arbitrary"),
        name="sb_attention",
    )(proj, proj, proj, uu)


def _moba_kernel(q_ref, k_ref, v_ref, o_ref, kmean_ref, *, nb, nbp, group):
    i = pl.program_id(2)
    blk = MOBA_BLOCK
    tq = blk
    n_pairs = q_ref.shape[2] // PAIR

    def lanes(p):
        return slice(p * PAIR, (p + 1) * PAIR)

    @pl.when(i == 0)
    def _():
        kf = k_ref[0].astype(F32).reshape(nb, blk, n_pairs * PAIR)
        km = jnp.sum(kf, axis=1) * (1.0 / blk)
        if nbp > nb:
            km = jnp.concatenate([km, jnp.zeros((nbp - nb, n_pairs * PAIR), F32)], axis=0)
        kmean_ref[...] = km

    lo_qb = _lane_index((tq, PAIR), BF16) < HEAD_DIM
    lo_m = _lane_index((nbp, PAIR)) < HEAD_DIM
    jidx = lax.broadcasted_iota(jnp.int32, (nbp, tq), 0)
    valid = jidx < i

    def gated_queries(q, km):
        zq = jnp.zeros_like(q)
        plain, aug = [], []
        for h in (0, 1):
            head_m = lo_m if h == 0 else jnp.logical_not(lo_m)
            head_q = lo_qb if h == 0 else jnp.logical_not(lo_qb)
            kmh = jnp.where(head_m, km, 0.0)
            a = kmh.astype(BF16)
            r1 = kmh - a.astype(F32)
            b2 = r1.astype(BF16)
            c3 = (r1 - b2.astype(F32)).astype(BF16)
            g3 = lax.dot_general(jnp.concatenate([a, b2, c3], axis=0), q, NT_DIMS,
                                 preferred_element_type=F32)
            gate = g3[:nbp] + g3[nbp:2 * nbp] + g3[2 * nbp:]
            gate = jnp.where(valid, gate, -jnp.inf)
            beaten_by = jnp.zeros((nbp, tq), jnp.int32)
            for jp in range(nb):
                other = gate[jp:jp + 1, :]
                beats = (other > gate) | ((other == gate) & (jidx > jp))
                beaten_by = beaten_by + beats.astype(jnp.int32)
            sel = valid & (beaten_by < MOBA_TOPK)
            bias_t = jnp.where(sel, 0.0, NEG)
            top = HEAD_DIM if h == 0 else 0
            pieces = [jnp.zeros((top, tq), F32)] if top else []
            pieces.append(bias_t)
            if PAIR - top - nbp:
                pieces.append(jnp.zeros((PAIR - top - nbp, tq), F32))
            placed = jnp.concatenate(pieces, axis=0).T
            plain.append(jnp.where(head_q, q, zq))
            aug.append(jnp.where(head_q, q, placed.astype(BF16)))
        return plain, aug

    q_plain, q_aug = [], []
    for p in range(n_pairs):
        plain, aug = gated_queries(q_ref[0, :, lanes(p)], kmean_ref[:, lanes(p)])
        q_plain.append(plain)
        q_aug.append(aug)

    lo_q = _lane_index((tq, PAIR)) < HEAD_DIM
    lo_q2 = jnp.concatenate([lo_q, lo_q], axis=1)

    def attend(s0, s1, vblk, m0, m1, acc):
        n0 = jnp.maximum(m0, jnp.max(s0, axis=1, keepdims=True))
        n1 = jnp.maximum(m1, jnp.max(s1, axis=1, keepdims=True))
        p0 = jnp.exp(s0 - n0).astype(BF16)
        p1 = jnp.exp(s1 - n1).astype(BF16)
        alpha = jnp.where(lo_q, jnp.exp(m0 - n0), jnp.exp(m1 - n1))
        rhs = jnp.concatenate([vblk, jnp.ones_like(vblk)], axis=1)
        u0 = jnp.dot(p0, rhs, preferred_element_type=F32)
        u1 = jnp.dot(p1, rhs, preferred_element_type=F32)
        acc = acc * jnp.concatenate([alpha, alpha], axis=1) + jnp.where(lo_q2, u0, u1)
        return n0, n1, acc

    own_off = pl.multiple_of(i * blk, blk)
    k_own = k_ref[0, pl.ds(own_off, blk), :]
    v_own = v_ref[0, pl.ds(own_off, blk), :]
    row = lax.broadcasted_iota(jnp.int32, (tq, blk), 0)
    col = lax.broadcasted_iota(jnp.int32, (tq, blk), 1)
    causal = col <= row
    m_init = jnp.full((tq, 1), NEG, F32)
    state = []
    for p in range(n_pairs):
        s_own = [jnp.where(causal, lax.dot_general(qh, k_own[:, lanes(p)], NT_DIMS,
                                                   preferred_element_type=F32), NEG)
                 for qh in q_plain[p]]
        state += attend(s_own[0], s_own[1], v_own[:, lanes(p)],
                        m_init, m_init, jnp.zeros((tq, 2 * PAIR), F32))

    gk = group * blk
    lane_g = _lane_index((gk, PAIR), BF16)
    lo_g = lane_g < HEAD_DIM
    blk_in_group = (lax.broadcasted_iota(jnp.int32, (gk, PAIR), 0) // blk).astype(F32).astype(BF16)
    one = jnp.ones((gk, PAIR), BF16)
    zk = jnp.zeros((gk, PAIR), BF16)

    def body(g, st):
        o = pl.multiple_of(g * gk, gk)
        kg = k_ref[0, pl.ds(o, gk), :]
        vg = v_ref[0, pl.ds(o, gk), :]
        first = jnp.full((1, PAIR), g * group, jnp.int32).astype(F32).astype(BF16)
        blk_id = blk_in_group + first
        ind0 = jnp.where(lane_g == blk_id + HEAD_DIM, one, zk)
        ind1 = jnp.where(lane_g == blk_id, one, zk)
        scores = []
        for p in range(n_pairs):
            kp = kg[:, lanes(p)]
            scores.append((
                lax.dot_general(q_aug[p][0], jnp.where(lo_g, kp, ind0), NT_DIMS, preferred_element_type=F32),
                lax.dot_general(q_aug[p][1], jnp.where(lo_g, ind1, kp), NT_DIMS, preferred_element_type=F32)))
        new = []
        for p in range(n_pairs):
            new += attend(scores[p][0], scores[p][1], vg[:, lanes(p)], *st[3 * p:3 * p + 3])
        return tuple(new)

    state = lax.fori_loop(0, (i + group - 1) // group, body, tuple(state))
    for p in range(n_pairs):
        acc = state[3 * p + 2]
        o_ref[0, :, lanes(p)] = (acc[:, :PAIR] / acc[:, PAIR:]).astype(o_ref.dtype)


def _moba_attention(proj, lay):
    b, seq, _ = proj.shape
    tq = MOBA_BLOCK
    assert seq % MOBA_BLOCK == 0
    nb = seq // MOBA_BLOCK
    nbp = -(-nb // 8) * 8
    group = min(MOBA_GROUP, nb)
    assert nbp <= HEAD_DIM
    assert nb % group == 0
    q_off, width = lay["qb"]
    k_off, v_off = lay["kb"][0], lay["vb"][0]
    lw = MOBA_PAIRS_PER_STEP * PAIR
    assert width % lw == 0 and q_off % lw == 0 and k_off % lw == 0 and v_off % lw == 0
    return pl.pallas_call(
        functools.partial(_moba_kernel, nb=nb, nbp=nbp, group=group),
        grid=(b, width // lw, seq // tq),
        in_specs=[
            pl.BlockSpec((1, tq, lw), lambda bi, p, i: (bi, i, q_off // lw + p)),
            pl.BlockSpec((1, seq, lw), lambda bi, p, i: (bi, 0, k_off // lw + p)),
            pl.BlockSpec((1, seq, lw), lambda bi, p, i: (bi, 0, v_off // lw + p)),
        ],
        out_specs=pl.BlockSpec((1, tq, lw), lambda bi, p, i: (bi, i, p)),
        out_shape=jax.ShapeDtypeStruct((b, seq, width), BF16),
        scratch_shapes=[pltpu.VMEM((nbp, lw), F32)],
        compiler_params=_params("parallel", "parallel", "arbitrary"),
        name="moba_attention",
    )(proj, proj, proj)


def _swa_kernel(sink_ref, q_ref, kp_ref, kc_ref, vp_ref, vc_ref, o_ref, *, n_pairs):
    n = pl.program_id(1)
    w = WINDOW
    k = jnp.concatenate([kp_ref[0], kc_ref[0]], axis=0).astype(F32)
    v = jnp.concatenate([vp_ref[0], vc_ref[0]], axis=0).astype(F32)
    lo_k = lax.broadcasted_iota(jnp.int32, (2 * w, PAIR), 1) < HEAD_DIM
    k_sw = pltpu.roll(k, HEAD_DIM, axis=1)
    v_sw = pltpu.roll(v, HEAD_DIM, axis=1)
    kk = [jnp.where(lo_k, k, k_sw).astype(BF16), jnp.where(lo_k, k_sw, k).astype(BF16)]
    vv = [jnp.where(lo_k, v, v_sw).astype(BF16), jnp.where(lo_k, v_sw, v).astype(BF16)]
    ones = jnp.ones((2 * w, PAIR), BF16)
    rhs = [jnp.concatenate([vg, ones], axis=1) for vg in vv]

    row = lax.broadcasted_iota(jnp.int32, (w, 2 * w), 0)
    col = lax.broadcasted_iota(jnp.int32, (w, 2 * w), 1)
    delta = row + w - col
    valid = (delta >= 0) & (delta < w) & ((n - 1) * w + col >= 0)
    lo_q = _lane_index((w, PAIR)) < HEAD_DIM
    lo_qb = _lane_index((w, PAIR), BF16) < HEAD_DIM
    pairs_per_kv = SWA_GROUP // 2

    for p in range(n_pairs):
        g = p // pairs_per_kv
        qp = q_ref[0, :, p * PAIR:(p + 1) * PAIR]
        zq = jnp.zeros_like(qp)
        outs = []
        for h in (0, 1):
            qh = jnp.where(lo_qb, qp, zq) if h == 0 else jnp.where(lo_qb, zq, qp)
            s = lax.dot_general(qh, kk[g], NT_DIMS, preferred_element_type=F32)
            s = jnp.where(valid, s, NEG)
            sink = sink_ref[2 * p + h]
            m = jnp.maximum(jnp.max(s, axis=1, keepdims=True), sink)
            pr = jnp.exp(s - m).astype(BF16)
            o2 = jnp.dot(pr, rhs[g], preferred_element_type=F32)
            outs.append(o2[:, :PAIR] / (o2[:, PAIR:] + jnp.exp(sink - m)))
        o_ref[0, :, p * PAIR:(p + 1) * PAIR] = jnp.where(lo_q, outs[0], outs[1]).astype(o_ref.dtype)


def _swa_attention(proj, sinks, lay):
    b, seq, _ = proj.shape
    w = WINDOW
    q_off, width = lay["qc"]
    k_off, v_off = lay["kc"][0], lay["vc"][0]
    assert q_off % width == 0 and seq % w == 0
    return pl.pallas_call(
        functools.partial(_swa_kernel, n_pairs=width // PAIR),
        grid=(b, seq // w),
        in_specs=[
            pl.BlockSpec(memory_space=pltpu.SMEM),
            pl.BlockSpec((1, w, width), lambda bi, n: (bi, n, q_off // width)),
            pl.BlockSpec((1, w, PAIR), lambda bi, n: (bi, jnp.maximum(n - 1, 0), k_off // PAIR)),
            pl.BlockSpec((1, w, PAIR), lambda bi, n: (bi, n, k_off // PAIR)),
            pl.BlockSpec((1, w, PAIR), lambda bi, n: (bi, jnp.maximum(n - 1, 0), v_off // PAIR)),
            pl.BlockSpec((1, w, PAIR), lambda bi, n: (bi, n, v_off // PAIR)),
        ],
        out_specs=pl.BlockSpec((1, w, width), lambda bi, n: (bi, n, 0)),
        out_shape=jax.ShapeDtypeStruct((b, seq, width), BF16),
        compiler_params=_params("parallel", "arbitrary"),
        name="swa_attention",
    )(sinks.astype(F32), proj, proj, proj, proj, proj)


def _out_proj_kernel(ya_ref, yb_ref, yc_ref, ga_ref, gb_ref, gc_ref, w_ref, x_ref, *refs, tn, n_cast):
    o_ref, mix_ref = refs[n_cast], refs[-1]
    _cast_slabs(refs[:n_cast], refs[n_cast + 1:-1])
    start = 0
    for y_ref, g_ref in ((ya_ref, ga_ref), (yb_ref, gb_ref), (yc_ref, gc_ref)):
        y = y_ref[...].astype(F32)
        ms = jnp.mean(y * y, axis=-1, keepdims=True)
        width = y.shape[1]
        mix_ref[:, start:start + width] = ((y * lax.rsqrt(ms + EPS)) * g_ref[...]).astype(BF16)
        start += width
    for c in range(o_ref.shape[1] // tn):
        cols = slice(c * tn, (c + 1) * tn)
        o_ref[:, cols] = x_ref[:, cols] + jnp.dot(mix_ref[...], w_ref[:, cols],
                                                  preferred_element_type=F32)


def _out_proj(ya, yb, yc, ga, gb, gc, w, xt, to_cast, layer, tiles):
    n_tok, d = xt.shape
    tm, tn = tiles["tm"], tiles["tn_out"]
    wa, wb, wc = ya.shape[1], yb.shape[1], yc.shape[1]
    mix_w = wa + wb + wc
    steps = n_tok // tm
    cast_in, cast_out, cast_shapes = _cast_specs(to_cast, layer, steps)
    assert d % tn == 0 and w.shape == (mix_w, d)
    return pl.pallas_call(
        functools.partial(_out_proj_kernel, tn=tn, n_cast=len(to_cast)),
        grid=(steps,),
        in_specs=[
            pl.BlockSpec((tm, wa), lambda i: (i, 0)),
            pl.BlockSpec((tm, wb), lambda i: (i, 0)),
            pl.BlockSpec((tm, wc), lambda i: (i, 0)),
            pl.BlockSpec((1, wa), lambda i: (0, 0)),
            pl.BlockSpec((1, wb), lambda i: (0, 0)),
            pl.BlockSpec((1, wc), lambda i: (0, 0)),
            pl.BlockSpec((mix_w, d), lambda i: (0, 0), pipeline_mode=pl.Buffered(1)),
            pl.BlockSpec((tm, d), lambda i: (i, 0)),
        ] + cast_in,
        out_specs=[pl.BlockSpec((tm, d), lambda i: (i, 0))] + cast_out,
        out_shape=[jax.ShapeDtypeStruct((n_tok, d), F32)] + cast_shapes,
        scratch_shapes=[pltpu.VMEM((tm, mix_w), BF16)],
        compiler_params=_params("arbitrary"),
        name="out_proj",
    )(ya, yb, yc, ga.reshape(1, wa), gb.reshape(1, wb), gc.reshape(1, wc), w, xt, *to_cast)


def _mlp_kernel(x_ref, g_ref, wu_ref, wd_ref, gf_ref, o_ref, h_ref, *, final):
    f = pl.program_id(1)

    @pl.when(f == 0)
    def _():
        x = x_ref[...]
        ms = jnp.mean(x * x, axis=-1, keepdims=True)
        h_ref[...] = ((x * lax.rsqrt(ms + EPS)) * g_ref[...]).astype(BF16)
        o_ref[...] = x

    u = jnp.maximum(jnp.dot(h_ref[...], wu_ref[...], preferred_element_type=F32), 0.0)
    o_ref[...] += jnp.dot((u * u).astype(BF16), wd_ref[...], preferred_element_type=F32)

    if final:
        @pl.when(f == pl.num_programs(1) - 1)
        def _():
            y = o_ref[...]
            ms = jnp.mean(y * y, axis=-1, keepdims=True)
            o_ref[...] = (y * lax.rsqrt(ms + EPS)) * gf_ref[...]


def _mlp(xt, gain, w_up, w_down, final_gain, tiles, final):
    n_tok, d = xt.shape
    d_ff = w_up.shape[1]
    tm, tf = tiles["tm_mlp"], tiles["tf"]
    assert d_ff % tf == 0 and n_tok % tm == 0
    return pl.pallas_call(
        functools.partial(_mlp_kernel, final=final),
        grid=(n_tok // tm, d_ff // tf),
        in_specs=[
            pl.BlockSpec((tm, d), lambda i, f: (i, 0)),
            pl.BlockSpec((1, d), lambda i, f: (0, 0)),
            pl.BlockSpec((d, tf), lambda i, f: (0, f)),
            pl.BlockSpec((tf, d), lambda i, f: (f, 0)),
            pl.BlockSpec((1, d), lambda i, f: (0, 0)),
        ],
        out_specs=pl.BlockSpec((tm, d), lambda i, f: (i, 0)),
        out_shape=jax.ShapeDtypeStruct((n_tok, d), F32),
        scratch_shapes=[pltpu.VMEM((tm, d), BF16)],
        compiler_params=_params("parallel", "arbitrary"),
        name="mlp",
    )(xt, gain.reshape(1, d), w_up, w_down, final_gain.reshape(1, d))


def kernel(x, attn_norm, w_in, sinks, gn_sb, gn_moba, gn_swa, w_out, mlp_norm, w_up, w_down, final_norm):
    b, seq, d = x.shape
    depth = w_in.shape[0]
    n_tok = b * seq
    lay = _layout(d)
    tiles = _tiles(n_tok, seq)
    cos_t, sin_t = _rope_tables(seq)
    xt = x.reshape(n_tok, d)
    w_in_l = w_in[0].astype(BF16)
    for l in range(depth):
        proj, w_up_l, w_down_l, w_out_l = _in_proj(xt, attn_norm[l], w_in_l, cos_t, sin_t,
                                                   (w_up, w_down, w_out), l, lay, seq, tiles)
        proj = proj.reshape(b, seq, lay["in_width"])
        ya = _sb_attention(proj, lay).reshape(n_tok, -1)
        yb = _moba_attention(proj, lay).reshape(n_tok, -1)
        yc = _swa_attention(proj, sinks[l], lay).reshape(n_tok, -1)
        next_w_in = (w_in,) if l + 1 < depth else ()
        xt, *cast = _out_proj(ya, yb, yc, gn_sb[l], gn_moba[l], gn_swa[l], w_out_l, xt,
                              next_w_in, l + 1, tiles)
        if cast:
            w_in_l = cast[0]
        xt = _mlp(xt, mlp_norm[l], w_up_l, w_down_l, final_norm, tiles, final=(l == depth - 1))
    return xt.reshape(b, seq, d)
```

```python
import functools

import numpy as np
import jax
import jax.numpy as jnp
from jax import lax
from jax.experimental import pallas as pl
from jax.experimental.pallas import tpu as pltpu

F32 = jnp.float32
BF16 = jnp.bfloat16

HEAD_DIM = 64
PAIR = 2 * HEAD_DIM
BF16_SUBLANES = 16
ROPE_HALF = HEAD_DIM // 2
MOBA_BLOCK = 256
MOBA_TOPK = 3
WINDOW = 128
SWA_GROUP = 8
ROPE_THETA = 10000.0
EPS = 1e-6
NEG = -1e30
Q_SCALE = HEAD_DIM ** -0.5
MOBA_GROUP = 4
MOBA_PAIRS_PER_STEP = 4
SB_TILE = 256
SB_CHUNK = PAIR
SB_PAIRS_PER_STEP = 4
SB_UNDERFLOW = 104.0

VMEM_LIMIT_BYTES = 56 * 1024 * 1024

NT_DIMS = (((1,), (1,)), ((), ()))


def _params(*semantics):
    return pltpu.CompilerParams(dimension_semantics=semantics,
                                vmem_limit_bytes=VMEM_LIMIT_BYTES)


def _lane_index(shape, dtype=jnp.int32):
    idx = lax.broadcasted_iota(jnp.int32, shape, len(shape) - 1)
    return idx if dtype == jnp.int32 else idx.astype(F32).astype(dtype)


def _layout(d_model):
    sb = d_model // 4
    moba = d_model // 4
    swa_q = d_model // 2
    swa_kv = (swa_q // HEAD_DIM // SWA_GROUP) * HEAD_DIM
    sizes = (sb, sb, sb, moba, moba, moba, swa_q, swa_kv, swa_kv)
    offs = np.concatenate([[0], np.cumsum(sizes)]).astype(int)
    names = ("qa", "ka", "va", "qb", "kb", "vb", "qc", "kc", "vc")
    lay = {n: (int(offs[i]), int(sizes[i])) for i, n in enumerate(names)}
    lay["in_width"] = int(offs[-1])
    assert sb % PAIR == 0 and swa_q % PAIR == 0 and swa_kv == PAIR
    return lay


def _tiles(n_tokens, seq):
    tm = min(512, seq)
    assert seq % tm == 0 and n_tokens % tm == 0
    return dict(tm=tm, tm_mlp=min(512, n_tokens), tn_in=256, tn_out=512, tf=1024)


def _rope_tables(seq):
    inv_freq = ROPE_THETA ** (-jnp.arange(ROPE_HALF, dtype=F32) * 2.0 / HEAD_DIM)
    ang = jnp.arange(seq, dtype=F32)[:, None] * inv_freq[None, :]
    cos, sin = jnp.cos(ang), jnp.sin(ang)
    cos_t = jnp.tile(cos, (1, PAIR // ROPE_HALF))
    sin_t = jnp.tile(jnp.concatenate([-sin, sin], axis=1), (1, PAIR // HEAD_DIM))
    return cos_t, sin_t


def _cast_specs(weights, layer, steps):
    in_specs, out_specs, out_shapes = [], [], []
    for w in weights:
        _, rows, cols = w.shape
        slab = rows // steps
        assert rows % steps == 0 and slab % BF16_SUBLANES == 0
        in_specs.append(pl.BlockSpec((None, slab, cols), lambda i: (layer, i, 0)))
        out_specs.append(pl.BlockSpec((slab, cols), lambda i: (i, 0)))
        out_shapes.append(jax.ShapeDtypeStruct((rows, cols), BF16))
    return in_specs, out_specs, out_shapes


def _cast_slabs(src_refs, dst_refs):
    for src, dst in zip(src_refs, dst_refs):
        dst[...] = src[...].astype(dst.dtype)


def _in_proj_kernel(x_ref, g_ref, w_ref, cos_ref, sin_ref, sink_ref, *refs, chunks, n_cast, swa, pos_blocks):
    o_ref, yc_ref = refs[n_cast], refs[n_cast + 1]
    h_ref, kprev_ref, vprev_ref = refs[-3:]
    _cast_slabs(refs[:n_cast], refs[n_cast + 2:-3])
    i = pl.program_id(0)

    @pl.when(i == 0)
    def _():
        kprev_ref[...] = jnp.zeros_like(kprev_ref)
        vprev_ref[...] = jnp.zeros_like(vprev_ref)

    x = x_ref[...]
    ms = jnp.mean(x * x, axis=-1, keepdims=True)
    h_ref[...] = ((x * lax.rsqrt(ms + EPS)) * g_ref[...]).astype(BF16)
    tm = x.shape[0]
    lane = lax.broadcasted_iota(jnp.int32, (tm, PAIR), 1)
    first = (lane % HEAD_DIM) < ROPE_HALF

    def project(start, classes):
        width = len(classes) * PAIR
        acc = jnp.dot(h_ref[...], w_ref[:, start:start + width], preferred_element_type=F32)
        for t, (rope, scale) in enumerate(classes):
            a = acc[:, t * PAIR:(t + 1) * PAIR]
            if rope:
                partner = jnp.where(first, pltpu.roll(a, PAIR - ROPE_HALF, axis=1),
                                    pltpu.roll(a, ROPE_HALF, axis=1))
                a = a * cos_ref[...] + partner * sin_ref[...]
            if scale != 1.0:
                a = a * scale
            col = start + t * PAIR
            tile[col] = a.astype(o_ref.dtype)
            o_ref[:, col:col + PAIR] = tile[col]

    tile = {}
    q_off, q_w, k_off, v_off = swa
    needed = [c for c in chunks if c[0] >= q_off]
    others = [c for c in chunks if c[0] < q_off]
    for c in needed:
        project(*c)

    w = WINDOW
    n_blocks = tm // w
    per_block = -(-len(others) // n_blocks)
    for j in range(n_blocks):
        rows = slice(j * w, (j + 1) * w)
        if j == 0:
            k_prev, v_prev = kprev_ref[...], vprev_ref[...]
            at_start = (i % pos_blocks) == 0
        else:
            prev = slice((j - 1) * w, j * w)
            k_prev, v_prev = tile[k_off][prev], tile[v_off][prev]
            at_start = False
        _swa_block(sink_ref, [tile[q_off + p * PAIR][rows] for p in range(q_w // PAIR)],
                   jnp.concatenate([k_prev, tile[k_off][rows]], axis=0),
                   jnp.concatenate([v_prev, tile[v_off][rows]], axis=0),
                   at_start, yc_ref, rows)
        for c in others[j * per_block:(j + 1) * per_block]:
            project(*c)
    last = slice(tm - w, tm)
    kprev_ref[...] = tile[k_off][last]
    vprev_ref[...] = tile[v_off][last]


def _in_proj(xt, gain, w, cos_t, sin_t, sinks, to_cast, layer, lay, seq, tiles):
    n_tok, d = xt.shape
    in_w = lay["in_width"]
    tm, tn = tiles["tm"], tiles["tn_in"]
    steps = n_tok // tm
    cast_in, cast_out, cast_shapes = _cast_specs(to_cast, layer, steps)
    assert in_w % tn == 0 and tn % PAIR == 0
    rope = np.zeros(in_w // PAIR, bool)
    scale = np.ones(in_w // PAIR, np.float32)
    for name in ("qb", "kb", "qc", "kc"):
        o, s = lay[name]
        rope[o // PAIR:(o + s) // PAIR] = True
    for name in ("qa", "qb", "qc"):
        o, s = lay[name]
        scale[o // PAIR:(o + s) // PAIR] = Q_SCALE
    per = tn // PAIR
    chunks = tuple((c * tn, tuple((bool(rope[c * per + t]), float(scale[c * per + t])) for t in range(per)))
                   for c in range(in_w // tn))
    pos_blocks = seq // tm
    (q_off, q_w), k_off, v_off = lay["qc"], lay["kc"][0], lay["vc"][0]
    assert tm % WINDOW == 0 and q_off % tn == 0 and q_off + q_w == k_off
    return pl.pallas_call(
        functools.partial(_in_proj_kernel, chunks=chunks, n_cast=len(to_cast),
                          swa=(q_off, q_w, k_off, v_off), pos_blocks=pos_blocks),
        grid=(steps,),
        in_specs=[
            pl.BlockSpec((tm, d), lambda i: (i, 0)),
            pl.BlockSpec((1, d), lambda i: (0, 0)),
            pl.BlockSpec((d, in_w), lambda i: (0, 0), pipeline_mode=pl.Buffered(1)),
            pl.BlockSpec((tm, PAIR), lambda i: (i % pos_blocks, 0)),
            pl.BlockSpec((tm, PAIR), lambda i: (i % pos_blocks, 0)),
            pl.BlockSpec(memory_space=pltpu.SMEM),
        ] + cast_in,
        out_specs=[pl.BlockSpec((tm, in_w), lambda i: (i, 0)),
                   pl.BlockSpec((tm, q_w), lambda i: (i, 0))] + cast_out,
        out_shape=[jax.ShapeDtypeStruct((n_tok, in_w), BF16),
                   jax.ShapeDtypeStruct((n_tok, q_w), BF16)] + cast_shapes,
        scratch_shapes=[pltpu.VMEM((tm, d), BF16), pltpu.VMEM((WINDOW, PAIR), BF16),
                        pltpu.VMEM((WINDOW, PAIR), BF16)],
        compiler_params=_params("arbitrary"),
        name="in_proj",
    )(xt, gain.reshape(1, d), w, cos_t, sin_t, sinks.astype(F32), *to_cast)


def _sb_kernel(q_ref, k_ref, v_ref, u_ref, o_ref, carry_ref, acc_ref, *, tq):
    i = pl.program_id(2)
    ch = SB_CHUNK
    n_pairs = q_ref.shape[2] // PAIR

    def lanes(p):
        return slice(p * PAIR, (p + 1) * PAIR)

    lo = _lane_index((tq, PAIR), BF16) < HEAD_DIM
    qs = []
    for p in range(n_pairs):
        q = q_ref[0, :, lanes(p)]
        zq = jnp.zeros_like(q)
        qs.append(jnp.concatenate([jnp.where(lo, q, zq), jnp.where(lo, zq, q)], axis=0))
    u = u_ref[...]
    row = lax.broadcasted_iota(jnp.int32, (2 * tq, tq), 0) % tq
    col = lax.broadcasted_iota(jnp.int32, (2 * tq, tq), 1)
    past = col < row

    def tiles(off, n_tiles, carries, accs, diagonal):
        nk = n_tiles * tq
        kblk = k_ref[0, pl.ds(off, nk), :]
        vblk = v_ref[0, pl.ds(off, nk), :]
        zs = [lax.dot_general(qs[p], kblk[:, lanes(p)], NT_DIMS, preferred_element_type=F32)
              for p in range(n_pairs)]
        sps = [jnp.maximum(z, 0.0) + jnp.log(1.0 + jnp.exp(-jnp.abs(z))) for z in zs]
        ws = [[None] * (nk // ch) for _ in range(n_pairs)]
        carries = list(carries)
        for c in reversed(range(nk // ch)):
            sl = slice(c * ch, (c + 1) * ch)
            masked = diagonal and c * ch >= nk - tq
            mask = past[:, c * ch - (nk - tq):(c + 1) * ch - (nk - tq)] if masked else None
            for p in range(n_pairs):
                s_c = jnp.where(mask, sps[p][:, sl], 0.0) if masked else sps[p][:, sl]
                hi = s_c.astype(BF16)
                lo_part = (s_c - hi.astype(F32)).astype(BF16)
                r = jnp.dot(jnp.concatenate([hi, lo_part], axis=1), u, preferred_element_type=F32)
                w = jnp.exp(zs[p][:, sl] - sps[p][:, sl] - r[:, :ch] - carries[p])
                if masked:
                    w = jnp.where(mask, w, 0.0)
                ws[p][c] = w.astype(BF16)
                carries[p] = carries[p] + r[:, ch:]
        lo_v = _lane_index((nk, PAIR), BF16) < HEAD_DIM
        accs = list(accs)
        for p in range(n_pairs):
            wb = jnp.concatenate(ws[p], axis=1)
            wcat = jnp.concatenate([wb[:tq], wb[tq:]], axis=1)
            vp = vblk[:, lanes(p)]
            zv = jnp.zeros_like(vp)
            vcat = jnp.concatenate([jnp.where(lo_v, vp, zv), jnp.where(lo_v, zv, vp)], axis=0)
            accs[p] = accs[p] + jnp.dot(wcat, vcat, preferred_element_type=F32)
        return carries, accs

    zero_carry = [jnp.zeros((2 * tq, PAIR), F32)] * n_pairs
    zero_acc = [jnp.zeros((tq, PAIR), F32)] * n_pairs

    def first_pass(off, n_tiles):
        carries, accs = tiles(off, n_tiles, zero_carry, zero_acc, True)
        for p in range(n_pairs):
            carry_ref[p] = carries[p]
            acc_ref[p] = accs[p]

    @pl.when(i == 0)
    def _():
        first_pass(0, 1)

    @pl.when(i > 0)
    def _():
        first_pass(pl.multiple_of((i - 1) * tq, tq), 2)

    def unfinished(carries):
        return functools.reduce(jnp.minimum, [jnp.min(c) for c in carries]) < SB_UNDERFLOW

    def cond(state):
        return (state[0] < i - 1) & state[1]

    def body(state):
        t = state[0]
        off = pl.multiple_of((i - 2 - t) * tq, tq)
        carries, accs = tiles(off, 1, state[2:2 + n_pairs], state[2 + n_pairs:], False)
        return (t + 1, unfinished(carries), *carries, *accs)

    carries = [carry_ref[p] for p in range(n_pairs)]
    accs = [acc_ref[p] for p in range(n_pairs)]
    state = lax.while_loop(cond, body, (jnp.int32(0), unfinished(carries), *carries, *accs))
    for p in range(n_pairs):
        o_ref[0, :, lanes(p)] = state[2 + n_pairs + p].astype(o_ref.dtype)


def _sb_attention(proj, lay):
    b, seq, _ = proj.shape
    tq = min(SB_TILE, seq)
    ch = SB_CHUNK
    assert seq % tq == 0 and tq % ch == 0
    q_off, width = lay["qa"]
    k_off, v_off = lay["ka"][0], lay["va"][0]
    n_pairs = SB_PAIRS_PER_STEP
    lw = n_pairs * PAIR
    assert width % lw == 0 and q_off % lw == 0 and k_off % lw == 0 and v_off % lw == 0
    tri = np.tril(np.ones((ch, ch), np.float32), -1)
    uu = np.concatenate([tri, np.ones((ch, PAIR), np.float32)], axis=1)
    uu = jnp.asarray(np.concatenate([uu, uu], axis=0), dtype=BF16)
    return pl.pallas_call(
        functools.partial(_sb_kernel, tq=tq),
        grid=(b, width // lw, seq // tq),
        in_specs=[
            pl.BlockSpec((1, tq, lw), lambda bi, p, i: (bi, i, q_off // lw + p)),
            pl.BlockSpec((1, seq, lw), lambda bi, p, i: (bi, 0, k_off // lw + p)),
            pl.BlockSpec((1, seq, lw), lambda bi, p, i: (bi, 0, v_off // lw + p)),
            pl.BlockSpec((2 * ch, ch + PAIR), lambda bi, p, i: (0, 0)),
        ],
        out_specs=pl.BlockSpec((1, tq, lw), lambda bi, p, i: (bi, i, p)),
        out_shape=jax.ShapeDtypeStruct((b, seq, width), BF16),
        scratch_shapes=[pltpu.VMEM((n_pairs, 2 * tq, PAIR), F32), pltpu.VMEM((n_pairs, tq, PAIR), F32)],
        compiler_params=_params("parallel", "parallel", "arbitrary"),
        name="sb_attention",
    )(proj, proj, proj, uu)


def _moba_kernel(q_ref, k_ref, v_ref, o_ref, kmean_ref, *, nb, nbp, group):
    i = pl.program_id(2)
    blk = MOBA_BLOCK
    tq = blk
    n_pairs = q_ref.shape[2] // PAIR

    def lanes(p):
        return slice(p * PAIR, (p + 1) * PAIR)

    @pl.when(i == 0)
    def _():
        kf = k_ref[0].astype(F32).reshape(nb, blk, n_pairs * PAIR)
        km = jnp.sum(kf, axis=1) * (1.0 / blk)
        if nbp > nb:
            km = jnp.concatenate([km, jnp.zeros((nbp - nb, n_pairs * PAIR), F32)], axis=0)
        kmean_ref[...] = km

    lo_qb = _lane_index((tq, PAIR), BF16) < HEAD_DIM
    lo_m = _lane_index((nbp, PAIR)) < HEAD_DIM
    jidx = lax.broadcasted_iota(jnp.int32, (nbp, tq), 0)
    valid = jidx < i

    def gated_queries(q, km):
        zq = jnp.zeros_like(q)
        plain, aug = [], []
        for h in (0, 1):
            head_m = lo_m if h == 0 else jnp.logical_not(lo_m)
            head_q = lo_qb if h == 0 else jnp.logical_not(lo_qb)
            kmh = jnp.where(head_m, km, 0.0)
            a = kmh.astype(BF16)
            r1 = kmh - a.astype(F32)
            b2 = r1.astype(BF16)
            c3 = (r1 - b2.astype(F32)).astype(BF16)
            g3 = lax.dot_general(jnp.concatenate([a, b2, c3], axis=0), q, NT_DIMS,
                                 preferred_element_type=F32)
            gate = g3[:nbp] + g3[nbp:2 * nbp] + g3[2 * nbp:]
            gate = jnp.where(valid, gate, -jnp.inf)
            beaten_by = jnp.zeros((nbp, tq), jnp.int32)
            for jp in range(nb):
                other = gate[jp:jp + 1, :]
                beats = (other > gate) | ((other == gate) & (jidx > jp))
                beaten_by = beaten_by + beats.astype(jnp.int32)
            sel = valid & (beaten_by < MOBA_TOPK)
            bias_t = jnp.where(sel, 0.0, NEG)
            top = HEAD_DIM if h == 0 else 0
            pieces = [jnp.zeros((top, tq), F32)] if top else []
            pieces.append(bias_t)
            if PAIR - top - nbp:
                pieces.append(jnp.zeros((PAIR - top - nbp, tq), F32))
            placed = jnp.concatenate(pieces, axis=0).T
            plain.append(jnp.where(head_q, q, zq))
            aug.append(jnp.where(head_q, q, placed.astype(BF16)))
        return plain, aug

    q_plain, q_aug = [], []
    for p in range(n_pairs):
        plain, aug = gated_queries(q_ref[0, :, lanes(p)], kmean_ref[:, lanes(p)])
        q_plain.append(plain)
        q_aug.append(aug)

    lo_q = _lane_index((tq, PAIR)) < HEAD_DIM
    lo_q2 = jnp.concatenate([lo_q, lo_q], axis=1)

    def attend(s0, s1, vblk, m0, m1, acc):
        n0 = jnp.maximum(m0, jnp.max(s0, axis=1, keepdims=True))
        n1 = jnp.maximum(m1, jnp.max(s1, axis=1, keepdims=True))
        p0 = jnp.exp(s0 - n0).astype(BF16)
        p1 = jnp.exp(s1 - n1).astype(BF16)
        alpha = jnp.where(lo_q, jnp.exp(m0 - n0), jnp.exp(m1 - n1))
        rhs = jnp.concatenate([vblk, jnp.ones_like(vblk)], axis=1)
        u0 = jnp.dot(p0, rhs, preferred_element_type=F32)
        u1 = jnp.dot(p1, rhs, preferred_element_type=F32)
        acc = acc * jnp.concatenate([alpha, alpha], axis=1) + jnp.where(lo_q2, u0, u1)
        return n0, n1, acc

    own_off = pl.multiple_of(i * blk, blk)
    k_own = k_ref[0, pl.ds(own_off, blk), :]
    v_own = v_ref[0, pl.ds(own_off, blk), :]
    row = lax.broadcasted_iota(jnp.int32, (tq, blk), 0)
    col = lax.broadcasted_iota(jnp.int32, (tq, blk), 1)
    causal = col <= row
    m_init = jnp.full((tq, 1), NEG, F32)
    state = []
    for p in range(n_pairs):
        s_own = [jnp.where(causal, lax.dot_general(qh, k_own[:, lanes(p)], NT_DIMS,
                                                   preferred_element_type=F32), NEG)
                 for qh in q_plain[p]]
        state += attend(s_own[0], s_own[1], v_own[:, lanes(p)],
                        m_init, m_init, jnp.zeros((tq, 2 * PAIR), F32))

    gk = group * blk
    lane_g = _lane_index((gk, PAIR), BF16)
    lo_g = lane_g < HEAD_DIM
    blk_in_group = (lax.broadcasted_iota(jnp.int32, (gk, PAIR), 0) // blk).astype(F32).astype(BF16)
    one = jnp.ones((gk, PAIR), BF16)
    zk = jnp.zeros((gk, PAIR), BF16)

    def body(g, st):
        o = pl.multiple_of(g * gk, gk)
        kg = k_ref[0, pl.ds(o, gk), :]
        vg = v_ref[0, pl.ds(o, gk), :]
        first = jnp.full((1, PAIR), g * group, jnp.int32).astype(F32).astype(BF16)
        blk_id = blk_in_group + first
        ind0 = jnp.where(lane_g == blk_id + HEAD_DIM, one, zk)
        ind1 = jnp.where(lane_g == blk_id, one, zk)
        scores = []
        for p in range(n_pairs):
            kp = kg[:, lanes(p)]
            scores.append((
                lax.dot_general(q_aug[p][0], jnp.where(lo_g, kp, ind0), NT_DIMS, preferred_element_type=F32),
                lax.dot_general(q_aug[p][1], jnp.where(lo_g, ind1, kp), NT_DIMS, preferred_element_type=F32)))
        new = []
        for p in range(n_pairs):
            new += attend(scores[p][0], scores[p][1], vg[:, lanes(p)], *st[3 * p:3 * p + 3])
        return tuple(new)

    state = lax.fori_loop(0, (i + group - 1) // group, body, tuple(state))
    for p in range(n_pairs):
        acc = state[3 * p + 2]
        o_ref[0, :, lanes(p)] = (acc[:, :PAIR] / acc[:, PAIR:]).astype(o_ref.dtype)


def _moba_attention(proj, lay):
    b, seq, _ = proj.shape
    tq = MOBA_BLOCK
    assert seq % MOBA_BLOCK == 0
    nb = seq // MOBA_BLOCK
    nbp = -(-nb // 8) * 8
    group = min(MOBA_GROUP, nb)
    assert nbp <= HEAD_DIM
    assert nb % group == 0
    q_off, width = lay["qb"]
    k_off, v_off = lay["kb"][0], lay["vb"][0]
    lw = MOBA_PAIRS_PER_STEP * PAIR
    assert width % lw == 0 and q_off % lw == 0 and k_off % lw == 0 and v_off % lw == 0
    return pl.pallas_call(
        functools.partial(_moba_kernel, nb=nb, nbp=nbp, group=group),
        grid=(b, width // lw, seq // tq),
        in_specs=[
            pl.BlockSpec((1, tq, lw), lambda bi, p, i: (bi, i, q_off // lw + p)),
            pl.BlockSpec((1, seq, lw), lambda bi, p, i: (bi, 0, k_off // lw + p)),
            pl.BlockSpec((1, seq, lw), lambda bi, p, i: (bi, 0, v_off // lw + p)),
        ],
        out_specs=pl.BlockSpec((1, tq, lw), lambda bi, p, i: (bi, i, p)),
        out_shape=jax.ShapeDtypeStruct((b, seq, width), BF16),
        scratch_shapes=[pltpu.VMEM((nbp, lw), F32)],
        compiler_params=_params("parallel", "parallel", "arbitrary"),
        name="moba_attention",
    )(proj, proj, proj)


def _swa_block(sink_ref, q, k2, v2, at_sequence_start, o_ref, rows):
    w = WINDOW
    n_pairs = len(q)
    k = k2.astype(F32)
    v = v2.astype(F32)
    lo_k = lax.broadcasted_iota(jnp.int32, (2 * w, PAIR), 1) < HEAD_DIM
    k_sw = pltpu.roll(k, HEAD_DIM, axis=1)
    v_sw = pltpu.roll(v, HEAD_DIM, axis=1)
    kk = [jnp.where(lo_k, k, k_sw).astype(BF16), jnp.where(lo_k, k_sw, k).astype(BF16)]
    vv = [jnp.where(lo_k, v, v_sw).astype(BF16), jnp.where(lo_k, v_sw, v).astype(BF16)]
    ones = jnp.ones((2 * w, PAIR), BF16)
    rhs = [jnp.concatenate([vg, ones], axis=1) for vg in vv]

    row = lax.broadcasted_iota(jnp.int32, (w, 2 * w), 0)
    col = lax.broadcasted_iota(jnp.int32, (w, 2 * w), 1)
    delta = row + w - col
    first_key = jnp.where(at_sequence_start, w, 0)
    valid = (delta >= 0) & (delta < w) & (col >= first_key)
    lo_q = _lane_index((w, PAIR)) < HEAD_DIM
    lo_qb = _lane_index((w, PAIR), BF16) < HEAD_DIM
    pairs_per_kv = SWA_GROUP // 2

    for p in range(n_pairs):
        g = p // pairs_per_kv
        qp = q[p]
        zq = jnp.zeros_like(qp)
        outs = []
        for h in (0, 1):
            qh = jnp.where(lo_qb, qp, zq) if h == 0 else jnp.where(lo_qb, zq, qp)
            s = lax.dot_general(qh, kk[g], NT_DIMS, preferred_element_type=F32)
            s = jnp.where(valid, s, NEG)
            sink = sink_ref[2 * p + h]
            m = jnp.maximum(jnp.max(s, axis=1, keepdims=True), sink)
            pr = jnp.exp(s - m).astype(BF16)
            o2 = jnp.dot(pr, rhs[g], preferred_element_type=F32)
            outs.append(o2[:, :PAIR] / (o2[:, PAIR:] + jnp.exp(sink - m)))
        o_ref[rows, p * PAIR:(p + 1) * PAIR] = jnp.where(lo_q, outs[0], outs[1]).astype(o_ref.dtype)


def _out_proj_kernel(ya_ref, yb_ref, yc_ref, ga_ref, gb_ref, gc_ref, w_ref, x_ref, *refs, tn, n_cast):
    o_ref, mix_ref = refs[n_cast], refs[-1]
    _cast_slabs(refs[:n_cast], refs[n_cast + 1:-1])
    start = 0
    for y_ref, g_ref in ((ya_ref, ga_ref), (yb_ref, gb_ref), (yc_ref, gc_ref)):
        y = y_ref[...].astype(F32)
        ms = jnp.mean(y * y, axis=-1, keepdims=True)
        width = y.shape[1]
        mix_ref[:, start:start + width] = ((y * lax.rsqrt(ms + EPS)) * g_ref[...]).astype(BF16)
        start += width
    for c in range(o_ref.shape[1] // tn):
        cols = slice(c * tn, (c + 1) * tn)
        o_ref[:, cols] = x_ref[:, cols] + jnp.dot(mix_ref[...], w_ref[:, cols],
                                                  preferred_element_type=F32)


def _out_proj(ya, yb, yc, ga, gb, gc, w, xt, to_cast, layer, tiles):
    n_tok, d = xt.shape
    tm, tn = tiles["tm"], tiles["tn_out"]
    wa, wb, wc = ya.shape[1], yb.shape[1], yc.shape[1]
    mix_w = wa + wb + wc
    steps = n_tok // tm
    cast_in, cast_out, cast_shapes = _cast_specs(to_cast, layer, steps)
    assert d % tn == 0 and w.shape == (mix_w, d)
    return pl.pallas_call(
        functools.partial(_out_proj_kernel, tn=tn, n_cast=len(to_cast)),
        grid=(steps,),
        in_specs=[
            pl.BlockSpec((tm, wa), lambda i: (i, 0)),
            pl.BlockSpec((tm, wb), lambda i: (i, 0)),
            pl.BlockSpec((tm, wc), lambda i: (i, 0)),
            pl.BlockSpec((1, wa), lambda i: (0, 0)),
            pl.BlockSpec((1, wb), lambda i: (0, 0)),
            pl.BlockSpec((1, wc), lambda i: (0, 0)),
            pl.BlockSpec((mix_w, d), lambda i: (0, 0), pipeline_mode=pl.Buffered(1)),
            pl.BlockSpec((tm, d), lambda i: (i, 0)),
        ] + cast_in,
        out_specs=[pl.BlockSpec((tm, d), lambda i: (i, 0))] + cast_out,
        out_shape=[jax.ShapeDtypeStruct((n_tok, d), F32)] + cast_shapes,
        scratch_shapes=[pltpu.VMEM((tm, mix_w), BF16)],
        compiler_params=_params("arbitrary"),
        name="out_proj",
    )(ya, yb, yc, ga.reshape(1, wa), gb.reshape(1, wb), gc.reshape(1, wc), w, xt, *to_cast)


def _mlp_kernel(x_ref, g_ref, wu_ref, wd_ref, gf_ref, o_ref, h_ref, *, final):
    f = pl.program_id(1)

    @pl.when(f == 0)
    def _():
        x = x_ref[...]
        ms = jnp.mean(x * x, axis=-1, keepdims=True)
        h_ref[...] = ((x * lax.rsqrt(ms + EPS)) * g_ref[...]).astype(BF16)
        o_ref[...] = x

    u = jnp.maximum(jnp.dot(h_ref[...], wu_ref[...], preferred_element_type=F32), 0.0)
    o_ref[...] += jnp.dot((u * u).astype(BF16), wd_ref[...], preferred_element_type=F32)

    if final:
        @pl.when(f == pl.num_programs(1) - 1)
        def _():
            y = o_ref[...]
            ms = jnp.mean(y * y, axis=-1, keepdims=True)
            o_ref[...] = (y * lax.rsqrt(ms + EPS)) * gf_ref[...]


def _mlp(xt, gain, w_up, w_down, final_gain, tiles, final):
    n_tok, d = xt.shape
    d_ff = w_up.shape[1]
    tm, tf = tiles["tm_mlp"], tiles["tf"]
    assert d_ff % tf == 0 and n_tok % tm == 0
    return pl.pallas_call(
        functools.partial(_mlp_kernel, final=final),
        grid=(n_tok // tm, d_ff // tf),
        in_specs=[
            pl.BlockSpec((tm, d), lambda i, f: (i, 0)),
            pl.BlockSpec((1, d), lambda i, f: (0, 0)),
            pl.BlockSpec((d, tf), lambda i, f: (0, f)),
            pl.BlockSpec((tf, d), lambda i, f: (f, 0)),
            pl.BlockSpec((1, d), lambda i, f: (0, 0)),
        ],
        out_specs=pl.BlockSpec((tm, d), lambda i, f: (i, 0)),
        out_shape=jax.ShapeDtypeStruct((n_tok, d), F32),
        scratch_shapes=[pltpu.VMEM((tm, d), BF16)],
        compiler_params=_params("parallel", "arbitrary"),
        name="mlp",
    )(xt, gain.reshape(1, d), w_up, w_down, final_gain.reshape(1, d))


def kernel(x, attn_norm, w_in, sinks, gn_sb, gn_moba, gn_swa, w_out, mlp_norm, w_up, w_down, final_norm):
    b, seq, d = x.shape
    depth = w_in.shape[0]
    n_tok = b * seq
    lay = _layout(d)
    tiles = _tiles(n_tok, seq)
    cos_t, sin_t = _rope_tables(seq)
    xt = x.reshape(n_tok, d)
    w_in_l = w_in[0].astype(BF16)
    for l in range(depth):
        proj, yc, w_up_l, w_down_l, w_out_l = _in_proj(xt, attn_norm[l], w_in_l, cos_t, sin_t, sinks[l],
                                                       (w_up, w_down, w_out), l, lay, seq, tiles)
        proj = proj.reshape(b, seq, lay["in_width"])
        ya = _sb_attention(proj, lay).reshape(n_tok, -1)
        yb = _moba_attention(proj, lay).reshape(n_tok, -1)
        next_w_in = (w_in,) if l + 1 < depth else ()
        xt, *cast = _out_proj(ya, yb, yc, gn_sb[l], gn_moba[l], gn_swa[l], w_out_l, xt,
                              next_w_in, l + 1, tiles)
        if cast:
            w_in_l = cast[0]
        xt = _mlp(xt, mlp_norm[l], w_up_l, w_down_l, final_norm, tiles, final=(l == depth - 1))
    return xt.reshape(b, seq, d)
```

```python
import functools

import numpy as np
import jax
import jax.numpy as jnp
from jax import lax
from jax.experimental import pallas as pl
from jax.experimental.pallas import tpu as pltpu

F32 = jnp.float32
BF16 = jnp.bfloat16

HEAD_DIM = 64
PAIR = 2 * HEAD_DIM
BF16_SUBLANES = 16
ROPE_HALF = HEAD_DIM // 2
MOBA_BLOCK = 256
MOBA_TOPK = 3
WINDOW = 128
SWA_GROUP = 8
ROPE_THETA = 10000.0
EPS = 1e-6
NEG = -1e30
Q_SCALE = HEAD_DIM ** -0.5
MOBA_GROUP = 4
MOBA_PAIRS_PER_STEP = 4
SB_TILE = 256
SB_CHUNK = PAIR
SB_PAIRS_PER_STEP = 4
SB_UNDERFLOW = 104.0

VMEM_LIMIT_BYTES = 56 * 1024 * 1024

NT_DIMS = (((1,), (1,)), ((), ()))


def _params(*semantics):
    return pltpu.CompilerParams(dimension_semantics=semantics,
                                vmem_limit_bytes=VMEM_LIMIT_BYTES)


def _lane_index(shape, dtype=jnp.int32):
    idx = lax.broadcasted_iota(jnp.int32, shape, len(shape) - 1)
    return idx if dtype == jnp.int32 else idx.astype(F32).astype(dtype)


def _layout(d_model):
    sb = d_model // 4
    moba = d_model // 4
    swa_q = d_model // 2
    swa_kv = (swa_q // HEAD_DIM // SWA_GROUP) * HEAD_DIM
    sizes = (sb, sb, sb, moba, moba, moba, swa_q, swa_kv, swa_kv)
    offs = np.concatenate([[0], np.cumsum(sizes)]).astype(int)
    names = ("qa", "ka", "va", "qb", "kb", "vb", "qc", "kc", "vc")
    lay = {n: (int(offs[i]), int(sizes[i])) for i, n in enumerate(names)}
    lay["in_width"] = int(offs[-1])
    assert sb % PAIR == 0 and swa_q % PAIR == 0 and swa_kv == PAIR
    return lay


def _tiles(n_tokens, seq):
    tm = min(512, seq)
    assert seq % tm == 0 and n_tokens % tm == 0
    return dict(tm=tm, tm_mlp=min(512, n_tokens), tm_up=min(1024, n_tokens), tn_in=256, tn_out=512, tf=1024)


def _rope_tables(seq):
    inv_freq = ROPE_THETA ** (-jnp.arange(ROPE_HALF, dtype=F32) * 2.0 / HEAD_DIM)
    ang = jnp.arange(seq, dtype=F32)[:, None] * inv_freq[None, :]
    cos, sin = jnp.cos(ang), jnp.sin(ang)
    cos_t = jnp.tile(cos, (1, PAIR // ROPE_HALF))
    sin_t = jnp.tile(jnp.concatenate([-sin, sin], axis=1), (1, PAIR // HEAD_DIM))
    return cos_t, sin_t


def _cast_specs(weights, layer, steps):
    in_specs, out_specs, out_shapes = [], [], []
    for w in weights:
        _, rows, cols = w.shape
        slab = rows // steps
        assert rows % steps == 0 and slab % BF16_SUBLANES == 0
        in_specs.append(pl.BlockSpec((None, slab, cols), lambda i: (layer, i, 0)))
        out_specs.append(pl.BlockSpec((slab, cols), lambda i: (i, 0)))
        out_shapes.append(jax.ShapeDtypeStruct((rows, cols), BF16))
    return in_specs, out_specs, out_shapes


def _cast_slabs(src_refs, dst_refs):
    for src, dst in zip(src_refs, dst_refs):
        dst[...] = src[...].astype(dst.dtype)


def _in_proj_kernel(x_ref, g_ref, w_ref, cos_ref, sin_ref, *refs, chunks, n_cast):
    o_ref, h_ref = refs[n_cast], refs[-1]
    _cast_slabs(refs[:n_cast], refs[n_cast + 1:-1])
    x = x_ref[...]
    ms = jnp.mean(x * x, axis=-1, keepdims=True)
    h_ref[...] = ((x * lax.rsqrt(ms + EPS)) * g_ref[...]).astype(BF16)
    tm = x.shape[0]
    lane = lax.broadcasted_iota(jnp.int32, (tm, PAIR), 1)
    first = (lane % HEAD_DIM) < ROPE_HALF
    for start, classes in chunks:
        width = len(classes) * PAIR
        acc = jnp.dot(h_ref[...], w_ref[:, start:start + width], preferred_element_type=F32)
        for t, (rope, scale) in enumerate(classes):
            a = acc[:, t * PAIR:(t + 1) * PAIR]
            if rope:
                partner = jnp.where(first, pltpu.roll(a, PAIR - ROPE_HALF, axis=1),
                                    pltpu.roll(a, ROPE_HALF, axis=1))
                a = a * cos_ref[...] + partner * sin_ref[...]
            if scale != 1.0:
                a = a * scale
            o_ref[:, start + t * PAIR:start + (t + 1) * PAIR] = a.astype(o_ref.dtype)


def _in_proj(xt, gain, w, cos_t, sin_t, to_cast, layer, lay, seq, tiles):
    n_tok, d = xt.shape
    in_w = lay["in_width"]
    tm, tn = tiles["tm"], tiles["tn_in"]
    steps = n_tok // tm
    cast_in, cast_out, cast_shapes = _cast_specs(to_cast, layer, steps)
    assert in_w % tn == 0 and tn % PAIR == 0
    rope = np.zeros(in_w // PAIR, bool)
    scale = np.ones(in_w // PAIR, np.float32)
    for name in ("qb", "kb", "qc", "kc"):
        o, s = lay[name]
        rope[o // PAIR:(o + s) // PAIR] = True
    for name in ("qa", "qb", "qc"):
        o, s = lay[name]
        scale[o // PAIR:(o + s) // PAIR] = Q_SCALE
    per = tn // PAIR
    chunks = tuple((c * tn, tuple((bool(rope[c * per + t]), float(scale[c * per + t])) for t in range(per)))
                   for c in range(in_w // tn))
    pos_blocks = seq // tm
    return pl.pallas_call(
        functools.partial(_in_proj_kernel, chunks=chunks, n_cast=len(to_cast)),
        grid=(steps,),
        in_specs=[
            pl.BlockSpec((tm, d), lambda i: (i, 0)),
            pl.BlockSpec((1, d), lambda i: (0, 0)),
            pl.BlockSpec((d, in_w), lambda i: (0, 0), pipeline_mode=pl.Buffered(1)),
            pl.BlockSpec((tm, PAIR), lambda i: (i % pos_blocks, 0)),
            pl.BlockSpec((tm, PAIR), lambda i: (i % pos_blocks, 0)),
        ] + cast_in,
        out_specs=[pl.BlockSpec((tm, in_w), lambda i: (i, 0))] + cast_out,
        out_shape=[jax.ShapeDtypeStruct((n_tok, in_w), BF16)] + cast_shapes,
        scratch_shapes=[pltpu.VMEM((tm, d), BF16)],
        compiler_params=_params("arbitrary"),
        name="in_proj",
    )(xt, gain.reshape(1, d), w, cos_t, sin_t, *to_cast)


def _sb_kernel(q_ref, k_ref, v_ref, u_ref, o_ref, carry_ref, acc_ref, *, tq):
    i = pl.program_id(2)
    ch = SB_CHUNK
    n_pairs = q_ref.shape[2] // PAIR

    def lanes(p):
        return slice(p * PAIR, (p + 1) * PAIR)

    lo = _lane_index((tq, PAIR), BF16) < HEAD_DIM
    qs = []
    for p in range(n_pairs):
        q = q_ref[0, :, lanes(p)]
        zq = jnp.zeros_like(q)
        qs.append(jnp.concatenate([jnp.where(lo, q, zq), jnp.where(lo, zq, q)], axis=0))
    u = u_ref[...]
    row = lax.broadcasted_iota(jnp.int32, (2 * tq, tq), 0) % tq
    col = lax.broadcasted_iota(jnp.int32, (2 * tq, tq), 1)
    past = col < row

    def tiles(off, n_tiles, carries, accs, diagonal):
        nk = n_tiles * tq
        kblk = k_ref[0, pl.ds(off, nk), :]
        vblk = v_ref[0, pl.ds(off, nk), :]
        zs = [lax.dot_general(qs[p], kblk[:, lanes(p)], NT_DIMS, preferred_element_type=F32)
              for p in range(n_pairs)]
        sps = [jnp.maximum(z, 0.0) + jnp.log(1.0 + jnp.exp(-jnp.abs(z))) for z in zs]
        ws = [[None] * (nk // ch) for _ in range(n_pairs)]
        carries = list(carries)
        for c in reversed(range(nk // ch)):
            sl = slice(c * ch, (c + 1) * ch)
            masked = diagonal and c * ch >= nk - tq
            mask = past[:, c * ch - (nk - tq):(c + 1) * ch - (nk - tq)] if masked else None
            for p in range(n_pairs):
                s_c = jnp.where(mask, sps[p][:, sl], 0.0) if masked else sps[p][:, sl]
                hi = s_c.astype(BF16)
                lo_part = (s_c - hi.astype(F32)).astype(BF16)
                r = jnp.dot(jnp.concatenate([hi, lo_part], axis=1), u, preferred_element_type=F32)
                w = jnp.exp(zs[p][:, sl] - sps[p][:, sl] - r[:, :ch] - carries[p])
                if masked:
                    w = jnp.where(mask, w, 0.0)
                ws[p][c] = w.astype(BF16)
                carries[p] = carries[p] + r[:, ch:]
        lo_v = _lane_index((nk, PAIR), BF16) < HEAD_DIM
        accs = list(accs)
        for p in range(n_pairs):
            wb = jnp.concatenate(ws[p], axis=1)
            wcat = jnp.concatenate([wb[:tq], wb[tq:]], axis=1)
            vp = vblk[:, lanes(p)]
            zv = jnp.zeros_like(vp)
            vcat = jnp.concatenate([jnp.where(lo_v, vp, zv), jnp.where(lo_v, zv, vp)], axis=0)
            accs[p] = accs[p] + jnp.dot(wcat, vcat, preferred_element_type=F32)
        return carries, accs

    zero_carry = [jnp.zeros((2 * tq, PAIR), F32)] * n_pairs
    zero_acc = [jnp.zeros((tq, PAIR), F32)] * n_pairs

    def first_pass(off, n_tiles):
        carries, accs = tiles(off, n_tiles, zero_carry, zero_acc, True)
        for p in range(n_pairs):
            carry_ref[p] = carries[p]
            acc_ref[p] = accs[p]

    @pl.when(i == 0)
    def _():
        first_pass(0, 1)

    @pl.when(i > 0)
    def _():
        first_pass(pl.multiple_of((i - 1) * tq, tq), 2)

    def unfinished(carries):
        return functools.reduce(jnp.minimum, [jnp.min(c) for c in carries]) < SB_UNDERFLOW

    def cond(state):
        return (state[0] < i - 1) & state[1]

    def body(state):
        t = state[0]
        off = pl.multiple_of((i - 2 - t) * tq, tq)
        carries, accs = tiles(off, 1, state[2:2 + n_pairs], state[2 + n_pairs:], False)
        return (t + 1, unfinished(carries), *carries, *accs)

    carries = [carry_ref[p] for p in range(n_pairs)]
    accs = [acc_ref[p] for p in range(n_pairs)]
    state = lax.while_loop(cond, body, (jnp.int32(0), unfinished(carries), *carries, *accs))
    for p in range(n_pairs):
        o_ref[0, :, lanes(p)] = state[2 + n_pairs + p].astype(o_ref.dtype)


def _sb_attention(proj, lay):
    b, seq, _ = proj.shape
    tq = min(SB_TILE, seq)
    ch = SB_CHUNK
    assert seq % tq == 0 and tq % ch == 0
    q_off, width = lay["qa"]
    k_off, v_off = lay["ka"][0], lay["va"][0]
    n_pairs = SB_PAIRS_PER_STEP
    lw = n_pairs * PAIR
    assert width % lw == 0 and q_off % lw == 0 and k_off % lw == 0 and v_off % lw == 0
    tri = np.tril(np.ones((ch, ch), np.float32), -1)
    uu = np.concatenate([tri, np.ones((ch, PAIR), np.float32)], axis=1)
    uu = jnp.asarray(np.concatenate([uu, uu], axis=0), dtype=BF16)
    return pl.pallas_call(
        functools.partial(_sb_kernel, tq=tq),
        grid=(b, width // lw, seq // tq),
        in_specs=[
            pl.BlockSpec((1, tq, lw), lambda bi, p, i: (bi, i, q_off // lw + p)),
            pl.BlockSpec((1, seq, lw), lambda bi, p, i: (bi, 0, k_off // lw + p)),
            pl.BlockSpec((1, seq, lw), lambda bi, p, i: (bi, 0, v_off // lw + p)),
            pl.BlockSpec((2 * ch, ch + PAIR), lambda bi, p, i: (0, 0)),
        ],
        out_specs=pl.BlockSpec((1, tq, lw), lambda bi, p, i: (bi, i, p)),
        out_shape=jax.ShapeDtypeStruct((b, seq, width), BF16),
        scratch_shapes=[pltpu.VMEM((n_pairs, 2 * tq, PAIR), F32), pltpu.VMEM((n_pairs, tq, PAIR), F32)],
        compiler_params=_params("parallel", "parallel", "arbitrary"),
        name="sb_attention",
    )(proj, proj, proj, uu)


def _moba_kernel(q_ref, k_ref, v_ref, o_ref, kmean_ref, *, nb, nbp, group):
    i = pl.program_id(2)
    blk = MOBA_BLOCK
    tq = blk
    n_pairs = q_ref.shape[2] // PAIR

    def lanes(p):
        return slice(p * PAIR, (p + 1) * PAIR)

    @pl.when(i == 0)
    def _():
        kf = k_ref[0].astype(F32).reshape(nb, blk, n_pairs * PAIR)
        km = jnp.sum(kf, axis=1) * (1.0 / blk)
        if nbp > nb:
            km = jnp.concatenate([km, jnp.zeros((nbp - nb, n_pairs * PAIR), F32)], axis=0)
        kmean_ref[...] = km

    lo_qb = _lane_index((tq, PAIR), BF16) < HEAD_DIM
    lo_m = _lane_index((nbp, PAIR)) < HEAD_DIM
    jidx = lax.broadcasted_iota(jnp.int32, (nbp, tq), 0)
    valid = jidx < i

    def gated_queries(q, km):
        zq = jnp.zeros_like(q)
        plain, aug = [], []
        for h in (0, 1):
            head_m = lo_m if h == 0 else jnp.logical_not(lo_m)
            head_q = lo_qb if h == 0 else jnp.logical_not(lo_qb)
            kmh = jnp.where(head_m, km, 0.0)
            a = kmh.astype(BF16)
            r1 = kmh - a.astype(F32)
            b2 = r1.astype(BF16)
            c3 = (r1 - b2.astype(F32)).astype(BF16)
            g3 = lax.dot_general(jnp.concatenate([a, b2, c3], axis=0), q, NT_DIMS,
                                 preferred_element_type=F32)
            gate = g3[:nbp] + g3[nbp:2 * nbp] + g3[2 * nbp:]
            gate = jnp.where(valid, gate, -jnp.inf)
            beaten_by = jnp.zeros((nbp, tq), jnp.int32)
            for jp in range(nb):
                other = gate[jp:jp + 1, :]
                beats = (other > gate) | ((other == gate) & (jidx > jp))
                beaten_by = beaten_by + beats.astype(jnp.int32)
            sel = valid & (beaten_by < MOBA_TOPK)
            bias_t = jnp.where(sel, 0.0, NEG)
            top = HEAD_DIM if h == 0 else 0
            pieces = [jnp.zeros((top, tq), F32)] if top else []
            pieces.append(bias_t)
            if PAIR - top - nbp:
                pieces.append(jnp.zeros((PAIR - top - nbp, tq), F32))
            placed = jnp.concatenate(pieces, axis=0).T
            plain.append(jnp.where(head_q, q, zq))
            aug.append(jnp.where(head_q, q, placed.astype(BF16)))
        return plain, aug

    q_plain, q_aug = [], []
    for p in range(n_pairs):
        plain, aug = gated_queries(q_ref[0, :, lanes(p)], kmean_ref[:, lanes(p)])
        q_plain.append(plain)
        q_aug.append(aug)

    lo_q = _lane_index((tq, PAIR)) < HEAD_DIM
    lo_q2 = jnp.concatenate([lo_q, lo_q], axis=1)

    def attend(s0, s1, vblk, m0, m1, acc):
        n0 = jnp.maximum(m0, jnp.max(s0, axis=1, keepdims=True))
        n1 = jnp.maximum(m1, jnp.max(s1, axis=1, keepdims=True))
        p0 = jnp.exp(s0 - n0).astype(BF16)
        p1 = jnp.exp(s1 - n1).astype(BF16)
        alpha = jnp.where(lo_q, jnp.exp(m0 - n0), jnp.exp(m1 - n1))
        rhs = jnp.concatenate([vblk, jnp.ones_like(vblk)], axis=1)
        u0 = jnp.dot(p0, rhs, preferred_element_type=F32)
        u1 = jnp.dot(p1, rhs, preferred_element_type=F32)
        acc = acc * jnp.concatenate([alpha, alpha], axis=1) + jnp.where(lo_q2, u0, u1)
        return n0, n1, acc

    own_off = pl.multiple_of(i * blk, blk)
    k_own = k_ref[0, pl.ds(own_off, blk), :]
    v_own = v_ref[0, pl.ds(own_off, blk), :]
    row = lax.broadcasted_iota(jnp.int32, (tq, blk), 0)
    col = lax.broadcasted_iota(jnp.int32, (tq, blk), 1)
    causal = col <= row
    m_init = jnp.full((tq, 1), NEG, F32)
    state = []
    for p in range(n_pairs):
        s_own = [jnp.where(causal, lax.dot_general(qh, k_own[:, lanes(p)], NT_DIMS,
                                                   preferred_element_type=F32), NEG)
                 for qh in q_plain[p]]
        state += attend(s_own[0], s_own[1], v_own[:, lanes(p)],
                        m_init, m_init, jnp.zeros((tq, 2 * PAIR), F32))

    gk = group * blk
    lane_g = _lane_index((gk, PAIR), BF16)
    lo_g = lane_g < HEAD_DIM
    blk_in_group = (lax.broadcasted_iota(jnp.int32, (gk, PAIR), 0) // blk).astype(F32).astype(BF16)
    one = jnp.ones((gk, PAIR), BF16)
    zk = jnp.zeros((gk, PAIR), BF16)

    def body(g, st):
        o = pl.multiple_of(g * gk, gk)
        kg = k_ref[0, pl.ds(o, gk), :]
        vg = v_ref[0, pl.ds(o, gk), :]
        first = jnp.full((1, PAIR), g * group, jnp.int32).astype(F32).astype(BF16)
        blk_id = blk_in_group + first
        ind0 = jnp.where(lane_g == blk_id + HEAD_DIM, one, zk)
        ind1 = jnp.where(lane_g == blk_id, one, zk)
        scores = []
        for p in range(n_pairs):
            kp = kg[:, lanes(p)]
            scores.append((
                lax.dot_general(q_aug[p][0], jnp.where(lo_g, kp, ind0), NT_DIMS, preferred_element_type=F32),
                lax.dot_general(q_aug[p][1], jnp.where(lo_g, ind1, kp), NT_DIMS, preferred_element_type=F32)))
        new = []
        for p in range(n_pairs):
            new += attend(scores[p][0], scores[p][1], vg[:, lanes(p)], *st[3 * p:3 * p + 3])
        return tuple(new)

    state = lax.fori_loop(0, (i + group - 1) // group, body, tuple(state))
    for p in range(n_pairs):
        acc = state[3 * p + 2]
        o_ref[0, :, lanes(p)] = (acc[:, :PAIR] / acc[:, PAIR:]).astype(o_ref.dtype)


def _moba_attention(proj, lay):
    b, seq, _ = proj.shape
    tq = MOBA_BLOCK
    assert seq % MOBA_BLOCK == 0
    nb = seq // MOBA_BLOCK
    nbp = -(-nb // 8) * 8
    group = min(MOBA_GROUP, nb)
    assert nbp <= HEAD_DIM
    assert nb % group == 0
    q_off, width = lay["qb"]
    k_off, v_off = lay["kb"][0], lay["vb"][0]
    lw = MOBA_PAIRS_PER_STEP * PAIR
    assert width % lw == 0 and q_off % lw == 0 and k_off % lw == 0 and v_off % lw == 0
    return pl.pallas_call(
        functools.partial(_moba_kernel, nb=nb, nbp=nbp, group=group),
        grid=(b, width // lw, seq // tq),
        in_specs=[
            pl.BlockSpec((1, tq, lw), lambda bi, p, i: (bi, i, q_off // lw + p)),
            pl.BlockSpec((1, seq, lw), lambda bi, p, i: (bi, 0, k_off // lw + p)),
            pl.BlockSpec((1, seq, lw), lambda bi, p, i: (bi, 0, v_off // lw + p)),
        ],
        out_specs=pl.BlockSpec((1, tq, lw), lambda bi, p, i: (bi, i, p)),
        out_shape=jax.ShapeDtypeStruct((b, seq, width), BF16),
        scratch_shapes=[pltpu.VMEM((nbp, lw), F32)],
        compiler_params=_params("parallel", "parallel", "arbitrary"),
        name="moba_attention",
    )(proj, proj, proj)


def _swa_kernel(sink_ref, q_ref, kp_ref, kc_ref, vp_ref, vc_ref, o_ref, *, n_pairs):
    n = pl.program_id(1)
    w = WINDOW
    k = jnp.concatenate([kp_ref[0], kc_ref[0]], axis=0).astype(F32)
    v = jnp.concatenate([vp_ref[0], vc_ref[0]], axis=0).astype(F32)
    lo_k = lax.broadcasted_iota(jnp.int32, (2 * w, PAIR), 1) < HEAD_DIM
    k_sw = pltpu.roll(k, HEAD_DIM, axis=1)
    v_sw = pltpu.roll(v, HEAD_DIM, axis=1)
    kk = [jnp.where(lo_k, k, k_sw).astype(BF16), jnp.where(lo_k, k_sw, k).astype(BF16)]
    vv = [jnp.where(lo_k, v, v_sw).astype(BF16), jnp.where(lo_k, v_sw, v).astype(BF16)]
    ones = jnp.ones((2 * w, PAIR), BF16)
    rhs = [jnp.concatenate([vg, ones], axis=1) for vg in vv]

    row = lax.broadcasted_iota(jnp.int32, (w, 2 * w), 0)
    col = lax.broadcasted_iota(jnp.int32, (w, 2 * w), 1)
    delta = row + w - col
    valid = (delta >= 0) & (delta < w) & ((n - 1) * w + col >= 0)
    lo_q = _lane_index((w, PAIR)) < HEAD_DIM
    lo_qb = _lane_index((w, PAIR), BF16) < HEAD_DIM
    pairs_per_kv = SWA_GROUP // 2

    for p in range(n_pairs):
        g = p // pairs_per_kv
        qp = q_ref[0, :, p * PAIR:(p + 1) * PAIR]
        zq = jnp.zeros_like(qp)
        outs = []
        for h in (0, 1):
            qh = jnp.where(lo_qb, qp, zq) if h == 0 else jnp.where(lo_qb, zq, qp)
            s = lax.dot_general(qh, kk[g], NT_DIMS, preferred_element_type=F32)
            s = jnp.where(valid, s, NEG)
            sink = sink_ref[2 * p + h]
            m = jnp.maximum(jnp.max(s, axis=1, keepdims=True), sink)
            pr = jnp.exp(s - m).astype(BF16)
            o2 = jnp.dot(pr, rhs[g], preferred_element_type=F32)
            outs.append(o2[:, :PAIR] / (o2[:, PAIR:] + jnp.exp(sink - m)))
        o_ref[0, :, p * PAIR:(p + 1) * PAIR] = jnp.where(lo_q, outs[0], outs[1]).astype(o_ref.dtype)


def _swa_attention(proj, sinks, lay):
    b, seq, _ = proj.shape
    w = WINDOW
    q_off, width = lay["qc"]
    k_off, v_off = lay["kc"][0], lay["vc"][0]
    assert q_off % width == 0 and seq % w == 0
    return pl.pallas_call(
        functools.partial(_swa_kernel, n_pairs=width // PAIR),
        grid=(b, seq // w),
        in_specs=[
            pl.BlockSpec(memory_space=pltpu.SMEM),
            pl.BlockSpec((1, w, width), lambda bi, n: (bi, n, q_off // width)),
            pl.BlockSpec((1, w, PAIR), lambda bi, n: (bi, jnp.maximum(n - 1, 0), k_off // PAIR)),
            pl.BlockSpec((1, w, PAIR), lambda bi, n: (bi, n, k_off // PAIR)),
            pl.BlockSpec((1, w, PAIR), lambda bi, n: (bi, jnp.maximum(n - 1, 0), v_off // PAIR)),
            pl.BlockSpec((1, w, PAIR), lambda bi, n: (bi, n, v_off // PAIR)),
        ],
        out_specs=pl.BlockSpec((1, w, width), lambda bi, n: (bi, n, 0)),
        out_shape=jax.ShapeDtypeStruct((b, seq, width), BF16),
        compiler_params=_params("parallel", "arbitrary"),
        name="swa_attention",
    )(sinks.astype(F32), proj, proj, proj, proj, proj)


def _out_proj_kernel(ya_ref, yb_ref, yc_ref, ga_ref, gb_ref, gc_ref, w_ref, x_ref, *refs, tn, n_cast):
    o_ref, mix_ref = refs[n_cast], refs[-1]
    _cast_slabs(refs[:n_cast], refs[n_cast + 1:-1])
    start = 0
    for y_ref, g_ref in ((ya_ref, ga_ref), (yb_ref, gb_ref), (yc_ref, gc_ref)):
        y = y_ref[...].astype(F32)
        ms = jnp.mean(y * y, axis=-1, keepdims=True)
        width = y.shape[1]
        mix_ref[:, start:start + width] = ((y * lax.rsqrt(ms + EPS)) * g_ref[...]).astype(BF16)
        start += width
    for c in range(o_ref.shape[1] // tn):
        cols = slice(c * tn, (c + 1) * tn)
        o_ref[:, cols] = x_ref[:, cols] + jnp.dot(mix_ref[...], w_ref[:, cols],
                                                  preferred_element_type=F32)


def _out_proj(ya, yb, yc, ga, gb, gc, w, xt, to_cast, layer, tiles):
    n_tok, d = xt.shape
    tm, tn = tiles["tm"], tiles["tn_out"]
    wa, wb, wc = ya.shape[1], yb.shape[1], yc.shape[1]
    mix_w = wa + wb + wc
    steps = n_tok // tm
    cast_in, cast_out, cast_shapes = _cast_specs(to_cast, layer, steps)
    assert d % tn == 0 and w.shape == (mix_w, d)
    return pl.pallas_call(
        functools.partial(_out_proj_kernel, tn=tn, n_cast=len(to_cast)),
        grid=(steps,),
        in_specs=[
            pl.BlockSpec((tm, wa), lambda i: (i, 0)),
            pl.BlockSpec((tm, wb), lambda i: (i, 0)),
            pl.BlockSpec((tm, wc), lambda i: (i, 0)),
            pl.BlockSpec((1, wa), lambda i: (0, 0)),
            pl.BlockSpec((1, wb), lambda i: (0, 0)),
            pl.BlockSpec((1, wc), lambda i: (0, 0)),
            pl.BlockSpec((mix_w, d), lambda i: (0, 0), pipeline_mode=pl.Buffered(1)),
            pl.BlockSpec((tm, d), lambda i: (i, 0)),
        ] + cast_in,
        out_specs=[pl.BlockSpec((tm, d), lambda i: (i, 0))] + cast_out,
        out_shape=[jax.ShapeDtypeStruct((n_tok, d), F32)] + cast_shapes,
        scratch_shapes=[pltpu.VMEM((tm, mix_w), BF16)],
        compiler_params=_params("arbitrary"),
        name="out_proj",
    )(ya, yb, yc, ga.reshape(1, wa), gb.reshape(1, wb), gc.reshape(1, wc), w, xt, *to_cast)


def _mlp_kernel(x_ref, g_ref, wu_ref, wd_ref, gf_ref, o_ref, h_ref, *, final):
    f = pl.program_id(1)

    @pl.when(f == 0)
    def _():
        x = x_ref[...]
        ms = jnp.mean(x * x, axis=-1, keepdims=True)
        h_ref[...] = ((x * lax.rsqrt(ms + EPS)) * g_ref[...]).astype(BF16)
        o_ref[...] = x

    u = jnp.maximum(jnp.dot(h_ref[...], wu_ref[...], preferred_element_type=F32), 0.0)
    o_ref[...] += jnp.dot((u * u).astype(BF16), wd_ref[...], preferred_element_type=F32)

    if final:
        @pl.when(f == pl.num_programs(1) - 1)
        def _():
            y = o_ref[...]
            ms = jnp.mean(y * y, axis=-1, keepdims=True)
            o_ref[...] = (y * lax.rsqrt(ms + EPS)) * gf_ref[...]


def _mlp(xt, gain, w_up, w_down, final_gain, tiles, final):
    n_tok, d = xt.shape
    d_ff = w_up.shape[1]
    tm, tf = tiles["tm_mlp"], tiles["tf"]
    assert d_ff % tf == 0 and n_tok % tm == 0
    return pl.pallas_call(
        functools.partial(_mlp_kernel, final=final),
        grid=(n_tok // tm, d_ff // tf),
        in_specs=[
            pl.BlockSpec((tm, d), lambda i, f: (i, 0)),
            pl.BlockSpec((1, d), lambda i, f: (0, 0)),
            pl.BlockSpec((d, tf), lambda i, f: (0, f)),
            pl.BlockSpec((tf, d), lambda i, f: (f, 0)),
            pl.BlockSpec((1, d), lambda i, f: (0, 0)),
        ],
        out_specs=pl.BlockSpec((tm, d), lambda i, f: (i, 0)),
        out_shape=jax.ShapeDtypeStruct((n_tok, d), F32),
        scratch_shapes=[pltpu.VMEM((tm, d), BF16)],
        compiler_params=_params("parallel", "arbitrary"),
        name="mlp",
    )(xt, gain.reshape(1, d), w_up, w_down, final_gain.reshape(1, d))


def _mlp_up_kernel(x_ref, g_ref, wu_ref, a_ref, h_ref):
    @pl.when(pl.program_id(1) == 0)
    def _():
        x = x_ref[...]
        ms = jnp.mean(x * x, axis=-1, keepdims=True)
        h_ref[...] = ((x * lax.rsqrt(ms + EPS)) * g_ref[...]).astype(BF16)

    u = jnp.maximum(jnp.dot(h_ref[...], wu_ref[...], preferred_element_type=F32), 0.0)
    a_ref[...] = (u * u).astype(a_ref.dtype)


def _mlp_down_kernel(a_ref, wd_ref, x_ref, o_ref):
    o_ref[...] = x_ref[...] + jnp.dot(a_ref[...], wd_ref[...], preferred_element_type=F32)


def _mlp_two_calls(xt, gain, w_up, w_down, tiles):
    n_tok, d = xt.shape
    d_ff = w_up.shape[1]
    tm_up, tf = tiles["tm_up"], tiles["tf"]
    tm, tn = tiles["tm"], tiles["tn_out"]
    assert n_tok % tm_up == 0 and d_ff % tf == 0 and n_tok % tm == 0 and d % tn == 0
    act = pl.pallas_call(
        _mlp_up_kernel,
        grid=(n_tok // tm_up, d_ff // tf),
        in_specs=[
            pl.BlockSpec((tm_up, d), lambda i, f: (i, 0)),
            pl.BlockSpec((1, d), lambda i, f: (0, 0)),
            pl.BlockSpec((d, tf), lambda i, f: (0, f)),
        ],
        out_specs=pl.BlockSpec((tm_up, tf), lambda i, f: (i, f)),
        out_shape=jax.ShapeDtypeStruct((n_tok, d_ff), BF16),
        scratch_shapes=[pltpu.VMEM((tm_up, d), BF16)],
        compiler_params=_params("parallel", "arbitrary"),
        name="mlp_up",
    )(xt, gain.reshape(1, d), w_up)
    return pl.pallas_call(
        _mlp_down_kernel,
        grid=(n_tok // tm, d // tn),
        in_specs=[
            pl.BlockSpec((tm, d_ff), lambda i, j: (i, 0)),
            pl.BlockSpec((d_ff, tn), lambda i, j: (0, j)),
            pl.BlockSpec((tm, tn), lambda i, j: (i, j)),
        ],
        out_specs=pl.BlockSpec((tm, tn), lambda i, j: (i, j)),
        out_shape=jax.ShapeDtypeStruct((n_tok, d), F32),
        compiler_params=_params("parallel", "arbitrary"),
        name="mlp_down",
    )(act, w_down, xt)


def kernel(x, attn_norm, w_in, sinks, gn_sb, gn_moba, gn_swa, w_out, mlp_norm, w_up, w_down, final_norm):
    b, seq, d = x.shape
    depth = w_in.shape[0]
    n_tok = b * seq
    lay = _layout(d)
    tiles = _tiles(n_tok, seq)
    cos_t, sin_t = _rope_tables(seq)
    xt = x.reshape(n_tok, d)
    w_in_l = w_in[0].astype(BF16)
    for l in range(depth):
        proj, w_up_l, w_down_l, w_out_l = _in_proj(xt, attn_norm[l], w_in_l, cos_t, sin_t,
                                                   (w_up, w_down, w_out), l, lay, seq, tiles)
        proj = proj.reshape(b, seq, lay["in_width"])
        ya = _sb_attention(proj, lay).reshape(n_tok, -1)
        yb = _moba_attention(proj, lay).reshape(n_tok, -1)
        yc = _swa_attention(proj, sinks[l], lay).reshape(n_tok, -1)
        next_w_in = (w_in,) if l + 1 < depth else ()
        xt, *cast = _out_proj(ya, yb, yc, gn_sb[l], gn_moba[l], gn_swa[l], w_out_l, xt,
                              next_w_in, l + 1, tiles)
        if cast:
            w_in_l = cast[0]
        if l == depth - 1:
            xt = _mlp(xt, mlp_norm[l], w_up_l, w_down_l, final_norm, tiles, final=True)
        else:
            xt = _mlp_two_calls(xt, mlp_norm[l], w_up_l, w_down_l, tiles)
    return xt.reshape(b, seq, d)
```

```python
import functools

import numpy as np
import jax
import jax.numpy as jnp
from jax import lax
from jax.experimental import pallas as pl
from jax.experimental.pallas import tpu as pltpu

F32 = jnp.float32
BF16 = jnp.bfloat16

HEAD_DIM = 64
PAIR = 2 * HEAD_DIM
BF16_SUBLANES = 16
ROPE_HALF = HEAD_DIM // 2
MOBA_BLOCK = 256
MOBA_TOPK = 3
WINDOW = 128
SWA_GROUP = 8
ROPE_THETA = 10000.0
EPS = 1e-6
NEG = -1e30
Q_SCALE = HEAD_DIM ** -0.5
MOBA_GROUP = 4
MOBA_PAIRS_PER_STEP = 4
SB_TILE = 256
SB_CHUNK = PAIR
SB_PAIRS_PER_STEP = 4
SB_UNDERFLOW = 104.0

VMEM_LIMIT_BYTES = 56 * 1024 * 1024

NT_DIMS = (((1,), (1,)), ((), ()))


def _params(*semantics):
    return pltpu.CompilerParams(dimension_semantics=semantics,
                                vmem_limit_bytes=VMEM_LIMIT_BYTES)


def _lane_index(shape, dtype=jnp.int32):
    idx = lax.broadcasted_iota(jnp.int32, shape, len(shape) - 1)
    return idx if dtype == jnp.int32 else idx.astype(F32).astype(dtype)


def _layout(d_model):
    sb = d_model // 4
    moba = d_model // 4
    swa_q = d_model // 2
    swa_kv = (swa_q // HEAD_DIM // SWA_GROUP) * HEAD_DIM
    sizes = (sb, sb, sb, moba, moba, moba, swa_q, swa_kv, swa_kv)
    offs = np.concatenate([[0], np.cumsum(sizes)]).astype(int)
    names = ("qa", "ka", "va", "qb", "kb", "vb", "qc", "kc", "vc")
    lay = {n: (int(offs[i]), int(sizes[i])) for i, n in enumerate(names)}
    lay["in_width"] = int(offs[-1])
    assert sb % PAIR == 0 and swa_q % PAIR == 0 and swa_kv == PAIR
    return lay


def _tiles(n_tokens, seq):
    tm = min(512, seq)
    assert seq % tm == 0 and n_tokens % tm == 0
    return dict(tm=tm, tm_mlp=min(512, n_tokens), tn_in=256, tn_out=512, tf=1024)


def _rope_tables(seq):
    inv_freq = ROPE_THETA ** (-jnp.arange(ROPE_HALF, dtype=F32) * 2.0 / HEAD_DIM)
    ang = jnp.arange(seq, dtype=F32)[:, None] * inv_freq[None, :]
    cos, sin = jnp.cos(ang), jnp.sin(ang)
    cos_t = jnp.tile(cos, (1, PAIR // ROPE_HALF))
    sin_t = jnp.tile(jnp.concatenate([-sin, sin], axis=1), (1, PAIR // HEAD_DIM))
    return cos_t, sin_t


def _cast_specs(weights, layer, steps):
    in_specs, out_specs, out_shapes = [], [], []
    for w in weights:
        _, rows, cols = w.shape
        slab = rows // steps
        assert rows % steps == 0 and slab % BF16_SUBLANES == 0
        in_specs.append(pl.BlockSpec((None, slab, cols), lambda i: (layer, i, 0)))
        out_specs.append(pl.BlockSpec((slab, cols), lambda i: (i, 0)))
        out_shapes.append(jax.ShapeDtypeStruct((rows, cols), BF16))
    return in_specs, out_specs, out_shapes


def _cast_slabs(src_refs, dst_refs):
    for src, dst in zip(src_refs, dst_refs):
        dst[...] = src[...].astype(dst.dtype)


def _in_proj_kernel(x_ref, g_ref, w_ref, cos_ref, sin_ref, *refs, chunks, n_cast):
    o_ref, h_ref = refs[n_cast], refs[-1]
    _cast_slabs(refs[:n_cast], refs[n_cast + 1:-1])
    x = x_ref[...]
    ms = jnp.mean(x * x, axis=-1, keepdims=True)
    h_ref[...] = ((x * lax.rsqrt(ms + EPS)) * g_ref[...]).astype(BF16)
    tm = x.shape[0]
    lane = lax.broadcasted_iota(jnp.int32, (tm, PAIR), 1)
    first = (lane % HEAD_DIM) < ROPE_HALF
    for start, classes in chunks:
        width = len(classes) * PAIR
        acc = jnp.dot(h_ref[...], w_ref[:, start:start + width], preferred_element_type=F32)
        for t, (rope, scale) in enumerate(classes):
            a = acc[:, t * PAIR:(t + 1) * PAIR]
            if rope:
                partner = jnp.where(first, pltpu.roll(a, PAIR - ROPE_HALF, axis=1),
                                    pltpu.roll(a, ROPE_HALF, axis=1))
                a = a * cos_ref[...] + partner * sin_ref[...]
            if scale != 1.0:
                a = a * scale
            o_ref[:, start + t * PAIR:start + (t + 1) * PAIR] = a.astype(o_ref.dtype)


def _in_proj(xt, gain, w, cos_t, sin_t, to_cast, layer, lay, seq, tiles):
    n_tok, d = xt.shape
    in_w = lay["in_width"]
    tm, tn = tiles["tm"], tiles["tn_in"]
    steps = n_tok // tm
    cast_in, cast_out, cast_shapes = _cast_specs(to_cast, layer, steps)
    assert in_w % tn == 0 and tn % PAIR == 0
    rope = np.zeros(in_w // PAIR, bool)
    scale = np.ones(in_w // PAIR, np.float32)
    for name in ("qb", "kb", "qc", "kc"):
        o, s = lay[name]
        rope[o // PAIR:(o + s) // PAIR] = True
    for name in ("qa", "qb", "qc"):
        o, s = lay[name]
        scale[o // PAIR:(o + s) // PAIR] = Q_SCALE
    per = tn // PAIR
    chunks = tuple((c * tn, tuple((bool(rope[c * per + t]), float(scale[c * per + t])) for t in range(per)))
                   for c in range(in_w // tn))
    pos_blocks = seq // tm
    return pl.pallas_call(
        functools.partial(_in_proj_kernel, chunks=chunks, n_cast=len(to_cast)),
        grid=(steps,),
        in_specs=[
            pl.BlockSpec((tm, d), lambda i: (i, 0)),
            pl.BlockSpec((1, d), lambda i: (0, 0)),
            pl.BlockSpec((d, in_w), lambda i: (0, 0), pipeline_mode=pl.Buffered(1)),
            pl.BlockSpec((tm, PAIR), lambda i: (i % pos_blocks, 0)),
            pl.BlockSpec((tm, PAIR), lambda i: (i % pos_blocks, 0)),
        ] + cast_in,
        out_specs=[pl.BlockSpec((tm, in_w), lambda i: (i, 0))] + cast_out,
        out_shape=[jax.ShapeDtypeStruct((n_tok, in_w), BF16)] + cast_shapes,
        scratch_shapes=[pltpu.VMEM((tm, d), BF16)],
        compiler_params=pltpu.CompilerParams(
            dimension_semantics=("arbitrary",), vmem_limit_bytes=VMEM_LIMIT_BYTES,
            allow_input_fusion=[False, False, True, False, False] + [False] * len(to_cast)),
        name="in_proj",
    )(xt, gain.reshape(1, d), w, cos_t, sin_t, *to_cast)


def _sb_kernel(q_ref, k_ref, v_ref, u_ref, o_ref, carry_ref, acc_ref, *, tq):
    i = pl.program_id(2)
    ch = SB_CHUNK
    n_pairs = q_ref.shape[2] // PAIR

    def lanes(p):
        return slice(p * PAIR, (p + 1) * PAIR)

    lo = _lane_index((tq, PAIR), BF16) < HEAD_DIM
    qs = []
    for p in range(n_pairs):
        q = q_ref[0, :, lanes(p)]
        zq = jnp.zeros_like(q)
        qs.append(jnp.concatenate([jnp.where(lo, q, zq), jnp.where(lo, zq, q)], axis=0))
    u = u_ref[...]
    row = lax.broadcasted_iota(jnp.int32, (2 * tq, tq), 0) % tq
    col = lax.broadcasted_iota(jnp.int32, (2 * tq, tq), 1)
    past = col < row

    def tiles(off, n_tiles, carries, accs, diagonal):
        nk = n_tiles * tq
        kblk = k_ref[0, pl.ds(off, nk), :]
        vblk = v_ref[0, pl.ds(off, nk), :]
        zs = [lax.dot_general(qs[p], kblk[:, lanes(p)], NT_DIMS, preferred_element_type=F32)
              for p in range(n_pairs)]
        sps = [jnp.maximum(z, 0.0) + jnp.log(1.0 + jnp.exp(-jnp.abs(z))) for z in zs]
        ws = [[None] * (nk // ch) for _ in range(n_pairs)]
        carries = list(carries)
        for c in reversed(range(nk // ch)):
            sl = slice(c * ch, (c + 1) * ch)
            masked = diagonal and c * ch >= nk - tq
            mask = past[:, c * ch - (nk - tq):(c + 1) * ch - (nk - tq)] if masked else None
            for p in range(n_pairs):
                s_c = jnp.where(mask, sps[p][:, sl], 0.0) if masked else sps[p][:, sl]
                hi = s_c.astype(BF16)
                lo_part = (s_c - hi.astype(F32)).astype(BF16)
                r = jnp.dot(jnp.concatenate([hi, lo_part], axis=1), u, preferred_element_type=F32)
                w = jnp.exp(zs[p][:, sl] - sps[p][:, sl] - r[:, :ch] - carries[p])
                if masked:
                    w = jnp.where(mask, w, 0.0)
                ws[p][c] = w.astype(BF16)
                carries[p] = carries[p] + r[:, ch:]
        lo_v = _lane_index((nk, PAIR), BF16) < HEAD_DIM
        accs = list(accs)
        for p in range(n_pairs):
            wb = jnp.concatenate(ws[p], axis=1)
            wcat = jnp.concatenate([wb[:tq], wb[tq:]], axis=1)
            vp = vblk[:, lanes(p)]
            zv = jnp.zeros_like(vp)
            vcat = jnp.concatenate([jnp.where(lo_v, vp, zv), jnp.where(lo_v, zv, vp)], axis=0)
            accs[p] = accs[p] + jnp.dot(wcat, vcat, preferred_element_type=F32)
        return carries, accs

    zero_carry = [jnp.zeros((2 * tq, PAIR), F32)] * n_pairs
    zero_acc = [jnp.zeros((tq, PAIR), F32)] * n_pairs

    def first_pass(off, n_tiles):
        carries, accs = tiles(off, n_tiles, zero_carry, zero_acc, True)
        for p in range(n_pairs):
            carry_ref[p] = carries[p]
            acc_ref[p] = accs[p]

    @pl.when(i == 0)
    def _():
        first_pass(0, 1)

    @pl.when(i > 0)
    def _():
        first_pass(pl.multiple_of((i - 1) * tq, tq), 2)

    def unfinished(carries):
        return functools.reduce(jnp.minimum, [jnp.min(c) for c in carries]) < SB_UNDERFLOW

    def cond(state):
        return (state[0] < i - 1) & state[1]

    def body(state):
        t = state[0]
        off = pl.multiple_of((i - 2 - t) * tq, tq)
        carries, accs = tiles(off, 1, state[2:2 + n_pairs], state[2 + n_pairs:], False)
        return (t + 1, unfinished(carries), *carries, *accs)

    carries = [carry_ref[p] for p in range(n_pairs)]
    accs = [acc_ref[p] for p in range(n_pairs)]
    state = lax.while_loop(cond, body, (jnp.int32(0), unfinished(carries), *carries, *accs))
    for p in range(n_pairs):
        o_ref[0, :, lanes(p)] = state[2 + n_pairs + p].astype(o_ref.dtype)


def _sb_attention(proj, lay):
    b, seq, _ = proj.shape
    tq = min(SB_TILE, seq)
    ch = SB_CHUNK
    assert seq % tq == 0 and tq % ch == 0
    q_off, width = lay["qa"]
    k_off, v_off = lay["ka"][0], lay["va"][0]
    n_pairs = SB_PAIRS_PER_STEP
    lw = n_pairs * PAIR
    assert width % lw == 0 and q_off % lw == 0 and k_off % lw == 0 and v_off % lw == 0
    tri = np.tril(np.ones((ch, ch), np.float32), -1)
    uu = np.concatenate([tri, np.ones((ch, PAIR), np.float32)], axis=1)
    uu = jnp.asarray(np.concatenate([uu, uu], axis=0), dtype=BF16)
    return pl.pallas_call(
        functools.partial(_sb_kernel, tq=tq),
        grid=(b, width // lw, seq // tq),
        in_specs=[
            pl.BlockSpec((1, tq, lw), lambda bi, p, i: (bi, i, q_off // lw + p)),
            pl.BlockSpec((1, seq, lw), lambda bi, p, i: (bi, 0, k_off // lw + p)),
            pl.BlockSpec((1, seq, lw), lambda bi, p, i: (bi, 0, v_off // lw + p)),
            pl.BlockSpec((2 * ch, ch + PAIR), lambda bi, p, i: (0, 0)),
        ],
        out_specs=pl.BlockSpec((1, tq, lw), lambda bi, p, i: (bi, i, p)),
        out_shape=jax.ShapeDtypeStruct((b, seq, width), BF16),
        scratch_shapes=[pltpu.VMEM((n_pairs, 2 * tq, PAIR), F32), pltpu.VMEM((n_pairs, tq, PAIR), F32)],
        compiler_params=_params("parallel", "parallel", "arbitrary"),
        name="sb_attention",
    )(proj, proj, proj, uu)


def _moba_kernel(q_ref, k_ref, v_ref, o_ref, kmean_ref, *, nb, nbp, group):
    i = pl.program_id(2)
    blk = MOBA_BLOCK
    tq = blk
    n_pairs = q_ref.shape[2] // PAIR

    def lanes(p):
        return slice(p * PAIR, (p + 1) * PAIR)

    @pl.when(i == 0)
    def _():
        kf = k_ref[0].astype(F32).reshape(nb, blk, n_pairs * PAIR)
        km = jnp.sum(kf, axis=1) * (1.0 / blk)
        if nbp > nb:
            km = jnp.concatenate([km, jnp.zeros((nbp - nb, n_pairs * PAIR), F32)], axis=0)
        kmean_ref[...] = km

    lo_qb = _lane_index((tq, PAIR), BF16) < HEAD_DIM
    lo_m = _lane_index((nbp, PAIR)) < HEAD_DIM
    jidx = lax.broadcasted_iota(jnp.int32, (nbp, tq), 0)
    valid = jidx < i

    def gated_queries(q, km):
        zq = jnp.zeros_like(q)
        plain, aug = [], []
        for h in (0, 1):
            head_m = lo_m if h == 0 else jnp.logical_not(lo_m)
            head_q = lo_qb if h == 0 else jnp.logical_not(lo_qb)
            kmh = jnp.where(head_m, km, 0.0)
            a = kmh.astype(BF16)
            r1 = kmh - a.astype(F32)
            b2 = r1.astype(BF16)
            c3 = (r1 - b2.astype(F32)).astype(BF16)
            g3 = lax.dot_general(jnp.concatenate([a, b2, c3], axis=0), q, NT_DIMS,
                                 preferred_element_type=F32)
            gate = g3[:nbp] + g3[nbp:2 * nbp] + g3[2 * nbp:]
            gate = jnp.where(valid, gate, -jnp.inf)
            beaten_by = jnp.zeros((nbp, tq), jnp.int32)
            for jp in range(nb):
                other = gate[jp:jp + 1, :]
                beats = (other > gate) | ((other == gate) & (jidx > jp))
                beaten_by = beaten_by + beats.astype(jnp.int32)
            sel = valid & (beaten_by < MOBA_TOPK)
            bias_t = jnp.where(sel, 0.0, NEG)
            top = HEAD_DIM if h == 0 else 0
            pieces = [jnp.zeros((top, tq), F32)] if top else []
            pieces.append(bias_t)
            if PAIR - top - nbp:
                pieces.append(jnp.zeros((PAIR - top - nbp, tq), F32))
            placed = jnp.concatenate(pieces, axis=0).T
            plain.append(jnp.where(head_q, q, zq))
            aug.append(jnp.where(head_q, q, placed.astype(BF16)))
        return plain, aug

    q_plain, q_aug = [], []
    for p in range(n_pairs):
        plain, aug = gated_queries(q_ref[0, :, lanes(p)], kmean_ref[:, lanes(p)])
        q_plain.append(plain)
        q_aug.append(aug)

    lo_q = _lane_index((tq, PAIR)) < HEAD_DIM
    lo_q2 = jnp.concatenate([lo_q, lo_q], axis=1)

    def attend(s0, s1, vblk, m0, m1, acc):
        n0 = jnp.maximum(m0, jnp.max(s0, axis=1, keepdims=True))
        n1 = jnp.maximum(m1, jnp.max(s1, axis=1, keepdims=True))
        p0 = jnp.exp(s0 - n0).astype(BF16)
        p1 = jnp.exp(s1 - n1).astype(BF16)
        alpha = jnp.where(lo_q, jnp.exp(m0 - n0), jnp.exp(m1 - n1))
        rhs = jnp.concatenate([vblk, jnp.ones_like(vblk)], axis=1)
        u0 = jnp.dot(p0, rhs, preferred_element_type=F32)
        u1 = jnp.dot(p1, rhs, preferred_element_type=F32)
        acc = acc * jnp.concatenate([alpha, alpha], axis=1) + jnp.where(lo_q2, u0, u1)
        return n0, n1, acc

    own_off = pl.multiple_of(i * blk, blk)
    k_own = k_ref[0, pl.ds(own_off, blk), :]
    v_own = v_ref[0, pl.ds(own_off, blk), :]
    row = lax.broadcasted_iota(jnp.int32, (tq, blk), 0)
    col = lax.broadcasted_iota(jnp.int32, (tq, blk), 1)
    causal = col <= row
    m_init = jnp.full((tq, 1), NEG, F32)
    state = []
    for p in range(n_pairs):
        s_own = [jnp.where(causal, lax.dot_general(qh, k_own[:, lanes(p)], NT_DIMS,
                                                   preferred_element_type=F32), NEG)
                 for qh in q_plain[p]]
        state += attend(s_own[0], s_own[1], v_own[:, lanes(p)],
                        m_init, m_init, jnp.zeros((tq, 2 * PAIR), F32))

    gk = group * blk
    lane_g = _lane_index((gk, PAIR), BF16)
    lo_g = lane_g < HEAD_DIM
    blk_in_group = (lax.broadcasted_iota(jnp.int32, (gk, PAIR), 0) // blk).astype(F32).astype(BF16)
    one = jnp.ones((gk, PAIR), BF16)
    zk = jnp.zeros((gk, PAIR), BF16)

    def body(g, st):
        o = pl.multiple_of(g * gk, gk)
        kg = k_ref[0, pl.ds(o, gk), :]
        vg = v_ref[0, pl.ds(o, gk), :]
        first = jnp.full((1, PAIR), g * group, jnp.int32).astype(F32).astype(BF16)
        blk_id = blk_in_group + first
        ind0 = jnp.where(lane_g == blk_id + HEAD_DIM, one, zk)
        ind1 = jnp.where(lane_g == blk_id, one, zk)
        scores = []
        for p in range(n_pairs):
            kp = kg[:, lanes(p)]
            scores.append((
                lax.dot_general(q_aug[p][0], jnp.where(lo_g, kp, ind0), NT_DIMS, preferred_element_type=F32),
                lax.dot_general(q_aug[p][1], jnp.where(lo_g, ind1, kp), NT_DIMS, preferred_element_type=F32)))
        new = []
        for p in range(n_pairs):
            new += attend(scores[p][0], scores[p][1], vg[:, lanes(p)], *st[3 * p:3 * p + 3])
        return tuple(new)

    state = lax.fori_loop(0, (i + group - 1) // group, body, tuple(state))
    for p in range(n_pairs):
        acc = state[3 * p + 2]
        o_ref[0, :, lanes(p)] = (acc[:, :PAIR] / acc[:, PAIR:]).astype(o_ref.dtype)


def _moba_attention(proj, lay):
    b, seq, _ = proj.shape
    tq = MOBA_BLOCK
    assert seq % MOBA_BLOCK == 0
    nb = seq // MOBA_BLOCK
    nbp = -(-nb // 8) * 8
    group = min(MOBA_GROUP, nb)
    assert nbp <= HEAD_DIM
    assert nb % group == 0
    q_off, width = lay["qb"]
    k_off, v_off = lay["kb"][0], lay["vb"][0]
    lw = MOBA_PAIRS_PER_STEP * PAIR
    assert width % lw == 0 and q_off % lw == 0 and k_off % lw == 0 and v_off % lw == 0
    return pl.pallas_call(
        functools.partial(_moba_kernel, nb=nb, nbp=nbp, group=group),
        grid=(b, width // lw, seq // tq),
        in_specs=[
            pl.BlockSpec((1, tq, lw), lambda bi, p, i: (bi, i, q_off // lw + p)),
            pl.BlockSpec((1, seq, lw), lambda bi, p, i: (bi, 0, k_off // lw + p)),
            pl.BlockSpec((1, seq, lw), lambda bi, p, i: (bi, 0, v_off // lw + p)),
        ],
        out_specs=pl.BlockSpec((1, tq, lw), lambda bi, p, i: (bi, i, p)),
        out_shape=jax.ShapeDtypeStruct((b, seq, width), BF16),
        scratch_shapes=[pltpu.VMEM((nbp, lw), F32)],
        compiler_params=_params("parallel", "parallel", "arbitrary"),
        name="moba_attention",
    )(proj, proj, proj)


def _swa_kernel(sink_ref, q_ref, kp_ref, kc_ref, vp_ref, vc_ref, o_ref, *, n_pairs):
    n = pl.program_id(1)
    w = WINDOW
    k = jnp.concatenate([kp_ref[0], kc_ref[0]], axis=0).astype(F32)
    v = jnp.concatenate([vp_ref[0], vc_ref[0]], axis=0).astype(F32)
    lo_k = lax.broadcasted_iota(jnp.int32, (2 * w, PAIR), 1) < HEAD_DIM
    k_sw = pltpu.roll(k, HEAD_DIM, axis=1)
    v_sw = pltpu.roll(v, HEAD_DIM, axis=1)
    kk = [jnp.where(lo_k, k, k_sw).astype(BF16), jnp.where(lo_k, k_sw, k).astype(BF16)]
    vv = [jnp.where(lo_k, v, v_sw).astype(BF16), jnp.where(lo_k, v_sw, v).astype(BF16)]
    ones = jnp.ones((2 * w, PAIR), BF16)
    rhs = [jnp.concatenate([vg, ones], axis=1) for vg in vv]

    row = lax.broadcasted_iota(jnp.int32, (w, 2 * w), 0)
    col = lax.broadcasted_iota(jnp.int32, (w, 2 * w), 1)
    delta = row + w - col
    valid = (delta >= 0) & (delta < w) & ((n - 1) * w + col >= 0)
    lo_q = _lane_index((w, PAIR)) < HEAD_DIM
    lo_qb = _lane_index((w, PAIR), BF16) < HEAD_DIM
    pairs_per_kv = SWA_GROUP // 2

    for p in range(n_pairs):
        g = p // pairs_per_kv
        qp = q_ref[0, :, p * PAIR:(p + 1) * PAIR]
        zq = jnp.zeros_like(qp)
        outs = []
        for h in (0, 1):
            qh = jnp.where(lo_qb, qp, zq) if h == 0 else jnp.where(lo_qb, zq, qp)
            s = lax.dot_general(qh, kk[g], NT_DIMS, preferred_element_type=F32)
            s = jnp.where(valid, s, NEG)
            sink = sink_ref[2 * p + h]
            m = jnp.maximum(jnp.max(s, axis=1, keepdims=True), sink)
            pr = jnp.exp(s - m).astype(BF16)
            o2 = jnp.dot(pr, rhs[g], preferred_element_type=F32)
            outs.append(o2[:, :PAIR] / (o2[:, PAIR:] + jnp.exp(sink - m)))
        o_ref[0, :, p * PAIR:(p + 1) * PAIR] = jnp.where(lo_q, outs[0], outs[1]).astype(o_ref.dtype)


def _swa_attention(proj, sinks, lay):
    b, seq, _ = proj.shape
    w = WINDOW
    q_off, width = lay["qc"]
    k_off, v_off = lay["kc"][0], lay["vc"][0]
    assert q_off % width == 0 and seq % w == 0
    return pl.pallas_call(
        functools.partial(_swa_kernel, n_pairs=width // PAIR),
        grid=(b, seq // w),
        in_specs=[
            pl.BlockSpec(memory_space=pltpu.SMEM),
            pl.BlockSpec((1, w, width), lambda bi, n: (bi, n, q_off // width)),
            pl.BlockSpec((1, w, PAIR), lambda bi, n: (bi, jnp.maximum(n - 1, 0), k_off // PAIR)),
            pl.BlockSpec((1, w, PAIR), lambda bi, n: (bi, n, k_off // PAIR)),
            pl.BlockSpec((1, w, PAIR), lambda bi, n: (bi, jnp.maximum(n - 1, 0), v_off // PAIR)),
            pl.BlockSpec((1, w, PAIR), lambda bi, n: (bi, n, v_off // PAIR)),
        ],
        out_specs=pl.BlockSpec((1, w, width), lambda bi, n: (bi, n, 0)),
        out_shape=jax.ShapeDtypeStruct((b, seq, width), BF16),
        compiler_params=_params("parallel", "arbitrary"),
        name="swa_attention",
    )(sinks.astype(F32), proj, proj, proj, proj, proj)


def _out_proj_kernel(ya_ref, yb_ref, yc_ref, ga_ref, gb_ref, gc_ref, w_ref, x_ref, *refs, tn, n_cast):
    o_ref, mix_ref = refs[n_cast], refs[-1]
    _cast_slabs(refs[:n_cast], refs[n_cast + 1:-1])
    start = 0
    for y_ref, g_ref in ((ya_ref, ga_ref), (yb_ref, gb_ref), (yc_ref, gc_ref)):
        y = y_ref[...].astype(F32)
        ms = jnp.mean(y * y, axis=-1, keepdims=True)
        width = y.shape[1]
        mix_ref[:, start:start + width] = ((y * lax.rsqrt(ms + EPS)) * g_ref[...]).astype(BF16)
        start += width
    for c in range(o_ref.shape[1] // tn):
        cols = slice(c * tn, (c + 1) * tn)
        o_ref[:, cols] = x_ref[:, cols] + jnp.dot(mix_ref[...], w_ref[:, cols],
                                                  preferred_element_type=F32)


def _out_proj(ya, yb, yc, ga, gb, gc, w, xt, to_cast, layer, tiles):
    n_tok, d = xt.shape
    tm, tn = tiles["tm"], tiles["tn_out"]
    wa, wb, wc = ya.shape[1], yb.shape[1], yc.shape[1]
    mix_w = wa + wb + wc
    steps = n_tok // tm
    cast_in, cast_out, cast_shapes = _cast_specs(to_cast, layer, steps)
    assert d % tn == 0 and w.shape == (mix_w, d)
    return pl.pallas_call(
        functools.partial(_out_proj_kernel, tn=tn, n_cast=len(to_cast)),
        grid=(steps,),
        in_specs=[
            pl.BlockSpec((tm, wa), lambda i: (i, 0)),
            pl.BlockSpec((tm, wb), lambda i: (i, 0)),
            pl.BlockSpec((tm, wc), lambda i: (i, 0)),
            pl.BlockSpec((1, wa), lambda i: (0, 0)),
            pl.BlockSpec((1, wb), lambda i: (0, 0)),
            pl.BlockSpec((1, wc), lambda i: (0, 0)),
            pl.BlockSpec((mix_w, d), lambda i: (0, 0), pipeline_mode=pl.Buffered(1)),
            pl.BlockSpec((tm, d), lambda i: (i, 0)),
        ] + cast_in,
        out_specs=[pl.BlockSpec((tm, d), lambda i: (i, 0))] + cast_out,
        out_shape=[jax.ShapeDtypeStruct((n_tok, d), F32)] + cast_shapes,
        scratch_shapes=[pltpu.VMEM((tm, mix_w), BF16)],
        compiler_params=_params("arbitrary"),
        name="out_proj",
    )(ya, yb, yc, ga.reshape(1, wa), gb.reshape(1, wb), gc.reshape(1, wc), w, xt, *to_cast)


def _mlp_kernel(x_ref, g_ref, wu_ref, wd_ref, gf_ref, o_ref, h_ref, *, final):
    f = pl.program_id(1)

    @pl.when(f == 0)
    def _():
        x = x_ref[...]
        ms = jnp.mean(x * x, axis=-1, keepdims=True)
        h_ref[...] = ((x * lax.rsqrt(ms + EPS)) * g_ref[...]).astype(BF16)
        o_ref[...] = x

    u = jnp.maximum(jnp.dot(h_ref[...], wu_ref[...], preferred_element_type=F32), 0.0)
    o_ref[...] += jnp.dot((u * u).astype(BF16), wd_ref[...], preferred_element_type=F32)

    if final:
        @pl.when(f == pl.num_programs(1) - 1)
        def _():
            y = o_ref[...]
            ms = jnp.mean(y * y, axis=-1, keepdims=True)
            o_ref[...] = (y * lax.rsqrt(ms + EPS)) * gf_ref[...]


def _mlp(xt, gain, w_up, w_down, final_gain, tiles, final):
    n_tok, d = xt.shape
    d_ff = w_up.shape[1]
    tm, tf = tiles["tm_mlp"], tiles["tf"]
    assert d_ff % tf == 0 and n_tok % tm == 0
    return pl.pallas_call(
        functools.partial(_mlp_kernel, final=final),
        grid=(n_tok // tm, d_ff // tf),
        in_specs=[
            pl.BlockSpec((tm, d), lambda i, f: (i, 0)),
            pl.BlockSpec((1, d), lambda i, f: (0, 0)),
            pl.BlockSpec((d, tf), lambda i, f: (0, f)),
            pl.BlockSpec((tf, d), lambda i, f: (f, 0)),
            pl.BlockSpec((1, d), lambda i, f: (0, 0)),
        ],
        out_specs=pl.BlockSpec((tm, d), lambda i, f: (i, 0)),
        out_shape=jax.ShapeDtypeStruct((n_tok, d), F32),
        scratch_shapes=[pltpu.VMEM((tm, d), BF16)],
        compiler_params=_params("parallel", "arbitrary"),
        name="mlp",
    )(xt, gain.reshape(1, d), w_up, w_down, final_gain.reshape(1, d))


def kernel(x, attn_norm, w_in, sinks, gn_sb, gn_moba, gn_swa, w_out, mlp_norm, w_up, w_down, final_norm):
    b, seq, d = x.shape
    depth = w_in.shape[0]
    n_tok = b * seq
    lay = _layout(d)
    tiles = _tiles(n_tok, seq)
    cos_t, sin_t = _rope_tables(seq)
    xt = x.reshape(n_tok, d)
    w_in_l = w_in[0].astype(BF16)
    for l in range(depth):
        proj, w_up_l, w_down_l, w_out_l = _in_proj(xt, attn_norm[l], w_in_l, cos_t, sin_t,
                                                   (w_up, w_down, w_out), l, lay, seq, tiles)
        proj = proj.reshape(b, seq, lay["in_width"])
        ya = _sb_attention(proj, lay).reshape(n_tok, -1)
        yb = _moba_attention(proj, lay).reshape(n_tok, -1)
        yc = _swa_attention(proj, sinks[l], lay).reshape(n_tok, -1)
        next_w_in = (w_in,) if l + 1 < depth else ()
        xt, *cast = _out_proj(ya, yb, yc, gn_sb[l], gn_moba[l], gn_swa[l], w_out_l, xt,
                              next_w_in, l + 1, tiles)
        if cast:
            w_in_l = cast[0]
        xt = _mlp(xt, mlp_norm[l], w_up_l, w_down_l, final_norm, tiles, final=(l == depth - 1))
    return xt.reshape(b, seq, d)
```
